```python
import jax, jax.numpy as jnp
from jax import lax
import numpy as np

D_MODEL = 1024
BATCH = 16
SEQ = 2048
DEPTH = 1

GRID_W = 64
CTX_LEN = 256

HGRN_HEADS = 4
HGRN_DK = 128
HGRN_DV = 128
HGRN_W = HGRN_HEADS * HGRN_DK
GLA_HEADS = 4
GLA_DK = 64
GLA_DV = 128
GLA_WK = GLA_HEADS * GLA_DK
GLA_WV = GLA_HEADS * GLA_DV
GLA_GATE_RANK = 16
GLA_GATE_NORMALIZER = 16.0

CHUNK = 64

IN_SIZES = (HGRN_W, HGRN_W, HGRN_W, HGRN_W, HGRN_W,
            GLA_WK, GLA_WK, GLA_WV, GLA_GATE_RANK, GLA_GATE_RANK, GLA_WV,
            D_MODEL, D_MODEL)
IN_DIM = sum(IN_SIZES)

N_EXPERTS = 32
TOP_K = 4
D_EXPERT = 1024
SWIGLU_LIMIT = 7.0
SWIGLU_ALPHA = 1.702
MOE_BLOCK = 256

NORM_EPS = 1e-6

kernel_name = "hybrid_hgrn2_gla_moe_prefix_dit"


def _rms_norm(x, w):
    xf = x.astype(jnp.float32)
    y = xf * lax.rsqrt(jnp.mean(xf * xf, axis=-1, keepdims=True) + NORM_EPS) * w.astype(jnp.float32)
    return y.astype(x.dtype)


def _split_points():
    pts, acc = [], 0
    for s in IN_SIZES[:-1]:
        acc += s
        pts.append(acc)
    return pts


def _heads(t, n_heads):
    b_, l_, w_ = t.shape
    return t.reshape(b_, l_, n_heads, w_ // n_heads).transpose(0, 2, 1, 3)


def _gated_chunk_scan(q, k, v, log_g, s0):
    b_, h_, l_, _ = q.shape
    dv = v.shape[-1]
    n = l_ // CHUNK

    def to_chunks(t):
        return jnp.moveaxis(t.astype(jnp.float32).reshape(b_, h_, n, CHUNK, t.shape[-1]), 2, 0)

    incl = jnp.tril(jnp.ones((CHUNK, CHUNK), dtype=bool))[:, :, None]

    def step(state, inp):
        qc, kc, vc, gc = inp
        cum = jnp.cumsum(gc, axis=2)
        rel = cum[:, :, :, None, :] - cum[:, :, None, :, :]
        decay = jnp.exp(jnp.where(incl, rel, -jnp.inf))
        scores = jnp.einsum('bhid,bhjd,bhijd->bhij', qc, kc, decay)
        o = (jnp.einsum('bhij,bhjv->bhiv', scores, vc)
             + jnp.einsum('bhid,bhdv->bhiv', qc * jnp.exp(cum), state))
        last = cum[:, :, -1:, :]
        k_dec = kc * jnp.exp(last - cum)
        state = jnp.exp(last[:, :, 0, :, None]) * state + jnp.einsum('bhjd,bhjv->bhdv', k_dec, vc)
        return state, o

    s_fin, o = lax.scan(step, s0, (to_chunks(q), to_chunks(k), to_chunks(v), to_chunks(log_g)))
    o = jnp.moveaxis(o, 0, 2).reshape(b_, h_, l_, dv)
    return o, s_fin


def _zero_state(seq):
    q, v = seq[0], seq[3]
    return jnp.zeros((q.shape[0], q.shape[1], q.shape[3], v.shape[3]), jnp.float32)


def _bidir_scan(seq, s0_f, s0_b):
    q, kf, kb, v, gf, gb = seq
    flip = lambda t: jnp.flip(t, axis=2)
    o_f, s_f = _gated_chunk_scan(q, kf, v, gf, s0_f)
    o_b, s_b = _gated_chunk_scan(flip(q), flip(kb), flip(v), flip(gb), s0_b)
    return o_f + flip(o_b), s_f, s_b


def _hgrn2_inputs(q, z_f, z_b, i, lb):
    def gates(z, lb_d):
        z = z.astype(jnp.float32)
        log_f = jnp.log(lb_d + (1.0 - lb_d) * jax.nn.sigmoid(z))
        key = (1.0 - lb_d) * jax.nn.sigmoid(-z)
        return _heads(key, HGRN_HEADS), _heads(log_f, HGRN_HEADS)
    kf, gf = gates(z_f, lb[0])
    kb, gb = gates(z_b, lb[1])
    return (_heads(q, HGRN_HEADS), kf, kb, _heads(i, HGRN_HEADS), gf, gb)


def _gla_inputs(q, k, v, r_f, r_b, gk_w2, gk_b):
    gf = jax.nn.log_sigmoid((r_f @ gk_w2[0] + gk_b[0]).astype(jnp.float32)) / GLA_GATE_NORMALIZER
    gb = jax.nn.log_sigmoid((r_b @ gk_w2[1] + gk_b[1]).astype(jnp.float32)) / GLA_GATE_NORMALIZER
    kh = _heads(k, GLA_HEADS)
    return (_heads(q * (GLA_DK ** -0.5), GLA_HEADS), kh, kh, _heads(v, GLA_HEADS),
            _heads(gf, GLA_HEADS), _heads(gb, GLA_HEADS))


def _head_readout(o, norm_w, gate):
    o = o * lax.rsqrt(jnp.mean(o * o, axis=-1, keepdims=True) + NORM_EPS) * norm_w.astype(jnp.float32)
    b_, h_, l_, dv = o.shape
    o = o.transpose(0, 2, 1, 3).reshape(b_, l_, h_ * dv)
    return (o * jax.nn.silu(gate.astype(jnp.float32))).astype(gate.dtype)


def _hybrid_mixer(h_lat, h_ctx, need_ctx_out, w_in, lb, hgrn_norm_w, gla_gk_w2, gla_gk_b,
                  gla_norm_w, w_up_a, w_up_b, w_o):
    split_at = _split_points()

    def prepare(h):
        (qa, za_f, za_b, ia, oga, qb, kb, vb, rb_f, rb_b, ogb, mga, mgb) = jnp.split(h @ w_in, split_at, axis=-1)
        seq_a = _hgrn2_inputs(qa, za_f, za_b, ia, lb)
        seq_b = _gla_inputs(qb, kb, vb, rb_f, rb_b, gla_gk_w2, gla_gk_b)
        return seq_a, seq_b, (oga, ogb, mga, mgb)

    def merge(o_a, o_b, gates):
        oga, ogb, mga, mgb = gates
        y_a = _head_readout(o_a, hgrn_norm_w, oga) @ w_up_a
        y_b = _head_readout(o_b, gla_norm_w, ogb) @ w_up_b
        return (jax.nn.sigmoid(mga) * y_a + jax.nn.sigmoid(mgb) * y_b) @ w_o

    ctx_a, ctx_b, ctx_gates = prepare(h_ctx)
    oc_a, sa_f, sa_b = _bidir_scan(ctx_a, _zero_state(ctx_a), _zero_state(ctx_a))
    oc_b, sb_f, sb_b = _bidir_scan(ctx_b, _zero_state(ctx_b), _zero_state(ctx_b))
    lat_a, lat_b, lat_gates = prepare(h_lat)
    ol_a, _, _ = _bidir_scan(lat_a, sa_f, sa_b)
    ol_b, _, _ = _bidir_scan(lat_b, sb_f, sb_b)
    out_lat = merge(ol_a, ol_b, lat_gates)
    out_ctx = merge(oc_a, oc_b, ctx_gates) if need_ctx_out else None
    return out_lat, out_ctx


def _moe(xt, w_router, b_router, w_gate, b_gate, w_up, b_up, w_down, b_down):
    t_ = xt.shape[0]
    logits = (xt @ w_router).astype(jnp.float32) + b_router.astype(jnp.float32)
    top_val, top_idx = lax.top_k(logits, TOP_K)
    probs = jax.nn.softmax(top_val, axis=-1)
    n_pairs = t_ * TOP_K
    flat_e = top_idx.reshape(-1)
    flat_tok = jnp.arange(n_pairs, dtype=jnp.int32) // TOP_K
    flat_w = probs.reshape(-1)
    order = jnp.argsort(flat_e)
    e_sorted, tok_sorted, w_sorted = flat_e[order], flat_tok[order], flat_w[order]
    counts = jnp.zeros((N_EXPERTS,), jnp.int32).at[flat_e].add(1)
    padded = (counts + MOE_BLOCK - 1) // MOE_BLOCK * MOE_BLOCK
    padded_end = jnp.cumsum(padded)
    padded_start = padded_end - padded
    start = jnp.cumsum(counts) - counts
    rank = jnp.arange(n_pairs, dtype=jnp.int32) - start[e_sorted]
    dest = padded_start[e_sorted] + rank
    n_blocks = -(-n_pairs // MOE_BLOCK) + N_EXPERTS
    p_len = n_blocks * MOE_BLOCK
    tok_pad = jnp.full((p_len,), t_, jnp.int32).at[dest].set(tok_sorted)
    w_pad = jnp.zeros((p_len,), jnp.float32).at[dest].set(w_sorted)
    block_expert = jnp.minimum(
        jnp.searchsorted(padded_end, jnp.arange(n_blocks, dtype=jnp.int32) * MOE_BLOCK, side='right'),
        N_EXPERTS - 1)
    xt_pad = jnp.concatenate([xt, jnp.zeros((1, xt.shape[1]), xt.dtype)], axis=0)

    def expert_block(args):
        tok_b, e = args
        xb = xt_pad[tok_b]
        g = jnp.minimum(xb @ w_gate[e] + b_gate[e], SWIGLU_LIMIT)
        u = jnp.clip(xb @ w_up[e] + b_up[e], -SWIGLU_LIMIT, SWIGLU_LIMIT)
        act = g * jax.nn.sigmoid(SWIGLU_ALPHA * g) * (u + 1.0)
        return act @ w_down[e] + b_down[e]

    ys = lax.map(expert_block, (tok_pad.reshape(n_blocks, MOE_BLOCK), block_expert))
    ys = ys.reshape(p_len, -1) * w_pad[:, None].astype(ys.dtype)
    return jax.ops.segment_sum(ys, tok_pad, num_segments=t_ + 1)[:t_]


def setup_inputs(seed: int = 0) -> dict:
    key = jax.random.key(seed)
    ks = jax.random.split(key, 27)
    nrm = lambda k, shape, s: jax.random.normal(k, shape, jnp.float32) * s
    gain = lambda k, shape: 1.0 + 0.02 * jax.random.normal(k, shape, jnp.float32)
    D, E, F = D_MODEL, N_EXPERTS, D_EXPERT
    return {
        "x": nrm(ks[0], (BATCH, SEQ, D), 1.0),
        "c": nrm(ks[1], (BATCH, D), 1.0),
        "ctx": nrm(ks[2], (BATCH, CTX_LEN, D), 1.0),
        "c_ctx": nrm(ks[3], (D,), 1.0),
        "w_ada": nrm(ks[4], (DEPTH, D, 6 * D), 0.5 * D ** -0.5),
        "b_ada": nrm(ks[5], (DEPTH, 6 * D), 0.01),
        "g_pre_mix": gain(ks[6], (DEPTH, D)),
        "g_post_mix": gain(ks[7], (DEPTH, D)),
        "g_pre_ffn": gain(ks[8], (DEPTH, D)),
        "g_post_ffn": gain(ks[9], (DEPTH, D)),
        "w_in": nrm(ks[10], (DEPTH, D, IN_DIM), D ** -0.5),
        "hgrn_lb": nrm(ks[11], (DEPTH + 1, 2, HGRN_W), 0.1),
        "hgrn_norm_w": gain(ks[12], (DEPTH, HGRN_DV)),
        "gla_gk_w2": nrm(ks[13], (DEPTH, 2, GLA_GATE_RANK, GLA_WK), GLA_GATE_RANK ** -0.5),
        "gla_gk_b": nrm(ks[14], (DEPTH, 2, GLA_WK), 0.1),
        "gla_norm_w": gain(ks[15], (DEPTH, GLA_DV)),
        "w_up_a": nrm(ks[16], (DEPTH, HGRN_W, D), HGRN_W ** -0.5),
        "w_up_b": nrm(ks[17], (DEPTH, GLA_WV, D), GLA_WV ** -0.5),
        "w_o": nrm(ks[18], (DEPTH, D, D), D ** -0.5),
        "w_router": nrm(ks[19], (DEPTH, D, E), D ** -0.5),
        "b_router": nrm(ks[20], (DEPTH, E), 0.01),
        "w_gate": nrm(ks[21], (DEPTH, E, D, F), D ** -0.5),
        "b_gate": nrm(ks[22], (DEPTH, E, F), 0.01),
        "w_up": nrm(ks[23], (DEPTH, E, D, F), D ** -0.5),
        "b_up": nrm(ks[24], (DEPTH, E, F), 0.01),
        "w_down": nrm(ks[25], (DEPTH, E, F, D), F ** -0.5),
        "b_down": nrm(ks[26], (DEPTH, E, D), 0.01),
    }


def reference(x, c, ctx, c_ctx, w_ada, b_ada, g_pre_mix, g_post_mix, g_pre_ffn, g_post_ffn,
              w_in, hgrn_lb, hgrn_norm_w, gla_gk_w2, gla_gk_b, gla_norm_w, w_up_a, w_up_b, w_o,
              w_router, b_router, w_gate, b_gate, w_up, b_up, w_down, b_down):
    lb_all = jnp.cumsum(jax.nn.softmax(hgrn_lb.astype(jnp.float32), axis=0), axis=0)
    b_, l_, d_ = x.shape
    for layer in range(DEPTH):
        last = layer == DEPTH - 1
        mod = jax.nn.silu(c) @ w_ada[layer] + b_ada[layer]
        sh1, sc1, gt1, sh2, sc2, gt2 = jnp.split(mod[:, None, :], 6, axis=-1)
        cmod = jnp.split(jax.nn.silu(c_ctx) @ w_ada[layer] + b_ada[layer], 6)

        h = _rms_norm(x, g_pre_mix[layer]) * (1.0 + sc1) + sh1
        hc = _rms_norm(ctx, g_pre_mix[layer]) * (1.0 + cmod[1]) + cmod[0]
        mix_l, mix_c = _hybrid_mixer(h, hc, not last, w_in[layer], lb_all[layer], hgrn_norm_w[layer],
                                     gla_gk_w2[layer], gla_gk_b[layer], gla_norm_w[layer],
                                     w_up_a[layer], w_up_b[layer], w_o[layer])
        x = x + gt1 * _rms_norm(mix_l, g_post_mix[layer])

        h2 = _rms_norm(x, g_pre_ffn[layer]) * (1.0 + sc2) + sh2
        ffn = _moe(h2.reshape(b_ * l_, d_), w_router[layer], b_router[layer], w_gate[layer], b_gate[layer],
                   w_up[layer], b_up[layer], w_down[layer], b_down[layer]).reshape(b_, l_, d_)
        x = x + gt2 * _rms_norm(ffn, g_post_ffn[layer])

        if not last:
            ctx = ctx + cmod[2] * _rms_norm(mix_c, g_post_mix[layer])
            hc2 = _rms_norm(ctx, g_pre_ffn[layer]) * (1.0 + cmod[4]) + cmod[3]
            lc = ctx.shape[1]
            ffn_c = _moe(hc2.reshape(b_ * lc, d_), w_router[layer], b_router[layer], w_gate[layer], b_gate[layer],
                         w_up[layer], b_up[layer], w_down[layer], b_down[layer]).reshape(b_, lc, d_)
            ctx = ctx + cmod[5] * _rms_norm(ffn_c, g_post_ffn[layer])
    return x
```

```python
import functools

import jax
import jax.numpy as jnp
from jax import lax
from jax.experimental import pallas as pl
from jax.experimental.pallas import tpu as pltpu

F32 = jnp.float32
BF16 = jnp.bfloat16

NORM_EPS = 1e-6
HGRN_HEADS = 4
HGRN_DK = 128
GLA_HEADS = 4
GLA_DK = 64
GLA_DV = 128
GLA_GATE_RANK = 16
GLA_GATE_NORMALIZER = 16.0
N_HEADS = HGRN_HEADS + GLA_HEADS
HEAD_W = 128
TOP_K = 4
SWIGLU_LIMIT = 7.0
SWIGLU_ALPHA = 1.702
MOE_BLOCK = 256
SCAN_CHUNK = 64
TOKEN_TILE = 256
LANES = 128
VMEM_LIMIT = 56 * 1024 * 1024


def _dot(a, b):
    return jnp.dot(a, b, preferred_element_type=F32)


def _dot_nt(a, b):
    return lax.dot_general(a, b, (((1,), (1,)), ((), ())), preferred_element_type=F32)


def _dot_tn(a, b):
    return lax.dot_general(a, b, (((0,), (0,)), ((), ())), preferred_element_type=F32)


def _split_bf16(a):
    hi = a.astype(BF16)
    lo = (a - hi.astype(F32)).astype(BF16)
    return hi, lo


def _dot3(a, b):
    ah, al = _split_bf16(a)
    bh, bl = _split_bf16(b)
    return _dot(ah, bh) + _dot(al, bh) + _dot(ah, bl)


def _rms(x):
    return x * lax.rsqrt(jnp.mean(x * x, axis=-1, keepdims=True) + NORM_EPS)


def _sigmoid(x):
    return 1.0 / (1.0 + jnp.exp(-x))


def _ada_kernel(c_ref, w_ref, b_ref, o_ref):
    c = c_ref[...]
    o_ref[...] = _dot3(c * _sigmoid(c), w_ref[...]) + b_ref[...]


def _ada_call(cc, w_ada, b_ada):
    rows, d = cc.shape
    n = w_ada.shape[1]
    tn = 1536
    return pl.pallas_call(
        _ada_kernel,
        grid=(n // tn,),
        in_specs=[
            pl.BlockSpec((rows, d), lambda j: (0, 0)),
            pl.BlockSpec((d, tn), lambda j: (0, j)),
            pl.BlockSpec((1, tn), lambda j: (0, j)),
        ],
        out_specs=pl.BlockSpec((rows, tn), lambda j: (0, j)),
        out_shape=jax.ShapeDtypeStruct((rows, n), F32),
        compiler_params=pltpu.CompilerParams(
            dimension_semantics=("arbitrary",), vmem_limit_bytes=VMEM_LIMIT),
    )(cc, w_ada, b_ada.reshape(1, n))


def _inproj_kernel(n_ctx_tiles, x_ref, ctx_ref, a_ref, s_ref, w_ref, wr_ref, w2_ref, gkb_ref, lb_ref,
                   q_ref, v_ref, akf_ref, akb_ref, bk_ref, gf_ref, gb_ref):
    t = pl.program_id(1)
    xt = jnp.where(t < n_ctx_tiles, ctx_ref[0], x_ref[0])
    hb = (_rms(xt) * a_ref[0] + s_ref[0]).astype(BF16)

    def seg(j):
        return _dot(hb, w_ref[:, j * 512:(j + 1) * 512])

    q_ref[0, :, 0:512] = seg(0).astype(BF16)
    q_ref[0, :, 512:1024] = seg(4).astype(BF16)
    v_ref[0, :, 0:512] = seg(3).astype(BF16)
    v_ref[0, :, 512:1024] = seg(6).astype(BF16)
    bk_ref[0] = seg(5).astype(BF16)
    for j, k_ref, g_ref in ((1, akf_ref, gf_ref), (2, akb_ref, gb_ref)):
        z = seg(j)
        lb = lb_ref[j - 1:j, :]
        g_ref[0, :, 0:512] = jnp.log(lb + (1.0 - lb) * _sigmoid(z))
        k_ref[0] = ((1.0 - lb) * _sigmoid(-z)).astype(BF16)
    r = _dot(hb, wr_ref[...]).astype(BF16)
    pre = _dot(r, w2_ref[...]) + gkb_ref[...]
    ls = (jnp.minimum(pre, 0.0) - jnp.log(1.0 + jnp.exp(-jnp.abs(pre)))) * (1.0 / GLA_GATE_NORMALIZER)
    gf_ref[0, :, 512:1024] = ls[:, 0:512]
    gb_ref[0, :, 512:1024] = ls[:, 512:1024]


def _inproj_call(x, ctx, mod_a, mod_s, w_scan, w_rank, w2, gkb, lb):
    b_, l_, d = x.shape
    lc = ctx.shape[1]
    tm = TOKEN_TILE
    nct, nlt = lc // tm, l_ // tm
    lt = lc + l_

    def x_map(b, t):
        return (b, jnp.maximum(t - nct, 0), 0)

    def ctx_map(b, t):
        return (b, jnp.minimum(t, nct - 1), 0)

    def mod_map(b, t):
        return (jnp.where(t < nct, b_, b), 0, 0)

    const2 = lambda b, t: (0, 0)
    out_map = lambda b, t: (b, t, 0)
    wide = jax.ShapeDtypeStruct((b_, lt, 1024), BF16)
    half = jax.ShapeDtypeStruct((b_, lt, 512), BF16)
    gate = jax.ShapeDtypeStruct((b_, lt, 1024), F32)
    return pl.pallas_call(
        functools.partial(_inproj_kernel, nct),
        grid=(b_, nct + nlt),
        in_specs=[
            pl.BlockSpec((1, tm, d), x_map),
            pl.BlockSpec((1, tm, d), ctx_map),
            pl.BlockSpec((1, 1, d), mod_map),
            pl.BlockSpec((1, 1, d), mod_map),
            pl.BlockSpec(w_scan.shape, const2),
            pl.BlockSpec(w_rank.shape, const2),
            pl.BlockSpec(w2.shape, const2),
            pl.BlockSpec(gkb.shape, const2),
            pl.BlockSpec(lb.shape, const2),
        ],
        out_specs=[
            pl.BlockSpec((1, tm, 1024), out_map),
            pl.BlockSpec((1, tm, 1024), out_map),
            pl.BlockSpec((1, tm, 512), out_map),
            pl.BlockSpec((1, tm, 512), out_map),
            pl.BlockSpec((1, tm, 512), out_map),
            pl.BlockSpec((1, tm, 1024), out_map),
            pl.BlockSpec((1, tm, 1024), out_map),
        ],
        out_shape=[wide, wide, half, half, half, gate, gate],
        compiler_params=pltpu.CompilerParams(
            dimension_semantics=("arbitrary", "arbitrary"), vmem_limit_bytes=VMEM_LIMIT),
    )(x, ctx, mod_a, mod_s, w_scan, w_rank, w2, gkb, lb)


def _scan_direction(forward, q_ref, v_ref, ak_ref, bk_ref, g_ref, o_ref, st_ref):
    c_ = SCAN_CHUNK
    row = lax.broadcasted_iota(jnp.int32, (c_, c_), 0)
    col = lax.broadcasted_iota(jnp.int32, (c_, c_), 1)
    causal = (row >= col) if forward else (col >= row)
    tri = jnp.where(causal, 1.0, 0.0).astype(BF16)
    g_hi, g_lo = _split_bf16(g_ref[0])
    cum = _dot(tri, g_hi) + _dot(tri, g_lo)
    end_row, mid_row = (c_ - 1, c_ // 2 - 1) if forward else (0, c_ // 2)
    total = cum[end_row:end_row + 1, :]
    mid = cum[mid_row:mid_row + 1, :]
    e_in = jnp.exp(cum)
    e_out = jnp.exp(total - cum)
    e_q = jnp.exp(cum - mid)
    e_k = jnp.exp(mid - cum)
    e_tot = jnp.exp(total)
    for h in range(N_HEADS):
        sl = slice(h * HEAD_W, (h + 1) * HEAD_W)
        hk = slice((h % HGRN_HEADS) * HEAD_W, (h % HGRN_HEADS + 1) * HEAD_W)
        q = q_ref[0, :, sl].astype(F32)
        k = (ak_ref if h < HGRN_HEADS else bk_ref)[0, :, hk].astype(F32)
        v = v_ref[0, :, sl]
        scores = _dot_nt((q * e_q[:, sl]).astype(BF16), (k * e_k[:, sl]).astype(BF16))
        scores = jnp.where(causal, scores, 0.0).astype(BF16)
        st = st_ref[h]
        o = _dot(scores, v) + _dot_nt((q * e_in[:, sl]).astype(BF16), st.astype(BF16))
        o_ref[0, :, sl] = o.astype(o_ref.dtype)
        st_ref[h] = st * e_tot[:, sl] + _dot_tn(v, (k * e_out[:, sl]).astype(BF16))


def _scan_kernel(qf_ref, vf_ref, akf_ref, bkf_ref, gf_ref, qb_ref, vb_ref, akb_ref, bkb_ref, gb_ref,
                 of_ref, ob_ref, sf_ref, sb_ref):
    @pl.when(pl.program_id(1) == 0)
    def _():
        sf_ref[...] = jnp.zeros_like(sf_ref)
        sb_ref[...] = jnp.zeros_like(sb_ref)

    _scan_direction(True, qf_ref, vf_ref, akf_ref, bkf_ref, gf_ref, of_ref, sf_ref)
    _scan_direction(False, qb_ref, vb_ref, akb_ref, bkb_ref, gb_ref, ob_ref, sb_ref)


def _scan_call(q, v, akf, akb, bk, gf, gb, lc):
    b_, lt, _ = q.shape
    c_ = SCAN_CHUNK
    ncc = lc // c_
    nlc = (lt - lc) // c_
    n = ncc + nlc

    def fwd(b, s):
        return (b, s, 0)

    def bwd(b, s):
        return (b, jnp.where(s < ncc, ncc - 1 - s, n + ncc - 1 - s), 0)

    def out_fwd(b, s):
        return (b, jnp.maximum(s - ncc, 0), 0)

    def out_bwd(b, s):
        return (b, jnp.where(s < ncc, nlc - 1, n - 1 - s), 0)

    def specs(index_map):
        return [
            pl.BlockSpec((1, c_, 1024), index_map),
            pl.BlockSpec((1, c_, 1024), index_map),
            pl.BlockSpec((1, c_, 512), index_map),
            pl.BlockSpec((1, c_, 512), index_map),
            pl.BlockSpec((1, c_, 1024), index_map),
        ]

    out = jax.ShapeDtypeStruct((b_, lt - lc, 1024), BF16)
    return pl.pallas_call(
        _scan_kernel,
        grid=(b_, n),
        in_specs=specs(fwd) + specs(bwd),
        out_specs=[pl.BlockSpec((1, c_, 1024), out_fwd), pl.BlockSpec((1, c_, 1024), out_bwd)],
        out_shape=[out, out],
        scratch_shapes=[pltpu.VMEM((N_HEADS, HEAD_W, HEAD_W), F32),
                        pltpu.VMEM((N_HEADS, HEAD_W, HEAD_W), F32)],
        compiler_params=pltpu.CompilerParams(
            dimension_semantics=("arbitrary", "arbitrary"), vmem_limit_bytes=VMEM_LIMIT),
    )(q, v, akf, bk, gf, q, v, akb, bk, gb)


def _merge_kernel(x_ref, a1_ref, s1_ref, gt1_ref, a2_ref, s2_ref, of_ref, ob_ref, wg_ref, nw_ref,
                  wua_ref, wub_ref, wo_ref, gpost_ref, wr_ref, br_ref,
                  x1_ref, h2_ref, idx_ref, prob_ref):
    x = x_ref[0]
    hb = (_rms(x) * a1_ref[0] + s1_ref[0]).astype(BF16)
    o = of_ref[0].astype(F32) + ob_ref[0].astype(F32)
    og = _dot(hb, wg_ref[:, 0:1024])
    heads = []
    for h in range(N_HEADS):
        sl = slice(h * HEAD_W, (h + 1) * HEAD_W)
        heads.append(_rms(o[:, sl]))
    og = og * _sigmoid(og)
    r = (jnp.concatenate(heads, axis=-1) * nw_ref[...] * og).astype(BF16)
    y_a = _dot(r[:, 0:512], wua_ref[...])
    y_b = _dot(r[:, 512:1024], wub_ref[...])
    mg_a = _sigmoid(_dot(hb, wg_ref[:, 1024:2048]))
    mg_b = _sigmoid(_dot(hb, wg_ref[:, 2048:3072]))
    mix = _dot((mg_a * y_a + mg_b * y_b).astype(BF16), wo_ref[...])
    x1 = x + gt1_ref[0] * (_rms(mix) * gpost_ref[...])
    x1_ref[0] = x1
    h2 = _rms(x1) * a2_ref[0] + s2_ref[0]
    h2_ref[0] = h2
    logits = _dot3(h2, wr_ref[...]) + br_ref[...]
    lane = lax.broadcasted_iota(jnp.int32, logits.shape, 1).astype(F32)
    vals, idxs = [], []
    for _ in range(TOP_K):
        m = jnp.max(logits, axis=-1, keepdims=True)
        sel = jnp.min(jnp.where(logits == m, lane, float(LANES)), axis=-1, keepdims=True)
        vals.append(m)
        idxs.append(sel)
        logits = jnp.where(lane == sel, -jnp.inf, logits)
    exps = [jnp.exp(v_ - vals[0]) for v_ in vals]
    denom = exps[0] + exps[1] + exps[2] + exps[3]
    idx_out = jnp.zeros_like(lane)
    prob_out = jnp.zeros_like(lane)
    for k_ in range(TOP_K):
        idx_out = jnp.where(lane == float(k_), idxs[k_], idx_out)
        prob_out = jnp.where(lane == float(k_), exps[k_] / denom, prob_out)
    idx_ref[0] = idx_out.astype(jnp.int32)
    prob_ref[0] = prob_out


def _merge_call(x, a1, s1, gt1, a2, s2, o_f, o_b, w_gates, norm_w, w_up_a, w_up_b, w_o, g_post, w_r, b_r):
    b_, l_, d = x.shape
    tm = TOKEN_TILE
    tile = lambda b, t: (b, t, 0)
    per_b = lambda b, t: (b, 0, 0)
    const2 = lambda b, t: (0, 0)
    return pl.pallas_call(
        _merge_kernel,
        grid=(b_, l_ // tm),
        in_specs=[
            pl.BlockSpec((1, tm, d), tile),
            pl.BlockSpec((1, 1, d), per_b), pl.BlockSpec((1, 1, d), per_b), pl.BlockSpec((1, 1, d), per_b),
            pl.BlockSpec((1, 1, d), per_b), pl.BlockSpec((1, 1, d), per_b),
            pl.BlockSpec((1, tm, 1024), tile), pl.BlockSpec((1, tm, 1024), tile),
            pl.BlockSpec(w_gates.shape, const2),
            pl.BlockSpec(norm_w.shape, const2),
            pl.BlockSpec(w_up_a.shape, const2),
            pl.BlockSpec(w_up_b.shape, const2),
            pl.BlockSpec(w_o.shape, const2),
            pl.BlockSpec(g_post.shape, const2),
            pl.BlockSpec(w_r.shape, const2),
            pl.BlockSpec(b_r.shape, const2),
        ],
        out_specs=[
            pl.BlockSpec((1, tm, d), tile), pl.BlockSpec((1, tm, d), tile),
            pl.BlockSpec((1, tm, LANES), tile), pl.BlockSpec((1, tm, LANES), tile),
        ],
        out_shape=[
            jax.ShapeDtypeStruct((b_, l_, d), F32), jax.ShapeDtypeStruct((b_, l_, d), F32),
            jax.ShapeDtypeStruct((b_, l_, LANES), jnp.int32), jax.ShapeDtypeStruct((b_, l_, LANES), F32),
        ],
        compiler_params=pltpu.CompilerParams(
            dimension_semantics=("arbitrary", "arbitrary"), vmem_limit_bytes=VMEM_LIMIT),
    )(x, a1, s1, gt1, a2, s2, o_f, o_b, w_gates, norm_w, w_up_a, w_up_b, w_o, g_post, w_r, b_r)


def _moe_kernel(be_ref, nv_ref, tok_ref, tokn_ref, pair_ref, h2_hbm,
                wg_ref, bg_ref, wu_ref, bu_ref, wd_ref, bd_ref, out_hbm,
                xbuf, ybuf, wgb, wub, wdb, gsem, ssem):
    i = pl.program_id(0)
    n = pl.num_programs(0)
    slot = i % 2
    rows = MOE_BLOCK

    def gather_copy(idx_ref, r, s):
        return pltpu.make_async_copy(h2_hbm.at[pl.ds(idx_ref[0, 0, r], 1), :],
                                     xbuf.at[s, pl.ds(r, 1), :], gsem.at[s])

    def scatter_copy(r, s):
        return pltpu.make_async_copy(ybuf.at[s, pl.ds(r, 1), :],
                                     out_hbm.at[pl.ds(pair_ref[0, 0, r], 1), :], ssem.at[s])

    def scatter_wait(count, s):
        def body(r, carry):
            pltpu.make_async_copy(ybuf.at[s, pl.ds(0, 1), :], out_hbm.at[pl.ds(0, 1), :], ssem.at[s]).wait()
            return carry
        lax.fori_loop(0, count, body, 0)

    def issue_gather(idx_ref, s):
        def body(r, carry):
            gather_copy(idx_ref, r, s).start()
            return carry
        lax.fori_loop(0, rows, body, 0)

    @pl.when(jnp.logical_and(i == 0, nv_ref[0] > 0))
    def _():
        issue_gather(tok_ref, 0)

    nxt = jnp.minimum(i + 1, n - 1)

    @pl.when(jnp.logical_and(i + 1 < n, nv_ref[nxt] > 0))
    def _():
        issue_gather(tokn_ref, 1 - slot)

    prev = jnp.maximum(i - 1, 0)

    @pl.when(jnp.logical_or(i == 0, be_ref[i] != be_ref[prev]))
    def _():
        wgb[...] = wg_ref[0].astype(BF16)
        wub[...] = wu_ref[0].astype(BF16)
        wdb[...] = wd_ref[0].astype(BF16)

    @pl.when(i >= 2)
    def _():
        scatter_wait(nv_ref[jnp.maximum(i - 2, 0)], slot)

    @pl.when(nv_ref[i] > 0)
    def _():
        def wait_body(r, carry):
            gather_copy(tok_ref, r, slot).wait()
            return carry
        lax.fori_loop(0, rows, wait_body, 0)
        xb = xbuf[slot].astype(BF16)
        g = jnp.minimum(_dot(xb, wgb[...]) + bg_ref[0], SWIGLU_LIMIT)
        u = jnp.clip(_dot(xb, wub[...]) + bu_ref[0], -SWIGLU_LIMIT, SWIGLU_LIMIT)
        act = g * _sigmoid(SWIGLU_ALPHA * g) * (u + 1.0)
        ybuf[slot] = _dot(act.astype(BF16), wdb[...]) + bd_ref[0]

        def scatter_body(r, carry):
            scatter_copy(r, slot).start()
            return carry
        lax.fori_loop(0, nv_ref[i], scatter_body, 0)

    @pl.when(i == n - 1)
    def _():
        @pl.when(i >= 1)
        def _():
            scatter_wait(nv_ref[prev], 1 - slot)
        scatter_wait(nv_ref[i], slot)


def _moe_call(block_expert, n_valid, tok, pair, h2, w_gate, b_gate, w_up, b_up, w_down, b_down):
    t_, d = h2.shape
    e_, _, f = w_gate.shape
    nblk = tok.shape[0]
    rows = MOE_BLOCK

    def blk(i, be, nv):
        return (i, 0, 0)

    def blk_next(i, be, nv):
        return (jnp.minimum(i + 1, nblk - 1), 0, 0)

    def expert(i, be, nv):
        return (be[i], 0, 0)

    smem_rows = functools.partial(pl.BlockSpec, (1, 1, rows), memory_space=pltpu.SMEM)
    grid_spec = pltpu.PrefetchScalarGridSpec(
        num_scalar_prefetch=2,
        grid=(nblk,),
        in_specs=[
            smem_rows(blk), smem_rows(blk_next), smem_rows(blk),
            pl.BlockSpec(memory_space=pl.ANY),
            pl.BlockSpec((1, d, f), expert), pl.BlockSpec((1, 1, f), expert),
            pl.BlockSpec((1, d, f), expert), pl.BlockSpec((1, 1, f), expert),
            pl.BlockSpec((1, f, d), expert), pl.BlockSpec((1, 1, d), expert),
        ],
        out_specs=pl.BlockSpec(memory_space=pl.ANY),
        scratch_shapes=[
            pltpu.VMEM((2, rows, d), F32), pltpu.VMEM((2, rows, d), F32),
            pltpu.VMEM((d, f), BF16), pltpu.VMEM((d, f), BF16), pltpu.VMEM((f, d), BF16),
            pltpu.SemaphoreType.DMA((2,)), pltpu.SemaphoreType.DMA((2,)),
        ],
    )
    return pl.pallas_call(
        _moe_kernel,
        grid_spec=grid_spec,
        out_shape=jax.ShapeDtypeStruct((t_ * TOP_K, d), F32),
        compiler_params=pltpu.CompilerParams(
            dimension_semantics=("arbitrary",), vmem_limit_bytes=VMEM_LIMIT),
    )(block_expert, n_valid, tok, tok, pair, h2,
      w_gate, b_gate.reshape(e_, 1, f), w_up, b_up.reshape(e_, 1, f), w_down, b_down.reshape(e_, 1, d))


def _dispatch_plan(top_idx, n_experts):
    t_ = top_idx.shape[0]
    n_pairs = t_ * TOP_K
    flat_e = top_idx.reshape(-1)
    order = jnp.argsort(flat_e).astype(jnp.int32)
    counts = jnp.sum((flat_e[:, None] == jnp.arange(n_experts, dtype=jnp.int32)[None, :]).astype(jnp.int32), axis=0)
    nblk_e = (counts + MOE_BLOCK - 1) // MOE_BLOCK
    blk_end = jnp.cumsum(nblk_e)
    blk_start = blk_end - nblk_e
    start = jnp.cumsum(counts) - counts
    nblk = -(-n_pairs // MOE_BLOCK) + n_experts
    b = jnp.arange(nblk, dtype=jnp.int32)
    be = jnp.minimum(jnp.searchsorted(blk_end, b, side='right'), n_experts - 1).astype(jnp.int32)
    first_row = (b - blk_start[be]) * MOE_BLOCK
    n_valid = jnp.clip(counts[be] - first_row, 0, MOE_BLOCK).astype(jnp.int32)
    r = jnp.arange(MOE_BLOCK, dtype=jnp.int32)[None, :]
    src = jnp.clip(start[be][:, None] + first_row[:, None] + r, 0, n_pairs - 1)
    pair = jnp.where(r < n_valid[:, None], order[src], 0).astype(jnp.int32)
    tok = pair // TOP_K
    return be, n_valid, tok.reshape(nblk, 1, MOE_BLOCK), pair.reshape(nblk, 1, MOE_BLOCK)


def _final_kernel(x1_ref, y_ref, p_ref, gt2_ref, gpost_ref, o_ref):
    d = x1_ref.shape[-1]
    p = p_ref[0]
    ffn = p[:, 0:1] * y_ref[0, :, 0:d]
    for k_ in range(1, TOP_K):
        ffn = ffn + p[:, k_:k_ + 1] * y_ref[0, :, k_ * d:(k_ + 1) * d]
    o_ref[0] = x1_ref[0] + gt2_ref[0] * (_rms(ffn) * gpost_ref[...])


def _final_call(x1, y4, probs, gt2, g_post):
    b_, l_, d = x1.shape
    tm = TOKEN_TILE
    tile = lambda b, t: (b, t, 0)
    return pl.pallas_call(
        _final_kernel,
        grid=(b_, l_ // tm),
        in_specs=[
            pl.BlockSpec((1, tm, d), tile),
            pl.BlockSpec((1, tm, TOP_K * d), tile),
            pl.BlockSpec((1, tm, LANES), tile),
            pl.BlockSpec((1, 1, d), lambda b, t: (b, 0, 0)),
            pl.BlockSpec(g_post.shape, lambda b, t: (0, 0)),
        ],
        out_specs=pl.BlockSpec((1, tm, d), tile),
        out_shape=jax.ShapeDtypeStruct((b_, l_, d), F32),
        compiler_params=pltpu.CompilerParams(
            dimension_semantics=("arbitrary", "arbitrary"), vmem_limit_bytes=VMEM_LIMIT),
    )(x1, y4, probs, gt2, g_post)


def _pad_heads(w, n_heads, dk):
    d = w.shape[0]
    w = w.reshape(d, n_heads, dk)
    return jnp.pad(w, ((0, 0), (0, 0), (0, HEAD_W - dk))).reshape(d, n_heads * HEAD_W)


def kernel(x, c, ctx, c_ctx, w_ada, b_ada, g_pre_mix, g_post_mix, g_pre_ffn, g_post_ffn, w_in, hgrn_lb, hgrn_norm_w, gla_gk_w2, gla_gk_b, gla_norm_w, w_up_a, w_up_b, w_o, w_router, b_router, w_gate, b_gate, w_up, b_up, w_down, b_down):
    b_, l_, d = x.shape
    lc = ctx.shape[1]
    n_experts = w_router.shape[-1]
    layer = 0
    hw = HGRN_HEADS * HGRN_DK
    kw = GLA_HEADS * GLA_DK
    vw = GLA_HEADS * GLA_DV
    rk = GLA_GATE_RANK

    rows = -(-(b_ + 1) // 8) * 8
    cc = jnp.concatenate([c, c_ctx[None, :], jnp.zeros((rows - b_ - 1, d), F32)], axis=0)
    mod = _ada_call(cc, w_ada[layer], b_ada[layer])[:b_ + 1]
    sh1, sc1, gt1, sh2, sc2, gt2 = [m.reshape(b_ + 1, 1, d) for m in jnp.split(mod, 6, axis=-1)]
    a1 = g_pre_mix[layer] * (1.0 + sc1)
    a2 = g_pre_ffn[layer] * (1.0 + sc2)

    w = w_in[layer]
    o0 = 0
    cols = {}
    for name, size in (("qa", hw), ("zf", hw), ("zb", hw), ("ia", hw), ("oga", hw), ("qb", kw), ("kb", kw),
                       ("vb", vw), ("rf", rk), ("rb", rk), ("ogb", vw), ("mga", d), ("mgb", d)):
        cols[name] = w[:, o0:o0 + size]
        o0 += size
    w_scan = jnp.concatenate([
        cols["qa"], cols["zf"], cols["zb"], cols["ia"],
        _pad_heads(cols["qb"] * (GLA_DK ** -0.5), GLA_HEADS, GLA_DK),
        _pad_heads(cols["kb"], GLA_HEADS, GLA_DK), cols["vb"]], axis=1).astype(BF16)
    w_rank = jnp.pad(jnp.concatenate([cols["rf"], cols["rb"]], axis=1), ((0, 0), (0, LANES - 2 * rk))).astype(BF16)
    w2 = jnp.zeros((LANES, 2 * GLA_HEADS * HEAD_W), F32)
    w2 = w2.at[0:rk, 0:GLA_HEADS * HEAD_W].set(_pad_heads(gla_gk_w2[layer, 0], GLA_HEADS, GLA_DK))
    w2 = w2.at[rk:2 * rk, GLA_HEADS * HEAD_W:].set(_pad_heads(gla_gk_w2[layer, 1], GLA_HEADS, GLA_DK))
    w2 = w2.astype(BF16)
    gkb = jnp.concatenate([_pad_heads(gla_gk_b[layer, 0][None, :], GLA_HEADS, GLA_DK),
                           _pad_heads(gla_gk_b[layer, 1][None, :], GLA_HEADS, GLA_DK)], axis=1)
    lb = jnp.cumsum(jax.nn.softmax(hgrn_lb.astype(F32), axis=0), axis=0)[layer]
    w_gates = jnp.concatenate([cols["oga"], cols["ogb"], cols["mga"], cols["mgb"]], axis=1).astype(BF16)
    norm_w = jnp.concatenate([jnp.tile(hgrn_norm_w[layer], HGRN_HEADS), jnp.tile(gla_norm_w[layer], GLA_HEADS)])[None, :]
    w_r = jnp.pad(w_router[layer], ((0, 0), (0, LANES - n_experts)))
    b_r = jnp.pad(b_router[layer], (0, LANES - n_experts), constant_values=-1e30)[None, :]

    q, v, akf, akb, bk, gf, gb = _inproj_call(x, ctx, a1, sh1, w_scan, w_rank, w2, gkb, lb)
    o_f, o_b = _scan_call(q, v, akf, akb, bk, gf, gb, lc)
    x1, h2, top_idx, probs = _merge_call(
        x, a1, sh1, gt1, a2, sh2, o_f, o_b, w_gates, norm_w,
        w_up_a[layer].astype(BF16), w_up_b[layer].astype(BF16), w_o[layer].astype(BF16),
        g_post_mix[layer][None, :], w_r, b_r)

    t_ = b_ * l_
    be, n_valid, tok, pair = _dispatch_plan(top_idx.reshape(t_, LANES)[:, :TOP_K], n_experts)
    y4 = _moe_call(be, n_valid, tok, pair, h2.reshape(t_, d), w_gate[layer], b_gate[layer],
                   w_up[layer], b_up[layer], w_down[layer], b_down[layer])
    return _final_call(x1, y4.reshape(b_, l_, TOP_K * d), probs, gt2, g_post_ffn[layer][None, :])
```

```python
import functools

import jax
import jax.numpy as jnp
from jax import lax
from jax.experimental import pallas as pl
from jax.experimental.pallas import tpu as pltpu

F32 = jnp.float32
BF16 = jnp.bfloat16

NORM_EPS = 1e-6
HGRN_HEADS = 4
HGRN_DK = 128
GLA_HEADS = 4
GLA_DK = 64
GLA_DV = 128
GLA_GATE_RANK = 16
GLA_GATE_NORMALIZER = 16.0
N_HEADS = HGRN_HEADS + GLA_HEADS
HEAD_W = 128
TOP_K = 4
SWIGLU_LIMIT = 7.0
SWIGLU_ALPHA = 1.702
MOE_BLOCK = 256
SCAN_CHUNK = 64
TOKEN_TILE = 256
LANES = 128
VMEM_LIMIT = 56 * 1024 * 1024


def _dot(a, b):
    return jnp.dot(a, b, preferred_element_type=F32)


def _dot_nt(a, b):
    return lax.dot_general(a, b, (((1,), (1,)), ((), ())), preferred_element_type=F32)


def _dot_tn(a, b):
    return lax.dot_general(a, b, (((0,), (0,)), ((), ())), preferred_element_type=F32)


def _split_bf16(a):
    hi = a.astype(BF16)
    lo = (a - hi.astype(F32)).astype(BF16)
    return hi, lo


def _dot3(a, b):
    ah, al = _split_bf16(a)
    bh, bl = _split_bf16(b)
    return _dot(ah, bh) + _dot(al, bh) + _dot(ah, bl)


def _rms(x):
    return x * lax.rsqrt(jnp.mean(x * x, axis=-1, keepdims=True) + NORM_EPS)


def _sigmoid(x):
    return 1.0 / (1.0 + jnp.exp(-x))


def _ada_kernel(c_ref, w_ref, b_ref, o_ref):
    c = c_ref[...]
    o_ref[...] = _dot3(c * _sigmoid(c), w_ref[...]) + b_ref[...]


def _ada_call(cc, w_ada, b_ada):
    rows, d = cc.shape
    n = w_ada.shape[1]
    tn = 1536
    return pl.pallas_call(
        _ada_kernel,
        grid=(n // tn,),
        in_specs=[
            pl.BlockSpec((rows, d), lambda j: (0, 0)),
            pl.BlockSpec((d, tn), lambda j: (0, j)),
            pl.BlockSpec((1, tn), lambda j: (0, j)),
        ],
        out_specs=pl.BlockSpec((rows, tn), lambda j: (0, j)),
        out_shape=jax.ShapeDtypeStruct((rows, n), F32),
        compiler_params=pltpu.CompilerParams(
            dimension_semantics=("arbitrary",), vmem_limit_bytes=VMEM_LIMIT),
    )(cc, w_ada, b_ada.reshape(1, n))


def _inproj_kernel(n_ctx_tiles, x_ref, ctx_ref, a_ref, s_ref, w_ref, wr_ref, w2_ref, gkb_ref, lb_ref,
                   q_ref, v_ref, akf_ref, akb_ref, bk_ref, gf_ref, gb_ref):
    t = pl.program_id(1)
    xt = jnp.where(t < n_ctx_tiles, ctx_ref[0], x_ref[0])
    hb = (_rms(xt) * a_ref[0] + s_ref[0]).astype(BF16)

    def seg(j):
        return _dot(hb, w_ref[:, j * 512:(j + 1) * 512])

    q_ref[0, :, 0:512] = seg(0).astype(BF16)
    q_ref[0, :, 512:1024] = seg(4).astype(BF16)
    v_ref[0, :, 0:512] = seg(3).astype(BF16)
    v_ref[0, :, 512:1024] = seg(6).astype(BF16)
    bk_ref[0] = seg(5).astype(BF16)
    for j, k_ref, g_ref in ((1, akf_ref, gf_ref), (2, akb_ref, gb_ref)):
        z = seg(j)
        lb = lb_ref[j - 1:j, :]
        g_ref[0, :, 0:512] = jnp.log(lb + (1.0 - lb) * _sigmoid(z))
        k_ref[0] = ((1.0 - lb) * _sigmoid(-z)).astype(BF16)
    r = _dot(hb, wr_ref[...]).astype(BF16)
    pre = _dot(r, w2_ref[...]) + gkb_ref[...]
    ls = (jnp.minimum(pre, 0.0) - jnp.log(1.0 + jnp.exp(-jnp.abs(pre)))) * (1.0 / GLA_GATE_NORMALIZER)
    gf_ref[0, :, 512:1024] = ls[:, 0:512]
    gb_ref[0, :, 512:1024] = ls[:, 512:1024]


def _inproj_call(x, ctx, mod_a, mod_s, w_scan, w_rank, w2, gkb, lb):
    b_, l_, d = x.shape
    lc = ctx.shape[1]
    tm = TOKEN_TILE
    nct, nlt = lc // tm, l_ // tm
    lt = lc + l_

    def x_map(b, t):
        return (b, jnp.maximum(t - nct, 0), 0)

    def ctx_map(b, t):
        return (b, jnp.minimum(t, nct - 1), 0)

    def mod_map(b, t):
        return (jnp.where(t < nct, b_, b), 0, 0)

    const2 = lambda b, t: (0, 0)
    out_map = lambda b, t: (b, t, 0)
    wide = jax.ShapeDtypeStruct((b_, lt, 1024), BF16)
    half = jax.ShapeDtypeStruct((b_, lt, 512), BF16)
    gate = jax.ShapeDtypeStruct((b_, lt, 1024), F32)
    return pl.pallas_call(
        functools.partial(_inproj_kernel, nct),
        grid=(b_, nct + nlt),
        in_specs=[
            pl.BlockSpec((1, tm, d), x_map),
            pl.BlockSpec((1, tm, d), ctx_map),
            pl.BlockSpec((1, 1, d), mod_map),
            pl.BlockSpec((1, 1, d), mod_map),
            pl.BlockSpec(w_scan.shape, const2),
            pl.BlockSpec(w_rank.shape, const2),
            pl.BlockSpec(w2.shape, const2),
            pl.BlockSpec(gkb.shape, const2),
            pl.BlockSpec(lb.shape, const2),
        ],
        out_specs=[
            pl.BlockSpec((1, tm, 1024), out_map),
            pl.BlockSpec((1, tm, 1024), out_map),
            pl.BlockSpec((1, tm, 512), out_map),
            pl.BlockSpec((1, tm, 512), out_map),
            pl.BlockSpec((1, tm, 512), out_map),
            pl.BlockSpec((1, tm, 1024), out_map),
            pl.BlockSpec((1, tm, 1024), out_map),
        ],
        out_shape=[wide, wide, half, half, half, gate, gate],
        compiler_params=pltpu.CompilerParams(
            dimension_semantics=("arbitrary", "arbitrary"), vmem_limit_bytes=VMEM_LIMIT),
    )(x, ctx, mod_a, mod_s, w_scan, w_rank, w2, gkb, lb)


def _scan_direction(forward, q_ref, v_ref, ak_ref, bk_ref, g_ref, o_ref, st_ref):
    c_ = SCAN_CHUNK
    row = lax.broadcasted_iota(jnp.int32, (c_, c_), 0)
    col = lax.broadcasted_iota(jnp.int32, (c_, c_), 1)
    causal = (row >= col) if forward else (col >= row)
    tri = jnp.where(causal, 1.0, 0.0).astype(BF16)
    g_hi, g_lo = _split_bf16(g_ref[0])
    cum = _dot(tri, g_hi) + _dot(tri, g_lo)
    end_row, mid_row = (c_ - 1, c_ // 2 - 1) if forward else (0, c_ // 2)
    total = cum[end_row:end_row + 1, :]
    mid = cum[mid_row:mid_row + 1, :]
    e_in = jnp.exp(cum)
    e_out = jnp.exp(total - cum)
    e_q = jnp.exp(cum - mid)
    e_k = jnp.exp(mid - cum)
    e_tot = jnp.exp(total)
    for h in range(N_HEADS):
        sl = slice(h * HEAD_W, (h + 1) * HEAD_W)
        hk = slice((h % HGRN_HEADS) * HEAD_W, (h % HGRN_HEADS + 1) * HEAD_W)
        q = q_ref[0, :, sl].astype(F32)
        k = (ak_ref if h < HGRN_HEADS else bk_ref)[0, :, hk].astype(F32)
        v = v_ref[0, :, sl]
        scores = _dot_nt((q * e_q[:, sl]).astype(BF16), (k * e_k[:, sl]).astype(BF16))
        scores = jnp.where(causal, scores, 0.0).astype(BF16)
        st = st_ref[h]
        o = _dot(scores, v) + _dot_nt((q * e_in[:, sl]).astype(BF16), st.astype(BF16))
        o_ref[0, :, sl] = o.astype(o_ref.dtype)
        st_ref[h] = st * e_tot[:, sl] + _dot_tn(v, (k * e_out[:, sl]).astype(BF16))


def _scan_kernel(qf_ref, vf_ref, akf_ref, bkf_ref, gf_ref, qb_ref, vb_ref, akb_ref, bkb_ref, gb_ref,
                 of_ref, ob_ref, sf_ref, sb_ref):
    @pl.when(pl.program_id(1) == 0)
    def _():
        sf_ref[...] = jnp.zeros_like(sf_ref)
        sb_ref[...] = jnp.zeros_like(sb_ref)

    _scan_direction(True, qf_ref, vf_ref, akf_ref, bkf_ref, gf_ref, of_ref, sf_ref)
    _scan_direction(False, qb_ref, vb_ref, akb_ref, bkb_ref, gb_ref, ob_ref, sb_ref)


def _scan_call(q, v, akf, akb, bk, gf, gb, lc):
    b_, lt, _ = q.shape
    c_ = SCAN_CHUNK
    ncc = lc // c_
    nlc = (lt - lc) // c_
    n = ncc + nlc

    def fwd(b, s):
        return (b, s, 0)

    def bwd(b, s):
        return (b, jnp.where(s < ncc, ncc - 1 - s, n + ncc - 1 - s), 0)

    def out_fwd(b, s):
        return (b, jnp.maximum(s - ncc, 0), 0)

    def out_bwd(b, s):
        return (b, jnp.where(s < ncc, nlc - 1, n - 1 - s), 0)

    def specs(index_map):
        return [
            pl.BlockSpec((1, c_, 1024), index_map),
            pl.BlockSpec((1, c_, 1024), index_map),
            pl.BlockSpec((1, c_, 512), index_map),
            pl.BlockSpec((1, c_, 512), index_map),
            pl.BlockSpec((1, c_, 1024), index_map),
        ]

    out = jax.ShapeDtypeStruct((b_, lt - lc, 1024), BF16)
    return pl.pallas_call(
        _scan_kernel,
        grid=(b_, n),
        in_specs=specs(fwd) + specs(bwd),
        out_specs=[pl.BlockSpec((1, c_, 1024), out_fwd), pl.BlockSpec((1, c_, 1024), out_bwd)],
        out_shape=[out, out],
        scratch_shapes=[pltpu.VMEM((N_HEADS, HEAD_W, HEAD_W), F32),
                        pltpu.VMEM((N_HEADS, HEAD_W, HEAD_W), F32)],
        compiler_params=pltpu.CompilerParams(
            dimension_semantics=("arbitrary", "arbitrary"), vmem_limit_bytes=VMEM_LIMIT),
    )(q, v, akf, bk, gf, q, v, akb, bk, gb)


def _merge_kernel(x_ref, a1_ref, s1_ref, gt1_ref, a2_ref, s2_ref, of_ref, ob_ref, wg_ref, nw_ref,
                  wua_ref, wub_ref, wo_ref, gpost_ref, wr_ref, br_ref,
                  x1_ref, h2_ref, idx_ref, prob_ref):
    x = x_ref[0]
    hb = (_rms(x) * a1_ref[0] + s1_ref[0]).astype(BF16)
    o = of_ref[0].astype(F32) + ob_ref[0].astype(F32)
    og = _dot(hb, wg_ref[:, 0:1024])
    heads = []
    for h in range(N_HEADS):
        sl = slice(h * HEAD_W, (h + 1) * HEAD_W)
        heads.append(_rms(o[:, sl]))
    og = og * _sigmoid(og)
    r = (jnp.concatenate(heads, axis=-1) * nw_ref[...] * og).astype(BF16)
    y_a = _dot(r[:, 0:512], wua_ref[...])
    y_b = _dot(r[:, 512:1024], wub_ref[...])
    mg_a = _sigmoid(_dot(hb, wg_ref[:, 1024:2048]))
    mg_b = _sigmoid(_dot(hb, wg_ref[:, 2048:3072]))
    mix = _dot((mg_a * y_a + mg_b * y_b).astype(BF16), wo_ref[...])
    x1 = x + gt1_ref[0] * (_rms(mix) * gpost_ref[...])
    x1_ref[0] = x1
    h2 = _rms(x1) * a2_ref[0] + s2_ref[0]
    nslab = h2.shape[-1] // LANES
    for s in range(nslab):
        h2_ref[pl.ds(s, h2.shape[0], stride=nslab), :] = h2[:, s * LANES:(s + 1) * LANES]
    logits = _dot3(h2, wr_ref[...]) + br_ref[...]
    lane = lax.broadcasted_iota(jnp.int32, logits.shape, 1).astype(F32)
    vals, idxs = [], []
    for _ in range(TOP_K):
        m = jnp.max(logits, axis=-1, keepdims=True)
        sel = jnp.min(jnp.where(logits == m, lane, float(LANES)), axis=-1, keepdims=True)
        vals.append(m)
        idxs.append(sel)
        logits = jnp.where(lane == sel, -jnp.inf, logits)
    exps = [jnp.exp(v_ - vals[0]) for v_ in vals]
    denom = exps[0] + exps[1] + exps[2] + exps[3]
    idx_out = jnp.zeros_like(lane)
    prob_out = jnp.zeros_like(lane)
    for k_ in range(TOP_K):
        idx_out = jnp.where(lane == float(k_), idxs[k_], idx_out)
        prob_out = jnp.where(lane == float(k_), exps[k_] / denom, prob_out)
    idx_ref[0] = idx_out.astype(jnp.int32)
    prob_ref[0] = prob_out


def _merge_call(x, a1, s1, gt1, a2, s2, o_f, o_b, w_gates, norm_w, w_up_a, w_up_b, w_o, g_post, w_r, b_r):
    b_, l_, d = x.shape
    tm = TOKEN_TILE
    tile = lambda b, t: (b, t, 0)
    per_b = lambda b, t: (b, 0, 0)
    const2 = lambda b, t: (0, 0)
    return pl.pallas_call(
        _merge_kernel,
        grid=(b_, l_ // tm),
        in_specs=[
            pl.BlockSpec((1, tm, d), tile),
            pl.BlockSpec((1, 1, d), per_b), pl.BlockSpec((1, 1, d), per_b), pl.BlockSpec((1, 1, d), per_b),
            pl.BlockSpec((1, 1, d), per_b), pl.BlockSpec((1, 1, d), per_b),
            pl.BlockSpec((1, tm, 1024), tile), pl.BlockSpec((1, tm, 1024), tile),
            pl.BlockSpec(w_gates.shape, const2),
            pl.BlockSpec(norm_w.shape, const2),
            pl.BlockSpec(w_up_a.shape, const2),
            pl.BlockSpec(w_up_b.shape, const2),
            pl.BlockSpec(w_o.shape, const2),
            pl.BlockSpec(g_post.shape, const2),
            pl.BlockSpec(w_r.shape, const2),
            pl.BlockSpec(b_r.shape, const2),
        ],
        out_specs=[
            pl.BlockSpec((1, tm, d), tile),
            pl.BlockSpec((tm * (d // LANES), LANES), lambda b, t: (b * (l_ // tm) + t, 0)),
            pl.BlockSpec((1, tm, LANES), tile), pl.BlockSpec((1, tm, LANES), tile),
        ],
        out_shape=[
            jax.ShapeDtypeStruct((b_, l_, d), F32), jax.ShapeDtypeStruct((b_ * l_ * (d // LANES), LANES), F32),
            jax.ShapeDtypeStruct((b_, l_, LANES), jnp.int32), jax.ShapeDtypeStruct((b_, l_, LANES), F32),
        ],
        compiler_params=pltpu.CompilerParams(
            dimension_semantics=("arbitrary", "arbitrary"), vmem_limit_bytes=VMEM_LIMIT),
    )(x, a1, s1, gt1, a2, s2, o_f, o_b, w_gates, norm_w, w_up_a, w_up_b, w_o, g_post, w_r, b_r)


def _moe_kernel(be_ref, nv_ref, tok_ref, tokn_ref, dst_ref, h2_hbm,
                wg_ref, bg_ref, wu_ref, bu_ref, wd_ref, bd_ref, out_hbm,
                xbuf, ybuf, wgb, wub, wdb, gsem, ssem):
    i = pl.program_id(0)
    n = pl.num_programs(0)
    slot = i % 2
    rows = MOE_BLOCK
    nslab = xbuf.shape[0] // (2 * rows)
    blk = rows * nslab

    def slab(ref, start):
        return ref.at[pl.ds(pl.multiple_of(start, nslab), nslab), :]

    def issue_gather(idx_ref, s):
        def body(r, carry):
            pltpu.make_async_copy(slab(h2_hbm, idx_ref[0, 0, r]), slab(xbuf, s * blk + r * nslab), gsem.at[s]).start()
            return carry
        lax.fori_loop(0, rows, body, 0, unroll=8)

    def wait_gather(s):
        pltpu.make_async_copy(h2_hbm.at[pl.ds(0, blk), :], xbuf.at[pl.ds(pl.multiple_of(s * blk, blk), blk), :],
                              gsem.at[s]).wait()

    def issue_scatter(s):
        def body(r, carry):
            pltpu.make_async_copy(slab(ybuf, s * blk + r * nslab), slab(out_hbm, dst_ref[0, 0, r]), ssem.at[s]).start()
            return carry
        lax.fori_loop(0, rows, body, 0, unroll=8)

    def wait_scatter(s):
        pltpu.make_async_copy(ybuf.at[pl.ds(pl.multiple_of(s * blk, blk), blk), :], out_hbm.at[pl.ds(0, blk), :],
                              ssem.at[s]).wait()

    @pl.when(jnp.logical_and(i == 0, nv_ref[0] > 0))
    def _():
        issue_gather(tok_ref, 0)

    nxt = jnp.minimum(i + 1, n - 1)

    @pl.when(jnp.logical_and(i + 1 < n, nv_ref[nxt] > 0))
    def _():
        issue_gather(tokn_ref, 1 - slot)

    prev = jnp.maximum(i - 1, 0)

    @pl.when(jnp.logical_or(i == 0, be_ref[i] != be_ref[prev]))
    def _():
        wgb[...] = wg_ref[0].astype(BF16)
        wub[...] = wu_ref[0].astype(BF16)
        wdb[...] = wd_ref[0].astype(BF16)

    @pl.when(jnp.logical_and(i >= 2, nv_ref[jnp.maximum(i - 2, 0)] > 0))
    def _():
        wait_scatter(slot)

    @pl.when(nv_ref[i] > 0)
    def _():
        wait_gather(slot)
        base = slot * blk
        xb = jnp.concatenate([xbuf[pl.ds(base + s, rows, stride=nslab), :] for s in range(nslab)],
                             axis=-1).astype(BF16)
        g = jnp.minimum(_dot(xb, wgb[...]) + bg_ref[0], SWIGLU_LIMIT)
        u = jnp.clip(_dot(xb, wub[...]) + bu_ref[0], -SWIGLU_LIMIT, SWIGLU_LIMIT)
        act = g * _sigmoid(SWIGLU_ALPHA * g) * (u + 1.0)
        y = _dot(act.astype(BF16), wdb[...]) + bd_ref[0]
        for s in range(nslab):
            ybuf[pl.ds(base + s, rows, stride=nslab), :] = y[:, s * LANES:(s + 1) * LANES]
        issue_scatter(slot)

    @pl.when(nv_ref[i] == 0)
    def _():
        ybuf[pl.ds(pl.multiple_of(slot * blk, blk), blk), :] = jnp.zeros((blk, LANES), F32)
        fill = pltpu.make_async_copy(ybuf.at[pl.ds(pl.multiple_of(slot * blk, blk), blk), :],
                                     out_hbm.at[pl.ds(pl.multiple_of(dst_ref[0, 0, 0], nslab), blk), :], ssem.at[slot])
        fill.start()
        fill.wait()

    @pl.when(i == n - 1)
    def _():
        @pl.when(jnp.logical_and(i >= 1, nv_ref[prev] > 0))
        def _():
            wait_scatter(1 - slot)

        @pl.when(nv_ref[i] > 0)
        def _():
            wait_scatter(slot)


def _moe_call(block_expert, n_valid, tok, dst, h2, n_out_rows, w_gate, b_gate, w_up, b_up, w_down, b_down):
    e_, d, f = w_gate.shape
    nslab = d // LANES
    nblk = tok.shape[0]
    rows = MOE_BLOCK

    def blk(i, be, nv):
        return (i, 0, 0)

    def blk_next(i, be, nv):
        return (jnp.minimum(i + 1, nblk - 1), 0, 0)

    def expert(i, be, nv):
        return (be[i], 0, 0)

    smem_rows = functools.partial(pl.BlockSpec, (1, 1, rows), memory_space=pltpu.SMEM)
    grid_spec = pltpu.PrefetchScalarGridSpec(
        num_scalar_prefetch=2,
        grid=(nblk,),
        in_specs=[
            smem_rows(blk), smem_rows(blk_next), smem_rows(blk),
            pl.BlockSpec(memory_space=pl.ANY),
            pl.BlockSpec((1, d, f), expert), pl.BlockSpec((1, 1, f), expert),
            pl.BlockSpec((1, d, f), expert), pl.BlockSpec((1, 1, f), expert),
            pl.BlockSpec((1, f, d), expert), pl.BlockSpec((1, 1, d), expert),
        ],
        out_specs=pl.BlockSpec(memory_space=pl.ANY),
        scratch_shapes=[
            pltpu.VMEM((2 * rows * nslab, LANES), F32), pltpu.VMEM((2 * rows * nslab, LANES), F32),
            pltpu.VMEM((d, f), BF16), pltpu.VMEM((d, f), BF16), pltpu.VMEM((f, d), BF16),
            pltpu.SemaphoreType.DMA((2,)), pltpu.SemaphoreType.DMA((2,)),
        ],
    )
    return pl.pallas_call(
        _moe_kernel,
        grid_spec=grid_spec,
        out_shape=jax.ShapeDtypeStruct((n_out_rows * nslab, LANES), F32),
        compiler_params=pltpu.CompilerParams(
            dimension_semantics=("arbitrary",), vmem_limit_bytes=VMEM_LIMIT),
    )(block_expert, n_valid, tok, tok, dst, h2,
      w_gate, b_gate.reshape(e_, 1, f), w_up, b_up.reshape(e_, 1, f), w_down, b_down.reshape(e_, 1, d))


def _dispatch_plan(top_idx, n_experts, nslab):
    t_ = top_idx.shape[0]
    n_pairs = t_ * TOP_K
    flat_e = top_idx.reshape(-1)
    order = jnp.argsort(flat_e).astype(jnp.int32)
    counts = jnp.sum((flat_e[:, None] == jnp.arange(n_experts, dtype=jnp.int32)[None, :]).astype(jnp.int32), axis=0)
    nblk_e = (counts + MOE_BLOCK - 1) // MOE_BLOCK
    blk_end = jnp.cumsum(nblk_e)
    blk_start = blk_end - nblk_e
    start = jnp.cumsum(counts) - counts
    nblk = n_pairs // MOE_BLOCK + n_experts
    b = jnp.arange(nblk, dtype=jnp.int32)
    be = jnp.minimum(jnp.sum((b[:, None] >= blk_end[None, :]).astype(jnp.int32), axis=1), n_experts - 1)
    first_row = (b - blk_start[be]) * MOE_BLOCK
    n_valid = jnp.clip(counts[be] - first_row, 0, MOE_BLOCK).astype(jnp.int32)
    r = jnp.arange(MOE_BLOCK, dtype=jnp.int32)[None, :]
    valid = r < n_valid[:, None]
    src = jnp.clip(start[be][:, None] + first_row[:, None] + r, 0, n_pairs - 1)
    pair = jnp.where(valid, order[src], 0).astype(jnp.int32)
    tok = pair // TOP_K
    spare = (jnp.cumsum((~valid).reshape(-1).astype(jnp.int32)) - 1).reshape(nblk, MOE_BLOCK)
    dst = jnp.where(valid, (pair % TOP_K) * t_ + tok, n_pairs + spare).astype(jnp.int32)
    n_out_rows = n_pairs + n_experts * MOE_BLOCK
    return (be, n_valid, (tok * nslab).reshape(nblk, 1, MOE_BLOCK), (dst * nslab).reshape(nblk, 1, MOE_BLOCK),
            n_out_rows)


def _final_kernel(x1_ref, y0_ref, y1_ref, y2_ref, y3_ref, p_ref, gt2_ref, gpost_ref, o_ref):
    p = p_ref[0]
    tm, d = x1_ref.shape[1:]
    nslab = d // LANES
    parts = []
    for s in range(nslab):
        acc = p[:, 0:1] * y0_ref[pl.ds(s, tm, stride=nslab), :]
        for k_, y_ref in ((1, y1_ref), (2, y2_ref), (3, y3_ref)):
            acc = acc + p[:, k_:k_ + 1] * y_ref[pl.ds(s, tm, stride=nslab), :]
        parts.append(acc)
    ffn = jnp.concatenate(parts, axis=-1)
    o_ref[0] = x1_ref[0] + gt2_ref[0] * (_rms(ffn) * gpost_ref[...])


def _final_call(x1, y, probs, gt2, g_post):
    b_, l_, d = x1.shape
    tm = TOKEN_TILE
    nslab = d // LANES
    tiles_per_k = b_ * l_ // tm
    tile = lambda b, t: (b, t, 0)

    def y_spec(k_):
        return pl.BlockSpec((tm * nslab, LANES), lambda b, t: (k_ * tiles_per_k + b * (l_ // tm) + t, 0))

    return pl.pallas_call(
        _final_kernel,
        grid=(b_, l_ // tm),
        in_specs=[
            pl.BlockSpec((1, tm, d), tile),
            y_spec(0), y_spec(1), y_spec(2), y_spec(3),
            pl.BlockSpec((1, tm, LANES), tile),
            pl.BlockSpec((1, 1, d), lambda b, t: (b, 0, 0)),
            pl.BlockSpec(g_post.shape, lambda b, t: (0, 0)),
        ],
        out_specs=pl.BlockSpec((1, tm, d), tile),
        out_shape=jax.ShapeDtypeStruct((b_, l_, d), F32),
        compiler_params=pltpu.CompilerParams(
            dimension_semantics=("arbitrary", "arbitrary"), vmem_limit_bytes=VMEM_LIMIT),
    )(x1, y, y, y, y, probs, gt2, g_post)


def _pad_heads(w, n_heads, dk):
    d = w.shape[0]
    w = w.reshape(d, n_heads, dk)
    return jnp.pad(w, ((0, 0), (0, 0), (0, HEAD_W - dk))).reshape(d, n_heads * HEAD_W)


def kernel(x, c, ctx, c_ctx, w_ada, b_ada, g_pre_mix, g_post_mix, g_pre_ffn, g_post_ffn, w_in, hgrn_lb, hgrn_norm_w, gla_gk_w2, gla_gk_b, gla_norm_w, w_up_a, w_up_b, w_o, w_router, b_router, w_gate, b_gate, w_up, b_up, w_down, b_down):
    b_, l_, d = x.shape
    lc = ctx.shape[1]
    n_experts = w_router.shape[-1]
    layer = 0
    hw = HGRN_HEADS * HGRN_DK
    kw = GLA_HEADS * GLA_DK
    vw = GLA_HEADS * GLA_DV
    rk = GLA_GATE_RANK

    rows = -(-(b_ + 1) // 8) * 8
    cc = jnp.concatenate([c, c_ctx[None, :], jnp.zeros((rows - b_ - 1, d), F32)], axis=0)
    mod = _ada_call(cc, w_ada[layer], b_ada[layer])[:b_ + 1]
    sh1, sc1, gt1, sh2, sc2, gt2 = [m.reshape(b_ + 1, 1, d) for m in jnp.split(mod, 6, axis=-1)]
    a1 = g_pre_mix[layer] * (1.0 + sc1)
    a2 = g_pre_ffn[layer] * (1.0 + sc2)

    w = w_in[layer]
    o0 = 0
    cols = {}
    for name, size in (("qa", hw), ("zf", hw), ("zb", hw), ("ia", hw), ("oga", hw), ("qb", kw), ("kb", kw),
                       ("vb", vw), ("rf", rk), ("rb", rk), ("ogb", vw), ("mga", d), ("mgb", d)):
        cols[name] = w[:, o0:o0 + size]
        o0 += size
    w_scan = jnp.concatenate([
        cols["qa"], cols["zf"], cols["zb"], cols["ia"],
        _pad_heads(cols["qb"] * (GLA_DK ** -0.5), GLA_HEADS, GLA_DK),
        _pad_heads(cols["kb"], GLA_HEADS, GLA_DK), cols["vb"]], axis=1).astype(BF16)
    w_rank = jnp.pad(jnp.concatenate([cols["rf"], cols["rb"]], axis=1), ((0, 0), (0, LANES - 2 * rk))).astype(BF16)
    w2 = jnp.zeros((LANES, 2 * GLA_HEADS * HEAD_W), F32)
    w2 = w2.at[0:rk, 0:GLA_HEADS * HEAD_W].set(_pad_heads(gla_gk_w2[layer, 0], GLA_HEADS, GLA_DK))
    w2 = w2.at[rk:2 * rk, GLA_HEADS * HEAD_W:].set(_pad_heads(gla_gk_w2[layer, 1], GLA_HEADS, GLA_DK))
    w2 = w2.astype(BF16)
    gkb = jnp.concatenate([_pad_heads(gla_gk_b[layer, 0][None, :], GLA_HEADS, GLA_DK),
                           _pad_heads(gla_gk_b[layer, 1][None, :], GLA_HEADS, GLA_DK)], axis=1)
    lb = jnp.cumsum(jax.nn.softmax(hgrn_lb.astype(F32), axis=0), axis=0)[layer]
    w_gates = jnp.concatenate([cols["oga"], cols["ogb"], cols["mga"], cols["mgb"]], axis=1).astype(BF16)
    norm_w = jnp.concatenate([jnp.tile(hgrn_norm_w[layer], HGRN_HEADS), jnp.tile(gla_norm_w[layer], GLA_HEADS)])[None, :]
    w_r = jnp.pad(w_router[layer], ((0, 0), (0, LANES - n_experts)))
    b_r = jnp.pad(b_router[layer], (0, LANES - n_experts), constant_values=-1e30)[None, :]

    q, v, akf, akb, bk, gf, gb = _inproj_call(x, ctx, a1, sh1, w_scan, w_rank, w2, gkb, lb)
    o_f, o_b = _scan_call(q, v, akf, akb, bk, gf, gb, lc)
    x1, h2, top_idx, probs = _merge_call(
        x, a1, sh1, gt1, a2, sh2, o_f, o_b, w_gates, norm_w,
        w_up_a[layer].astype(BF16), w_up_b[layer].astype(BF16), w_o[layer].astype(BF16),
        g_post_mix[layer][None, :], w_r, b_r)

    t_ = b_ * l_
    be, n_valid, tok, dst, n_out_rows = _dispatch_plan(top_idx.reshape(t_, LANES)[:, :TOP_K], n_experts, d // LANES)
    y = _moe_call(be, n_valid, tok, dst, h2, n_out_rows,
                  w_gate[layer], b_gate[layer], w_up[layer], b_up[layer], w_down[layer], b_down[layer])
    return _final_call(x1, y, probs, gt2, g_post_ffn[layer][None, :])
```

```python
import functools

import jax
import jax.numpy as jnp
from jax import lax
from jax.experimental import pallas as pl
from jax.experimental.pallas import tpu as pltpu

F32 = jnp.float32
BF16 = jnp.bfloat16

NORM_EPS = 1e-6
HGRN_HEADS = 4
HGRN_DK = 128
GLA_HEADS = 4
GLA_DK = 64
GLA_DV = 128
GLA_GATE_RANK = 16
GLA_GATE_NORMALIZER = 16.0
N_HEADS = HGRN_HEADS + GLA_HEADS
HEAD_W = 128
TOP_K = 4
SWIGLU_LIMIT = 7.0
SWIGLU_ALPHA = 1.702
MOE_BLOCK = 256
SCAN_CHUNK = 64
SCAN_BLOCK = 256
LOG2E = 1.4426950408889634
TOKEN_TILE = 256
LANES = 128
VMEM_LIMIT = 56 * 1024 * 1024


def _dot(a, b):
    return jnp.dot(a, b, preferred_element_type=F32)


def _dot_nt(a, b):
    return lax.dot_general(a, b, (((1,), (1,)), ((), ())), preferred_element_type=F32)


def _dot_tn(a, b):
    return lax.dot_general(a, b, (((0,), (0,)), ((), ())), preferred_element_type=F32)


def _split_bf16(a):
    hi = a.astype(BF16)
    lo = (a - hi.astype(F32)).astype(BF16)
    return hi, lo


def _dot3(a, b):
    ah, al = _split_bf16(a)
    bh, bl = _split_bf16(b)
    return _dot(ah, bh) + _dot(al, bh) + _dot(ah, bl)


def _rms(x):
    return x * lax.rsqrt(jnp.mean(x * x, axis=-1, keepdims=True) + NORM_EPS)


def _sigmoid(x):
    return 1.0 / (1.0 + jnp.exp(-x))


def _ada_kernel(c_ref, w_ref, b_ref, o_ref):
    c = c_ref[...]
    o_ref[...] = _dot3(c * _sigmoid(c), w_ref[...]) + b_ref[...]


def _ada_call(cc, w_ada, b_ada):
    rows, d = cc.shape
    n = w_ada.shape[1]
    tn = 1536
    return pl.pallas_call(
        _ada_kernel,
        grid=(n // tn,),
        in_specs=[
            pl.BlockSpec((rows, d), lambda j: (0, 0)),
            pl.BlockSpec((d, tn), lambda j: (0, j)),
            pl.BlockSpec((1, tn), lambda j: (0, j)),
        ],
        out_specs=pl.BlockSpec((rows, tn), lambda j: (0, j)),
        out_shape=jax.ShapeDtypeStruct((rows, n), F32),
        compiler_params=pltpu.CompilerParams(
            dimension_semantics=("arbitrary",), vmem_limit_bytes=VMEM_LIMIT),
    )(cc, w_ada, b_ada.reshape(1, n))


def _inproj_kernel(n_ctx_tiles, x_ref, ctx_ref, a_ref, s_ref, w_ref, wr_ref, w2_ref, gkb_ref, lb_ref,
                   q_ref, v_ref, akf_ref, akb_ref, bk_ref, gf_ref, gb_ref):
    t = pl.program_id(1)
    xt = jnp.where(t < n_ctx_tiles, ctx_ref[0], x_ref[0])
    hb = (_rms(xt) * a_ref[0] + s_ref[0]).astype(BF16)

    def seg(j):
        return _dot(hb, w_ref[:, j * 512:(j + 1) * 512])

    z_f, z_b = seg(1), seg(2)
    r = _dot(hb, wr_ref[...]).astype(BF16)
    pre = _dot(r, w2_ref[...]) + gkb_ref[...]
    plain = [seg(j) for j in (0, 4, 3, 6, 5)]
    for j, z, k_ref, g_ref in ((1, z_f, akf_ref, gf_ref), (2, z_b, akb_ref, gb_ref)):
        lb = lb_ref[j - 1:j, :]
        sg = _sigmoid(z)
        g_ref[0, :, 0:512] = jnp.log2(lb + (1.0 - lb) * sg)
        k_ref[0] = ((1.0 - lb) * (1.0 - sg)).astype(BF16)
    ls = (jnp.minimum(pre, 0.0) - jnp.log(1.0 + jnp.exp(-jnp.abs(pre)))) * (LOG2E / GLA_GATE_NORMALIZER)
    gf_ref[0, :, 512:1024] = ls[:, 0:512]
    gb_ref[0, :, 512:1024] = ls[:, 512:1024]
    q_ref[0, :, 0:512] = plain[0].astype(BF16)
    q_ref[0, :, 512:1024] = plain[1].astype(BF16)
    v_ref[0, :, 0:512] = plain[2].astype(BF16)
    v_ref[0, :, 512:1024] = plain[3].astype(BF16)
    bk_ref[0] = plain[4].astype(BF16)


def _inproj_call(x, ctx, mod_a, mod_s, w_scan, w_rank, w2, gkb, lb):
    b_, l_, d = x.shape
    lc = ctx.shape[1]
    tm = TOKEN_TILE
    nct, nlt = lc // tm, l_ // tm
    lt = lc + l_

    def x_map(b, t):
        return (b, jnp.maximum(t - nct, 0), 0)

    def ctx_map(b, t):
        return (b, jnp.minimum(t, nct - 1), 0)

    def mod_map(b, t):
        return (jnp.where(t < nct, b_, b), 0, 0)

    const2 = lambda b, t: (0, 0)
    out_map = lambda b, t: (b, t, 0)
    wide = jax.ShapeDtypeStruct((b_, lt, 1024), BF16)
    half = jax.ShapeDtypeStruct((b_, lt, 512), BF16)
    gate = jax.ShapeDtypeStruct((b_, lt, 1024), F32)
    return pl.pallas_call(
        functools.partial(_inproj_kernel, nct),
        grid=(b_, nct + nlt),
        in_specs=[
            pl.BlockSpec((1, tm, d), x_map),
            pl.BlockSpec((1, tm, d), ctx_map),
            pl.BlockSpec((1, 1, d), mod_map),
            pl.BlockSpec((1, 1, d), mod_map),
            pl.BlockSpec(w_scan.shape, const2),
            pl.BlockSpec(w_rank.shape, const2),
            pl.BlockSpec(w2.shape, const2),
            pl.BlockSpec(gkb.shape, const2),
            pl.BlockSpec(lb.shape, const2),
        ],
        out_specs=[
            pl.BlockSpec((1, tm, 1024), out_map),
            pl.BlockSpec((1, tm, 1024), out_map),
            pl.BlockSpec((1, tm, 512), out_map),
            pl.BlockSpec((1, tm, 512), out_map),
            pl.BlockSpec((1, tm, 512), out_map),
            pl.BlockSpec((1, tm, 1024), out_map),
            pl.BlockSpec((1, tm, 1024), out_map),
        ],
        out_shape=[wide, wide, half, half, half, gate, gate],
        compiler_params=pltpu.CompilerParams(
            dimension_semantics=("arbitrary", "arbitrary"), vmem_limit_bytes=VMEM_LIMIT),
    )(x, ctx, mod_a, mod_s, w_scan, w_rank, w2, gkb, lb)


def _causal(forward):
    row = lax.broadcasted_iota(jnp.int32, (SCAN_CHUNK, SCAN_CHUNK), 0)
    col = lax.broadcasted_iota(jnp.int32, (SCAN_CHUNK, SCAN_CHUNK), 1)
    return (row >= col) if forward else (col >= row)


def _scan_block(with_out, qf_ref, vf_ref, akf_ref, bkf_ref, gf_ref, qb_ref, vb_ref, akb_ref, bkb_ref, gb_ref,
                of_ref, ob_ref, sf_ref, sb_ref):
    c_ = SCAN_CHUNK
    ncb = qf_ref.shape[1] // c_
    dirs = [(True, qf_ref, vf_ref, akf_ref, bkf_ref, gf_ref, of_ref, sf_ref),
            (False, qb_ref, vb_ref, akb_ref, bkb_ref, gb_ref, ob_ref, sb_ref)]
    cums = {}
    for di, (fw, _, _, _, _, g_ref, _, _) in enumerate(dirs):
        tri = jnp.where(_causal(fw), 1.0, 0.0).astype(BF16)
        for c in range(ncb):
            g_hi, g_lo = _split_bf16(g_ref[0, c * c_:(c + 1) * c_, :])
            cums[di, c] = _dot(tri, g_hi) + _dot(tri, g_lo)
    ops = []
    for di, (fw, q_ref, v_ref, ak_ref, bk_ref, _, _, _) in enumerate(dirs):
        end_row, mid_row = (c_ - 1, c_ // 2 - 1) if fw else (0, c_ // 2)
        for c in range(ncb):
            rows = slice(c * c_, (c + 1) * c_)
            cum = cums[di, c]
            total = cum[end_row:end_row + 1, :]
            mid = cum[mid_row:mid_row + 1, :]
            e_mid = jnp.exp2(mid)
            e_rest = jnp.exp2(total - mid)
            e_tot = jnp.exp2(total)
            for h in range(N_HEADS):
                sl = slice(h * HEAD_W, (h + 1) * HEAD_W)
                hk = slice((h % HGRN_HEADS) * HEAD_W, (h % HGRN_HEADS + 1) * HEAD_W)
                qs = q_ref[0, rows, sl].astype(F32) * jnp.exp2(cum[:, sl] - mid[:, sl])
                ks = (ak_ref if h < HGRN_HEADS else bk_ref)[0, rows, hk].astype(F32) * jnp.exp2(mid[:, sl] - cum[:, sl])
                ops.append(dict(di=di, fw=fw, c=c, h=h, sl=sl, rows=rows, v=v_ref[0, rows, sl],
                                qs=qs.astype(BF16), ks=ks.astype(BF16), qd=(qs * e_mid[:, sl]).astype(BF16),
                                kd=(ks * e_rest[:, sl]).astype(BF16), e_tot=e_tot[:, sl]))
    if with_out:
        for d in ops:
            d["sc"] = _dot_nt(d["qs"], d["ks"])
        for d in ops:
            d["sc"] = jnp.where(_causal(d["fw"]), d["sc"], 0.0).astype(BF16)
        for d in ops:
            d["o"] = _dot(d["sc"], d["v"])
    for d in ops:
        d["upd"] = _dot_tn(d["v"], d["kd"])
    for di, (fw, _, _, _, _, _, o_ref, st_ref) in enumerate(dirs):
        sts = [st_ref[h] for h in range(N_HEADS)]
        for c in (range(ncb) if fw else range(ncb - 1, -1, -1)):
            for d in ops:
                if d["di"] != di or d["c"] != c:
                    continue
                h = d["h"]
                if with_out:
                    o = d["o"] + _dot_nt(d["qd"], sts[h].astype(BF16))
                    o_ref[0, d["rows"], d["sl"]] = o.astype(o_ref.dtype)
                sts[h] = sts[h] * d["e_tot"] + d["upd"]
        for h in range(N_HEADS):
            st_ref[h] = sts[h]


def _scan_kernel(n_ctx_blocks, *refs):
    s = pl.program_id(1)

    @pl.when(s == 0)
    def _():
        refs[-2][...] = jnp.zeros_like(refs[-2])
        refs[-1][...] = jnp.zeros_like(refs[-1])

    @pl.when(s < n_ctx_blocks)
    def _():
        _scan_block(False, *refs)

    @pl.when(s >= n_ctx_blocks)
    def _():
        _scan_block(True, *refs)


def _scan_call(q, v, akf, akb, bk, gf, gb, lc):
    b_, lt, _ = q.shape
    c_ = SCAN_BLOCK
    ncc = lc // c_
    nlc = (lt - lc) // c_
    n = ncc + nlc

    def fwd(b, s):
        return (b, s, 0)

    def bwd(b, s):
        return (b, jnp.where(s < ncc, ncc - 1 - s, n + ncc - 1 - s), 0)

    def out_fwd(b, s):
        return (b, jnp.maximum(s - ncc, 0), 0)

    def out_bwd(b, s):
        return (b, jnp.where(s < ncc, nlc - 1, n - 1 - s), 0)

    def specs(index_map):
        return [
            pl.BlockSpec((1, c_, 1024), index_map),
            pl.BlockSpec((1, c_, 1024), index_map),
            pl.BlockSpec((1, c_, 512), index_map),
            pl.BlockSpec((1, c_, 512), index_map),
            pl.BlockSpec((1, c_, 1024), index_map),
        ]

    out = jax.ShapeDtypeStruct((b_, lt - lc, 1024), BF16)
    return pl.pallas_call(
        functools.partial(_scan_kernel, ncc),
        grid=(b_, n),
        in_specs=specs(fwd) + specs(bwd),
        out_specs=[pl.BlockSpec((1, c_, 1024), out_fwd), pl.BlockSpec((1, c_, 1024), out_bwd)],
        out_shape=[out, out],
        scratch_shapes=[pltpu.VMEM((N_HEADS, HEAD_W, HEAD_W), F32),
                        pltpu.VMEM((N_HEADS, HEAD_W, HEAD_W), F32)],
        compiler_params=pltpu.CompilerParams(
            dimension_semantics=("arbitrary", "arbitrary"), vmem_limit_bytes=VMEM_LIMIT),
    )(q, v, akf, bk, gf, q, v, akb, bk, gb)


def _merge_kernel(x_ref, a1_ref, s1_ref, gt1_ref, a2_ref, s2_ref, of_ref, ob_ref, wg_ref, nw_ref,
                  wua_ref, wub_ref, wo_ref, gpost_ref, wr_ref, br_ref,
                  x1_ref, h2_ref, idx_ref, prob_ref):
    x = x_ref[0]
    hb = (_rms(x) * a1_ref[0] + s1_ref[0]).astype(BF16)
    og = _dot(hb, wg_ref[:, 0:1024])
    mg_a = _dot(hb, wg_ref[:, 1024:2048])
    mg_b = _dot(hb, wg_ref[:, 2048:3072])
    o = of_ref[0].astype(F32) + ob_ref[0].astype(F32)
    heads = []
    for h in range(N_HEADS):
        sl = slice(h * HEAD_W, (h + 1) * HEAD_W)
        heads.append(_rms(o[:, sl]))
    og = og * _sigmoid(og)
    r = (jnp.concatenate(heads, axis=-1) * nw_ref[...] * og).astype(BF16)
    y_a = _dot(r[:, 0:512], wua_ref[...])
    y_b = _dot(r[:, 512:1024], wub_ref[...])
    mix = _dot((_sigmoid(mg_a) * y_a + _sigmoid(mg_b) * y_b).astype(BF16), wo_ref[...])
    x1 = x + gt1_ref[0] * (_rms(mix) * gpost_ref[...])
    x1_ref[0] = x1
    h2 = _rms(x1) * a2_ref[0] + s2_ref[0]
    nslab = h2.shape[-1] // LANES
    for s in range(nslab):
        h2_ref[pl.ds(s, h2.shape[0], stride=nslab), :] = h2[:, s * LANES:(s + 1) * LANES]
    h_hi, h_lo = _split_bf16(h2)
    logits = _dot(h_hi, wr_ref[0]) + _dot(h_lo, wr_ref[0]) + _dot(h_hi, wr_ref[1]) + br_ref[...]
    lane = lax.broadcasted_iota(jnp.int32, logits.shape, 1).astype(F32)
    vals, idxs = [], []
    for _ in range(TOP_K):
        m = jnp.max(logits, axis=-1, keepdims=True)
        sel = jnp.min(jnp.where(logits == m, lane, float(LANES)), axis=-1, keepdims=True)
        vals.append(m)
        idxs.append(sel)
        logits = jnp.where(lane == sel, -jnp.inf, logits)
    exps = [jnp.exp(v_ - vals[0]) for v_ in vals]
    denom = exps[0] + exps[1] + exps[2] + exps[3]
    idx_out = jnp.zeros_like(lane)
    prob_out = jnp.zeros_like(lane)
    for k_ in range(TOP_K):
        idx_out = jnp.where(lane == float(k_), idxs[k_], idx_out)
        prob_out = jnp.where(lane == float(k_), exps[k_] / denom, prob_out)
    idx_ref[0] = idx_out.astype(jnp.int32)
    prob_ref[0] = prob_out


def _merge_call(x, a1, s1, gt1, a2, s2, o_f, o_b, w_gates, norm_w, w_up_a, w_up_b, w_o, g_post, w_r, b_r):
    b_, l_, d = x.shape
    tm = TOKEN_TILE
    tile = lambda b, t: (b, t, 0)
    per_b = lambda b, t: (b, 0, 0)
    const2 = lambda b, t: (0, 0)
    return pl.pallas_call(
        _merge_kernel,
        grid=(b_, l_ // tm),
        in_specs=[
            pl.BlockSpec((1, tm, d), tile),
            pl.BlockSpec((1, 1, d), per_b), pl.BlockSpec((1, 1, d), per_b), pl.BlockSpec((1, 1, d), per_b),
            pl.BlockSpec((1, 1, d), per_b), pl.BlockSpec((1, 1, d), per_b),
            pl.BlockSpec((1, tm, 1024), tile), pl.BlockSpec((1, tm, 1024), tile),
            pl.BlockSpec(w_gates.shape, const2),
            pl.BlockSpec(norm_w.shape, const2),
            pl.BlockSpec(w_up_a.shape, const2),
            pl.BlockSpec(w_up_b.shape, const2),
            pl.BlockSpec(w_o.shape, const2),
            pl.BlockSpec(g_post.shape, const2),
            pl.BlockSpec(w_r.shape, lambda b, t: (0, 0, 0)),
            pl.BlockSpec(b_r.shape, const2),
        ],
        out_specs=[
            pl.BlockSpec((1, tm, d), tile),
            pl.BlockSpec((tm * (d // LANES), LANES), lambda b, t: (b * (l_ // tm) + t, 0)),
            pl.BlockSpec((1, tm, LANES), tile), pl.BlockSpec((1, tm, LANES), tile),
        ],
        out_shape=[
            jax.ShapeDtypeStruct((b_, l_, d), F32), jax.ShapeDtypeStruct((b_ * l_ * (d // LANES), LANES), F32),
            jax.ShapeDtypeStruct((b_, l_, LANES), jnp.int32), jax.ShapeDtypeStruct((b_, l_, LANES), F32),
        ],
        compiler_params=pltpu.CompilerParams(
            dimension_semantics=("arbitrary", "arbitrary"), vmem_limit_bytes=VMEM_LIMIT),
    )(x, a1, s1, gt1, a2, s2, o_f, o_b, w_gates, norm_w, w_up_a, w_up_b, w_o, g_post, w_r, b_r)


def _moe_kernel(be_ref, nv_ref, tok_ref, tokn_ref, dst_ref, h2_hbm,
                wg_ref, bg_ref, wu_ref, bu_ref, wd_ref, bd_ref, out_hbm,
                xbuf, ybuf, wgb, wub, wdb, gsem, ssem):
    i = pl.program_id(0)
    n = pl.num_programs(0)
    slot = i % 2
    rows = MOE_BLOCK
    nslab = xbuf.shape[0] // (2 * rows)
    blk = rows * nslab

    def slab(ref, start):
        return ref.at[pl.ds(pl.multiple_of(start, nslab), nslab), :]

    group = 8

    def issue_gather(idx_ref, s):
        def body(gi, carry):
            for j in range(group):
                r = gi * group + j
                pltpu.make_async_copy(slab(h2_hbm, idx_ref[0, 0, r]), slab(xbuf, s * blk + r * nslab),
                                      gsem.at[s]).start(priority=j % 2)
            return carry
        lax.fori_loop(0, rows // group, body, 0)

    def wait_gather(s):
        pltpu.make_async_copy(h2_hbm.at[pl.ds(0, blk), :], xbuf.at[pl.ds(pl.multiple_of(s * blk, blk), blk), :],
                              gsem.at[s]).wait()

    def issue_scatter(s):
        def body(gi, carry):
            for j in range(group):
                r = gi * group + j
                pltpu.make_async_copy(slab(ybuf, s * blk + r * nslab), slab(out_hbm, dst_ref[0, 0, r]),
                                      ssem.at[s]).start(priority=j % 2)
            return carry
        lax.fori_loop(0, rows // group, body, 0)

    def wait_scatter(s):
        pltpu.make_async_copy(ybuf.at[pl.ds(pl.multiple_of(s * blk, blk), blk), :], out_hbm.at[pl.ds(0, blk), :],
                              ssem.at[s]).wait()

    @pl.when(jnp.logical_and(i == 0, nv_ref[0] > 0))
    def _():
        issue_gather(tok_ref, 0)

    nxt = jnp.minimum(i + 1, n - 1)

    @pl.when(jnp.logical_and(i + 1 < n, nv_ref[nxt] > 0))
    def _():
        issue_gather(tokn_ref, 1 - slot)

    prev = jnp.maximum(i - 1, 0)

    @pl.when(jnp.logical_or(i == 0, be_ref[i] != be_ref[prev]))
    def _():
        wgb[...] = wg_ref[0].astype(BF16)
        wub[...] = wu_ref[0].astype(BF16)
        wdb[...] = wd_ref[0].astype(BF16)

    @pl.when(jnp.logical_and(i >= 2, nv_ref[jnp.maximum(i - 2, 0)] > 0))
    def _():
        wait_scatter(slot)

    @pl.when(nv_ref[i] > 0)
    def _():
        wait_gather(slot)
        base = slot * blk
        xb = jnp.concatenate([xbuf[pl.ds(base + s, rows, stride=nslab), :] for s in range(nslab)],
                             axis=-1).astype(BF16)
        g = jnp.minimum(_dot(xb, wgb[...]) + bg_ref[0], SWIGLU_LIMIT)
        u = jnp.clip(_dot(xb, wub[...]) + bu_ref[0], -SWIGLU_LIMIT, SWIGLU_LIMIT)
        act = g * _sigmoid(SWIGLU_ALPHA * g) * (u + 1.0)
        y = _dot(act.astype(BF16), wdb[...]) + bd_ref[0]
        for s in range(nslab):
            ybuf[pl.ds(base + s, rows, stride=nslab), :] = y[:, s * LANES:(s + 1) * LANES]
        issue_scatter(slot)

    @pl.when(nv_ref[i] == 0)
    def _():
        ybuf[pl.ds(pl.multiple_of(slot * blk, blk), blk), :] = jnp.zeros((blk, LANES), F32)
        fill = pltpu.make_async_copy(ybuf.at[pl.ds(pl.multiple_of(slot * blk, blk), blk), :],
                                     out_hbm.at[pl.ds(pl.multiple_of(dst_ref[0, 0, 0], nslab), blk), :], ssem.at[slot])
        fill.start()
        fill.wait()

    @pl.when(i == n - 1)
    def _():
        @pl.when(jnp.logical_and(i >= 1, nv_ref[prev] > 0))
        def _():
            wait_scatter(1 - slot)

        @pl.when(nv_ref[i] > 0)
        def _():
            wait_scatter(slot)


def _moe_call(block_expert, n_valid, tok, dst, h2, n_out_rows, w_gate, b_gate, w_up, b_up, w_down, b_down):
    e_, d, f = w_gate.shape
    nslab = d // LANES
    nblk = tok.shape[0]
    rows = MOE_BLOCK

    def blk(i, be, nv):
        return (i, 0, 0)

    def blk_next(i, be, nv):
        return (jnp.minimum(i + 1, nblk - 1), 0, 0)

    def expert(i, be, nv):
        return (be[i], 0, 0)

    smem_rows = functools.partial(pl.BlockSpec, (1, 1, rows), memory_space=pltpu.SMEM)
    grid_spec = pltpu.PrefetchScalarGridSpec(
        num_scalar_prefetch=2,
        grid=(nblk,),
        in_specs=[
            smem_rows(blk), smem_rows(blk_next), smem_rows(blk),
            pl.BlockSpec(memory_space=pl.ANY),
            pl.BlockSpec((1, d, f), expert), pl.BlockSpec((1, 1, f), expert),
            pl.BlockSpec((1, d, f), expert), pl.BlockSpec((1, 1, f), expert),
            pl.BlockSpec((1, f, d), expert), pl.BlockSpec((1, 1, d), expert),
        ],
        out_specs=pl.BlockSpec(memory_space=pl.ANY),
        scratch_shapes=[
            pltpu.VMEM((2 * rows * nslab, LANES), F32), pltpu.VMEM((2 * rows * nslab, LANES), F32),
            pltpu.VMEM((d, f), BF16), pltpu.VMEM((d, f), BF16), pltpu.VMEM((f, d), BF16),
            pltpu.SemaphoreType.DMA((2,)), pltpu.SemaphoreType.DMA((2,)),
        ],
    )
    return pl.pallas_call(
        _moe_kernel,
        grid_spec=grid_spec,
        out_shape=jax.ShapeDtypeStruct((n_out_rows * nslab, LANES), F32),
        compiler_params=pltpu.CompilerParams(
            dimension_semantics=("arbitrary",), vmem_limit_bytes=VMEM_LIMIT),
    )(block_expert, n_valid, tok, tok, dst, h2,
      w_gate, b_gate.reshape(e_, 1, f), w_up, b_up.reshape(e_, 1, f), w_down, b_down.reshape(e_, 1, d))


def _dispatch_plan(top_idx, n_experts, nslab):
    t_ = top_idx.shape[0]
    n_pairs = t_ * TOP_K
    flat_e = top_idx.reshape(-1)
    order = jnp.argsort(flat_e).astype(jnp.int32)
    counts = jnp.sum((flat_e[:, None] == jnp.arange(n_experts, dtype=jnp.int32)[None, :]).astype(jnp.int32), axis=0)
    nblk_e = (counts + MOE_BLOCK - 1) // MOE_BLOCK
    blk_end = jnp.cumsum(nblk_e)
    blk_start = blk_end - nblk_e
    start = jnp.cumsum(counts) - counts
    nblk = n_pairs // MOE_BLOCK + n_experts
    b = jnp.arange(nblk, dtype=jnp.int32)
    be = jnp.minimum(jnp.sum((b[:, None] >= blk_end[None, :]).astype(jnp.int32), axis=1), n_experts - 1)
    first_row = (b - blk_start[be]) * MOE_BLOCK
    n_valid = jnp.clip(counts[be] - first_row, 0, MOE_BLOCK).astype(jnp.int32)
    r = jnp.arange(MOE_BLOCK, dtype=jnp.int32)[None, :]
    valid = r < n_valid[:, None]
    src = jnp.clip(start[be][:, None] + first_row[:, None] + r, 0, n_pairs - 1)
    pair = jnp.where(valid, order[src], 0).astype(jnp.int32)
    tok = pair // TOP_K
    spare = (jnp.cumsum((~valid).reshape(-1).astype(jnp.int32)) - 1).reshape(nblk, MOE_BLOCK)
    dst = jnp.where(valid, (pair % TOP_K) * t_ + tok, n_pairs + spare).astype(jnp.int32)
    n_out_rows = n_pairs + n_experts * MOE_BLOCK
    return (be, n_valid, (tok * nslab).reshape(nblk, 1, MOE_BLOCK), (dst * nslab).reshape(nblk, 1, MOE_BLOCK),
            n_out_rows)


def _final_kernel(x1_ref, y0_ref, y1_ref, y2_ref, y3_ref, p_ref, gt2_ref, gpost_ref, o_ref):
    p = p_ref[0]
    tm, d = x1_ref.shape[1:]
    nslab = d // LANES
    parts = []
    for s in range(nslab):
        acc = p[:, 0:1] * y0_ref[pl.ds(s, tm, stride=nslab), :]
        for k_, y_ref in ((1, y1_ref), (2, y2_ref), (3, y3_ref)):
            acc = acc + p[:, k_:k_ + 1] * y_ref[pl.ds(s, tm, stride=nslab), :]
        parts.append(acc)
    ffn = jnp.concatenate(parts, axis=-1)
    o_ref[0] = x1_ref[0] + gt2_ref[0] * (_rms(ffn) * gpost_ref[...])


def _final_call(x1, y, probs, gt2, g_post):
    b_, l_, d = x1.shape
    tm = TOKEN_TILE
    nslab = d // LANES
    tiles_per_k = b_ * l_ // tm
    tile = lambda b, t: (b, t, 0)

    def y_spec(k_):
        return pl.BlockSpec((tm * nslab, LANES), lambda b, t: (k_ * tiles_per_k + b * (l_ // tm) + t, 0))

    return pl.pallas_call(
        _final_kernel,
        grid=(b_, l_ // tm),
        in_specs=[
            pl.BlockSpec((1, tm, d), tile),
            y_spec(0), y_spec(1), y_spec(2), y_spec(3),
            pl.BlockSpec((1, tm, LANES), tile),
            pl.BlockSpec((1, 1, d), lambda b, t: (b, 0, 0)),
            pl.BlockSpec(g_post.shape, lambda b, t: (0, 0)),
        ],
        out_specs=pl.BlockSpec((1, tm, d), tile),
        out_shape=jax.ShapeDtypeStruct((b_, l_, d), F32),
        compiler_params=pltpu.CompilerParams(
            dimension_semantics=("arbitrary", "arbitrary"), vmem_limit_bytes=VMEM_LIMIT),
    )(x1, y, y, y, y, probs, gt2, g_post)


def _pad_heads(w, n_heads, dk):
    d = w.shape[0]
    w = w.reshape(d, n_heads, dk)
    return jnp.pad(w, ((0, 0), (0, 0), (0, HEAD_W - dk))).reshape(d, n_heads * HEAD_W)


def kernel(x, c, ctx, c_ctx, w_ada, b_ada, g_pre_mix, g_post_mix, g_pre_ffn, g_post_ffn, w_in, hgrn_lb, hgrn_norm_w, gla_gk_w2, gla_gk_b, gla_norm_w, w_up_a, w_up_b, w_o, w_router, b_router, w_gate, b_gate, w_up, b_up, w_down, b_down):
    b_, l_, d = x.shape
    lc = ctx.shape[1]
    n_experts = w_router.shape[-1]
    layer = 0
    hw = HGRN_HEADS * HGRN_DK
    kw = GLA_HEADS * GLA_DK
    vw = GLA_HEADS * GLA_DV
    rk = GLA_GATE_RANK

    rows = -(-(b_ + 1) // 8) * 8
    cc = jnp.concatenate([c, c_ctx[None, :], jnp.zeros((rows - b_ - 1, d), F32)], axis=0)
    mod = _ada_call(cc, w_ada[layer], b_ada[layer])[:b_ + 1]
    sh1, sc1, gt1, sh2, sc2, gt2 = [m.reshape(b_ + 1, 1, d) for m in jnp.split(mod, 6, axis=-1)]
    a1 = g_pre_mix[layer] * (1.0 + sc1)
    a2 = g_pre_ffn[layer] * (1.0 + sc2)

    w = w_in[layer]
    o0 = 0
    cols = {}
    for name, size in (("qa", hw), ("zf", hw), ("zb", hw), ("ia", hw), ("oga", hw), ("qb", kw), ("kb", kw),
                       ("vb", vw), ("rf", rk), ("rb", rk), ("ogb", vw), ("mga", d), ("mgb", d)):
        cols[name] = w[:, o0:o0 + size]
        o0 += size
    w_scan = jnp.concatenate([
        cols["qa"], cols["zf"], cols["zb"], cols["ia"],
        _pad_heads(cols["qb"] * (GLA_DK ** -0.5), GLA_HEADS, GLA_DK),
        _pad_heads(cols["kb"], GLA_HEADS, GLA_DK), cols["vb"]], axis=1).astype(BF16)
    w_rank = jnp.pad(jnp.concatenate([cols["rf"], cols["rb"]], axis=1), ((0, 0), (0, LANES - 2 * rk))).astype(BF16)
    w2 = jnp.zeros((LANES, 2 * GLA_HEADS * HEAD_W), F32)
    w2 = w2.at[0:rk, 0:GLA_HEADS * HEAD_W].set(_pad_heads(gla_gk_w2[layer, 0], GLA_HEADS, GLA_DK))
    w2 = w2.at[rk:2 * rk, GLA_HEADS * HEAD_W:].set(_pad_heads(gla_gk_w2[layer, 1], GLA_HEADS, GLA_DK))
    w2 = w2.astype(BF16)
    gkb = jnp.concatenate([_pad_heads(gla_gk_b[layer, 0][None, :], GLA_HEADS, GLA_DK),
                           _pad_heads(gla_gk_b[layer, 1][None, :], GLA_HEADS, GLA_DK)], axis=1)
    lb = jnp.cumsum(jax.nn.softmax(hgrn_lb.astype(F32), axis=0), axis=0)[layer]
    w_gates = jnp.concatenate([cols["oga"], cols["ogb"], cols["mga"], cols["mgb"]], axis=1).astype(BF16)
    norm_w = jnp.concatenate([jnp.tile(hgrn_norm_w[layer], HGRN_HEADS), jnp.tile(gla_norm_w[layer], GLA_HEADS)])[None, :]
    w_r = jnp.stack(_split_bf16(jnp.pad(w_router[layer], ((0, 0), (0, LANES - n_experts)))))
    b_r = jnp.pad(b_router[layer], (0, LANES - n_experts), constant_values=-1e30)[None, :]

    q, v, akf, akb, bk, gf, gb = _inproj_call(x, ctx, a1, sh1, w_scan, w_rank, w2, gkb, lb)
    o_f, o_b = _scan_call(q, v, akf, akb, bk, gf, gb, lc)
    x1, h2, top_idx, probs = _merge_call(
        x, a1, sh1, gt1, a2, sh2, o_f, o_b, w_gates, norm_w,
        w_up_a[layer].astype(BF16), w_up_b[layer].astype(BF16), w_o[layer].astype(BF16),
        g_post_mix[layer][None, :], w_r, b_r)

    t_ = b_ * l_
    be, n_valid, tok, dst, n_out_rows = _dispatch_plan(top_idx.reshape(t_, LANES)[:, :TOP_K], n_experts, d // LANES)
    y = _moe_call(be, n_valid, tok, dst, h2, n_out_rows,
                  w_gate[layer], b_gate[layer], w_up[layer], b_up[layer], w_down[layer], b_down[layer])
    return _final_call(x1, y, probs, gt2, g_post_ffn[layer][None, :])
```

```python
import functools

import jax
import jax.numpy as jnp
from jax import lax
from jax.experimental import pallas as pl
from jax.experimental.pallas import tpu as pltpu
from jax.experimental.pallas import tpu_sc as plsc

F32 = jnp.float32
BF16 = jnp.bfloat16

NORM_EPS = 1e-6
HGRN_HEADS = 4
HGRN_DK = 128
GLA_HEADS = 4
GLA_DK = 64
GLA_DV = 128
GLA_GATE_RANK = 16
GLA_GATE_NORMALIZER = 16.0
N_HEADS = HGRN_HEADS + GLA_HEADS
HEAD_W = 128
TOP_K = 4
SWIGLU_LIMIT = 7.0
SWIGLU_ALPHA = 1.702
MOE_BLOCK = 256
MOE_GROUPS = 4
SCAN_CHUNK = 64
SCAN_BLOCK = 256
LOG2E = 1.4426950408889634
TOKEN_TILE = 256
LANES = 128
VMEM_LIMIT = 56 * 1024 * 1024


def _dot(a, b):
    return jnp.dot(a, b, preferred_element_type=F32)


def _dot_nt(a, b):
    return lax.dot_general(a, b, (((1,), (1,)), ((), ())), preferred_element_type=F32)


def _dot_tn(a, b):
    return lax.dot_general(a, b, (((0,), (0,)), ((), ())), preferred_element_type=F32)


def _split_bf16(a):
    hi = a.astype(BF16)
    lo = (a - hi.astype(F32)).astype(BF16)
    return hi, lo


def _dot3(a, b):
    ah, al = _split_bf16(a)
    bh, bl = _split_bf16(b)
    return _dot(ah, bh) + _dot(al, bh) + _dot(ah, bl)


def _rms(x):
    return x * lax.rsqrt(jnp.mean(x * x, axis=-1, keepdims=True) + NORM_EPS)


def _sigmoid(x):
    return 1.0 / (1.0 + jnp.exp(-x))


def _ada_kernel(c_ref, w_ref, b_ref, o_ref):
    c = c_ref[...]
    o_ref[...] = _dot3(c * _sigmoid(c), w_ref[...]) + b_ref[...]


def _ada_call(cc, w_ada, b_ada):
    rows, d = cc.shape
    n = w_ada.shape[1]
    tn = 1536
    return pl.pallas_call(
        _ada_kernel,
        grid=(n // tn,),
        in_specs=[
            pl.BlockSpec((rows, d), lambda j: (0, 0)),
            pl.BlockSpec((d, tn), lambda j: (0, j)),
            pl.BlockSpec((1, tn), lambda j: (0, j)),
        ],
        out_specs=pl.BlockSpec((rows, tn), lambda j: (0, j)),
        out_shape=jax.ShapeDtypeStruct((rows, n), F32),
        compiler_params=pltpu.CompilerParams(
            dimension_semantics=("arbitrary",), vmem_limit_bytes=VMEM_LIMIT),
    )(cc, w_ada, b_ada.reshape(1, n))


def _inproj_kernel(n_ctx_tiles, x_ref, ctx_ref, a_ref, s_ref, w_ref, wr_ref, w2_ref, gkb_ref, lb_ref,
                   q_ref, v_ref, akf_ref, akb_ref, bk_ref, gf_ref, gb_ref):
    t = pl.program_id(1)
    xt = jnp.where(t < n_ctx_tiles, ctx_ref[0], x_ref[0])
    hb = (_rms(xt) * a_ref[0] + s_ref[0]).astype(BF16)

    def seg(j):
        return _dot(hb, w_ref[:, j * 512:(j + 1) * 512])

    z_f, z_b = seg(1), seg(2)
    r = _dot(hb, wr_ref[...]).astype(BF16)
    pre = _dot(r, w2_ref[...]) + gkb_ref[...]
    plain = [seg(j) for j in (0, 4, 3, 6, 5)]
    for j, z, k_ref, g_ref in ((1, z_f, akf_ref, gf_ref), (2, z_b, akb_ref, gb_ref)):
        lb = lb_ref[j - 1:j, :]
        sg = _sigmoid(z)
        g_ref[0, :, 0:512] = jnp.log2(lb + (1.0 - lb) * sg)
        k_ref[0] = ((1.0 - lb) * (1.0 - sg)).astype(BF16)
    ls = (jnp.minimum(pre, 0.0) - jnp.log(1.0 + jnp.exp(-jnp.abs(pre)))) * (LOG2E / GLA_GATE_NORMALIZER)
    gf_ref[0, :, 512:1024] = ls[:, 0:512]
    gb_ref[0, :, 512:1024] = ls[:, 512:1024]
    q_ref[0, :, 0:512] = plain[0].astype(BF16)
    q_ref[0, :, 512:1024] = plain[1].astype(BF16)
    v_ref[0, :, 0:512] = plain[2].astype(BF16)
    v_ref[0, :, 512:1024] = plain[3].astype(BF16)
    bk_ref[0] = plain[4].astype(BF16)


def _inproj_call(x, ctx, mod_a, mod_s, w_scan, w_rank, w2, gkb, lb):
    b_, l_, d = x.shape
    lc = ctx.shape[1]
    tm = TOKEN_TILE
    nct, nlt = lc // tm, l_ // tm
    lt = lc + l_

    def x_map(b, t):
        return (b, jnp.maximum(t - nct, 0), 0)

    def ctx_map(b, t):
        return (b, jnp.minimum(t, nct - 1), 0)

    def mod_map(b, t):
        return (jnp.where(t < nct, b_, b), 0, 0)

    const2 = lambda b, t: (0, 0)
    out_map = lambda b, t: (b, t, 0)
    wide = jax.ShapeDtypeStruct((b_, lt, 1024), BF16)
    half = jax.ShapeDtypeStruct((b_, lt, 512), BF16)
    gate = jax.ShapeDtypeStruct((b_, lt, 1024), F32)
    return pl.pallas_call(
        functools.partial(_inproj_kernel, nct),
        grid=(b_, nct + nlt),
        in_specs=[
            pl.BlockSpec((1, tm, d), x_map),
            pl.BlockSpec((1, tm, d), ctx_map),
            pl.BlockSpec((1, 1, d), mod_map),
            pl.BlockSpec((1, 1, d), mod_map),
            pl.BlockSpec(w_scan.shape, const2),
            pl.BlockSpec(w_rank.shape, const2),
            pl.BlockSpec(w2.shape, const2),
            pl.BlockSpec(gkb.shape, const2),
            pl.BlockSpec(lb.shape, const2),
        ],
        out_specs=[
            pl.BlockSpec((1, tm, 1024), out_map),
            pl.BlockSpec((1, tm, 1024), out_map),
            pl.BlockSpec((1, tm, 512), out_map),
            pl.BlockSpec((1, tm, 512), out_map),
            pl.BlockSpec((1, tm, 512), out_map),
            pl.BlockSpec((1, tm, 1024), out_map),
            pl.BlockSpec((1, tm, 1024), out_map),
        ],
        out_shape=[wide, wide, half, half, half, gate, gate],
        compiler_params=pltpu.CompilerParams(
            dimension_semantics=("arbitrary", "arbitrary"), vmem_limit_bytes=VMEM_LIMIT),
    )(x, ctx, mod_a, mod_s, w_scan, w_rank, w2, gkb, lb)


def _causal(forward):
    row = lax.broadcasted_iota(jnp.int32, (SCAN_CHUNK, SCAN_CHUNK), 0)
    col = lax.broadcasted_iota(jnp.int32, (SCAN_CHUNK, SCAN_CHUNK), 1)
    return (row >= col) if forward else (col >= row)


def _scan_block(with_out, qf_ref, vf_ref, akf_ref, bkf_ref, gf_ref, qb_ref, vb_ref, akb_ref, bkb_ref, gb_ref,
                of_ref, ob_ref, sf_ref, sb_ref):
    c_ = SCAN_CHUNK
    ncb = qf_ref.shape[1] // c_
    dirs = [(True, qf_ref, vf_ref, akf_ref, bkf_ref, gf_ref, of_ref, sf_ref),
            (False, qb_ref, vb_ref, akb_ref, bkb_ref, gb_ref, ob_ref, sb_ref)]
    cums = {}
    for di, (fw, _, _, _, _, g_ref, _, _) in enumerate(dirs):
        tri = jnp.where(_causal(fw), 1.0, 0.0).astype(BF16)
        for c in range(ncb):
            g_hi, g_lo = _split_bf16(g_ref[0, c * c_:(c + 1) * c_, :])
            cums[di, c] = _dot(tri, g_hi) + _dot(tri, g_lo)
    ops = []
    for di, (fw, q_ref, v_ref, ak_ref, bk_ref, _, _, _) in enumerate(dirs):
        end_row, mid_row = (c_ - 1, c_ // 2 - 1) if fw else (0, c_ // 2)
        for c in range(ncb):
            rows = slice(c * c_, (c + 1) * c_)
            cum = cums[di, c]
            total = cum[end_row:end_row + 1, :]
            mid = cum[mid_row:mid_row + 1, :]
            e_mid = jnp.exp2(mid)
            e_rest = jnp.exp2(total - mid)
            e_tot = jnp.exp2(total)
            for h in range(N_HEADS):
                sl = slice(h * HEAD_W, (h + 1) * HEAD_W)
                hk = slice((h % HGRN_HEADS) * HEAD_W, (h % HGRN_HEADS + 1) * HEAD_W)
                qs = q_ref[0, rows, sl].astype(F32) * jnp.exp2(cum[:, sl] - mid[:, sl])
                ks = (ak_ref if h < HGRN_HEADS else bk_ref)[0, rows, hk].astype(F32) * jnp.exp2(mid[:, sl] - cum[:, sl])
                ops.append(dict(di=di, fw=fw, c=c, h=h, sl=sl, rows=rows, v=v_ref[0, rows, sl],
                                qs=qs.astype(BF16), ks=ks.astype(BF16), qd=(qs * e_mid[:, sl]).astype(BF16),
                                kd=(ks * e_rest[:, sl]).astype(BF16), e_tot=e_tot[:, sl]))
    if with_out:
        for d in ops:
            d["sc"] = _dot_nt(d["qs"], d["ks"])
        for d in ops:
            d["sc"] = jnp.where(_causal(d["fw"]), d["sc"], 0.0).astype(BF16)
        for d in ops:
            d["o"] = _dot(d["sc"], d["v"])
    for d in ops:
        d["upd"] = _dot_tn(d["v"], d["kd"])
    for di, (fw, _, _, _, _, _, o_ref, st_ref) in enumerate(dirs):
        sts = [st_ref[h] for h in range(N_HEADS)]
        for c in (range(ncb) if fw else range(ncb - 1, -1, -1)):
            for d in ops:
                if d["di"] != di or d["c"] != c:
                    continue
                h = d["h"]
                if with_out:
                    o = d["o"] + _dot_nt(d["qd"], sts[h].astype(BF16))
                    o_ref[0, d["rows"], d["sl"]] = o.astype(o_ref.dtype)
                sts[h] = sts[h] * d["e_tot"] + d["upd"]
        for h in range(N_HEADS):
            st_ref[h] = sts[h]


def _scan_kernel(n_ctx_blocks, *refs):
    s = pl.program_id(1)

    @pl.when(s == 0)
    def _():
        refs[-2][...] = jnp.zeros_like(refs[-2])
        refs[-1][...] = jnp.zeros_like(refs[-1])

    @pl.when(s < n_ctx_blocks)
    def _():
        _scan_block(False, *refs)

    @pl.when(s >= n_ctx_blocks)
    def _():
        _scan_block(True, *refs)


def _scan_call(q, v, akf, akb, bk, gf, gb, lc):
    b_, lt, _ = q.shape
    c_ = SCAN_BLOCK
    ncc = lc // c_
    nlc = (lt - lc) // c_
    n = ncc + nlc

    def fwd(b, s):
        return (b, s, 0)

    def bwd(b, s):
        return (b, jnp.where(s < ncc, ncc - 1 - s, n + ncc - 1 - s), 0)

    def out_fwd(b, s):
        return (b, jnp.maximum(s - ncc, 0), 0)

    def out_bwd(b, s):
        return (b, jnp.where(s < ncc, nlc - 1, n - 1 - s), 0)

    def specs(index_map):
        return [
            pl.BlockSpec((1, c_, 1024), index_map),
            pl.BlockSpec((1, c_, 1024), index_map),
            pl.BlockSpec((1, c_, 512), index_map),
            pl.BlockSpec((1, c_, 512), index_map),
            pl.BlockSpec((1, c_, 1024), index_map),
        ]

    out = jax.ShapeDtypeStruct((b_, lt - lc, 1024), BF16)
    return pl.pallas_call(
        functools.partial(_scan_kernel, ncc),
        grid=(b_, n),
        in_specs=specs(fwd) + specs(bwd),
        out_specs=[pl.BlockSpec((1, c_, 1024), out_fwd), pl.BlockSpec((1, c_, 1024), out_bwd)],
        out_shape=[out, out],
        scratch_shapes=[pltpu.VMEM((N_HEADS, HEAD_W, HEAD_W), F32),
                        pltpu.VMEM((N_HEADS, HEAD_W, HEAD_W), F32)],
        compiler_params=pltpu.CompilerParams(
            dimension_semantics=("arbitrary", "arbitrary"), vmem_limit_bytes=VMEM_LIMIT),
    )(q, v, akf, bk, gf, q, v, akb, bk, gb)


def _merge_kernel(x_ref, a1_ref, s1_ref, gt1_ref, a2_ref, s2_ref, of_ref, ob_ref, wg_ref, nw_ref,
                  wua_ref, wub_ref, wo_ref, gpost_ref, wr_ref, br_ref,
                  x1_ref, h2_ref, idx_ref, prob_ref):
    x = x_ref[0]
    hb = (_rms(x) * a1_ref[0] + s1_ref[0]).astype(BF16)
    og = _dot(hb, wg_ref[:, 0:1024])
    mg_a = _dot(hb, wg_ref[:, 1024:2048])
    mg_b = _dot(hb, wg_ref[:, 2048:3072])
    o = of_ref[0].astype(F32) + ob_ref[0].astype(F32)
    heads = []
    for h in range(N_HEADS):
        sl = slice(h * HEAD_W, (h + 1) * HEAD_W)
        heads.append(_rms(o[:, sl]))
    og = og * _sigmoid(og)
    r = (jnp.concatenate(heads, axis=-1) * nw_ref[...] * og).astype(BF16)
    y_a = _dot(r[:, 0:512], wua_ref[...])
    y_b = _dot(r[:, 512:1024], wub_ref[...])
    mix = _dot((_sigmoid(mg_a) * y_a + _sigmoid(mg_b) * y_b).astype(BF16), wo_ref[...])
    x1 = x + gt1_ref[0] * (_rms(mix) * gpost_ref[...])
    x1_ref[0] = x1
    h2 = _rms(x1) * a2_ref[0] + s2_ref[0]
    nslab = h2.shape[-1] // LANES
    for s in range(nslab):
        h2_ref[pl.ds(s, h2.shape[0], stride=nslab), :] = h2[:, s * LANES:(s + 1) * LANES]
    h_hi, h_lo = _split_bf16(h2)
    logits = _dot(h_hi, wr_ref[0]) + _dot(h_lo, wr_ref[0]) + _dot(h_hi, wr_ref[1]) + br_ref[...]
    lane = lax.broadcasted_iota(jnp.int32, logits.shape, 1).astype(F32)
    vals, idxs = [], []
    for _ in range(TOP_K):
        m = jnp.max(logits, axis=-1, keepdims=True)
        sel = jnp.min(jnp.where(logits == m, lane, float(LANES)), axis=-1, keepdims=True)
        vals.append(m)
        idxs.append(sel)
        logits = jnp.where(lane == sel, -jnp.inf, logits)
    exps = [jnp.exp(v_ - vals[0]) for v_ in vals]
    denom = exps[0] + exps[1] + exps[2] + exps[3]
    idx_out = jnp.zeros_like(lane)
    prob_out = jnp.zeros_like(lane)
    for k_ in range(TOP_K):
        idx_out = jnp.where(lane == float(k_), idxs[k_], idx_out)
        prob_out = jnp.where(lane == float(k_), exps[k_] / denom, prob_out)
    idx_ref[0] = idx_out.astype(jnp.int32)
    prob_ref[0] = prob_out


def _merge_call(x, a1, s1, gt1, a2, s2, o_f, o_b, w_gates, norm_w, w_up_a, w_up_b, w_o, g_post, w_r, b_r):
    b_, l_, d = x.shape
    tm = TOKEN_TILE
    tile = lambda b, t: (b, t, 0)
    per_b = lambda b, t: (b, 0, 0)
    const2 = lambda b, t: (0, 0)
    return pl.pallas_call(
        _merge_kernel,
        grid=(b_, l_ // tm),
        in_specs=[
            pl.BlockSpec((1, tm, d), tile),
            pl.BlockSpec((1, 1, d), per_b), pl.BlockSpec((1, 1, d), per_b), pl.BlockSpec((1, 1, d), per_b),
            pl.BlockSpec((1, 1, d), per_b), pl.BlockSpec((1, 1, d), per_b),
            pl.BlockSpec((1, tm, 1024), tile), pl.BlockSpec((1, tm, 1024), tile),
            pl.BlockSpec(w_gates.shape, const2),
            pl.BlockSpec(norm_w.shape, const2),
            pl.BlockSpec(w_up_a.shape, const2),
            pl.BlockSpec(w_up_b.shape, const2),
            pl.BlockSpec(w_o.shape, const2),
            pl.BlockSpec(g_post.shape, const2),
            pl.BlockSpec(w_r.shape, lambda b, t: (0, 0, 0)),
            pl.BlockSpec(b_r.shape, const2),
        ],
        out_specs=[
            pl.BlockSpec((1, tm, d), tile),
            pl.BlockSpec((tm * (d // LANES), LANES), lambda b, t: (b * (l_ // tm) + t, 0)),
            pl.BlockSpec((1, tm, LANES), tile), pl.BlockSpec((1, tm, LANES), tile),
        ],
        out_shape=[
            jax.ShapeDtypeStruct((b_, l_, d), F32), jax.ShapeDtypeStruct((b_ * l_ * (d // LANES), LANES), F32),
            jax.ShapeDtypeStruct((b_, l_, LANES), jnp.int32), jax.ShapeDtypeStruct((b_, l_, LANES), F32),
        ],
        compiler_params=pltpu.CompilerParams(
            dimension_semantics=("arbitrary", "arbitrary"), vmem_limit_bytes=VMEM_LIMIT),
    )(x, a1, s1, gt1, a2, s2, o_f, o_b, w_gates, norm_w, w_up_a, w_up_b, w_o, g_post, w_r, b_r)


def _moe_kernel(be_ref, nv_ref, tok_ref, tokn_ref, dst_ref, h2_hbm,
                wg_ref, bg_ref, wu_ref, bu_ref, wd_ref, bd_ref, out_hbm,
                xbuf, ybuf, wgb, wub, wdb, gsem, ssem):
    i = pl.program_id(0)
    n = pl.num_programs(0)
    slot = i % 2
    rows = MOE_BLOCK
    nslab = xbuf.shape[0] // (2 * rows)
    blk = rows * nslab

    def slab(ref, start):
        return ref.at[pl.ds(pl.multiple_of(start, nslab), nslab), :]

    group = 8

    def issue_gather(idx_ref, s):
        def body(gi, carry):
            for j in range(group):
                r = gi * group + j
                pltpu.make_async_copy(slab(h2_hbm, idx_ref[0, 0, r]), slab(xbuf, s * blk + r * nslab),
                                      gsem.at[s]).start(priority=j % 2)
            return carry
        lax.fori_loop(0, rows // group, body, 0)

    def wait_gather(s):
        pltpu.make_async_copy(h2_hbm.at[pl.ds(0, blk), :], xbuf.at[pl.ds(pl.multiple_of(s * blk, blk), blk), :],
                              gsem.at[s]).wait()

    def issue_scatter(s):
        def body(gi, carry):
            for j in range(group):
                r = gi * group + j
                pltpu.make_async_copy(slab(ybuf, s * blk + r * nslab), slab(out_hbm, dst_ref[0, 0, r]),
                                      ssem.at[s]).start(priority=j % 2)
            return carry
        lax.fori_loop(0, rows // group, body, 0)

    def wait_scatter(s):
        pltpu.make_async_copy(ybuf.at[pl.ds(pl.multiple_of(s * blk, blk), blk), :], out_hbm.at[pl.ds(0, blk), :],
                              ssem.at[s]).wait()

    @pl.when(jnp.logical_and(i == 0, nv_ref[0] > 0))
    def _():
        issue_gather(tok_ref, 0)

    nxt = jnp.minimum(i + 1, n - 1)

    @pl.when(jnp.logical_and(i + 1 < n, nv_ref[nxt] > 0))
    def _():
        issue_gather(tokn_ref, 1 - slot)

    prev = jnp.maximum(i - 1, 0)

    @pl.when(jnp.logical_or(i == 0, be_ref[i] != be_ref[prev]))
    def _():
        wgb[...] = wg_ref[0].astype(BF16)
        wub[...] = wu_ref[0].astype(BF16)
        wdb[...] = wd_ref[0].astype(BF16)

    @pl.when(jnp.logical_and(i >= 2, nv_ref[jnp.maximum(i - 2, 0)] > 0))
    def _():
        wait_scatter(slot)

    @pl.when(nv_ref[i] > 0)
    def _():
        wait_gather(slot)
        base = slot * blk
        xb = jnp.concatenate([xbuf[pl.ds(base + s, rows, stride=nslab), :] for s in range(nslab)],
                             axis=-1).astype(BF16)
        g = jnp.minimum(_dot(xb, wgb[...]) + bg_ref[0], SWIGLU_LIMIT)
        u = jnp.clip(_dot(xb, wub[...]) + bu_ref[0], -SWIGLU_LIMIT, SWIGLU_LIMIT)
        act = g * _sigmoid(SWIGLU_ALPHA * g) * (u + 1.0)
        y = _dot(act.astype(BF16), wdb[...]) + bd_ref[0]
        for s in range(nslab):
            ybuf[pl.ds(base + s, rows, stride=nslab), :] = y[:, s * LANES:(s + 1) * LANES]
        issue_scatter(slot)

    @pl.when(nv_ref[i] == 0)
    def _():
        ybuf[pl.ds(pl.multiple_of(slot * blk, blk), blk), :] = jnp.zeros((blk, LANES), F32)
        fill = pltpu.make_async_copy(ybuf.at[pl.ds(pl.multiple_of(slot * blk, blk), blk), :],
                                     out_hbm.at[pl.ds(pl.multiple_of(dst_ref[0, 0, 0], nslab), blk), :], ssem.at[slot])
        fill.start()
        fill.wait()

    @pl.when(i == n - 1)
    def _():
        @pl.when(jnp.logical_and(i >= 1, nv_ref[prev] > 0))
        def _():
            wait_scatter(1 - slot)

        @pl.when(nv_ref[i] > 0)
        def _():
            wait_scatter(slot)


def _moe_call(block_expert, n_valid, tok, dst, h2, n_out_rows, w_gate, b_gate, w_up, b_up, w_down, b_down):
    e_, d, f = w_gate.shape
    nslab = d // LANES
    nblk = tok.shape[0]
    rows = MOE_BLOCK

    def blk(i, be, nv):
        return (i, 0, 0)

    def blk_next(i, be, nv):
        return (jnp.minimum(i + 1, nblk - 1), 0, 0)

    def expert(i, be, nv):
        return (be[i], 0, 0)

    smem_rows = functools.partial(pl.BlockSpec, (1, 1, rows), memory_space=pltpu.SMEM)
    grid_spec = pltpu.PrefetchScalarGridSpec(
        num_scalar_prefetch=2,
        grid=(nblk,),
        in_specs=[
            smem_rows(blk), smem_rows(blk_next), smem_rows(blk),
            pl.BlockSpec(memory_space=pl.ANY),
            pl.BlockSpec((1, d, f), expert), pl.BlockSpec((1, 1, f), expert),
            pl.BlockSpec((1, d, f), expert), pl.BlockSpec((1, 1, f), expert),
            pl.BlockSpec((1, f, d), expert), pl.BlockSpec((1, 1, d), expert),
        ],
        out_specs=pl.BlockSpec(memory_space=pl.ANY),
        scratch_shapes=[
            pltpu.VMEM((2 * rows * nslab, LANES), F32), pltpu.VMEM((2 * rows * nslab, LANES), F32),
            pltpu.VMEM((d, f), BF16), pltpu.VMEM((d, f), BF16), pltpu.VMEM((f, d), BF16),
            pltpu.SemaphoreType.DMA((2,)), pltpu.SemaphoreType.DMA((2,)),
        ],
    )
    return pl.pallas_call(
        _moe_kernel,
        grid_spec=grid_spec,
        out_shape=jax.ShapeDtypeStruct((n_out_rows * nslab, LANES), F32),
        compiler_params=pltpu.CompilerParams(
            dimension_semantics=("arbitrary",), vmem_limit_bytes=VMEM_LIMIT),
    )(block_expert, n_valid, tok, tok, dst, h2,
      w_gate, b_gate.reshape(e_, 1, f), w_up, b_up.reshape(e_, 1, f), w_down, b_down.reshape(e_, 1, d))


def _dispatch_plan(top_idx, n_experts, nslab):
    t_ = top_idx.shape[0]
    n_pairs = t_ * TOP_K
    flat_e = top_idx.reshape(-1)
    order = jnp.argsort(flat_e).astype(jnp.int32)
    counts = jnp.sum((flat_e[:, None] == jnp.arange(n_experts, dtype=jnp.int32)[None, :]).astype(jnp.int32), axis=0)
    nblk_e = (counts + MOE_BLOCK - 1) // MOE_BLOCK
    blk_end = jnp.cumsum(nblk_e)
    blk_start = blk_end - nblk_e
    start = jnp.cumsum(counts) - counts
    nblk = n_pairs // MOE_BLOCK + n_experts
    b = jnp.arange(nblk, dtype=jnp.int32)
    be = jnp.minimum(jnp.sum((b[:, None] >= blk_end[None, :]).astype(jnp.int32), axis=1), n_experts - 1)
    first_row = (b - blk_start[be]) * MOE_BLOCK
    n_valid = jnp.clip(counts[be] - first_row, 0, MOE_BLOCK).astype(jnp.int32)
    r = jnp.arange(MOE_BLOCK, dtype=jnp.int32)[None, :]
    valid = r < n_valid[:, None]
    src = jnp.clip(start[be][:, None] + first_row[:, None] + r, 0, n_pairs - 1)
    pair = jnp.where(valid, order[src], 0).astype(jnp.int32)
    tok = pair // TOP_K
    spare = (jnp.cumsum((~valid).reshape(-1).astype(jnp.int32)) - 1).reshape(nblk, MOE_BLOCK)
    dst = jnp.where(valid, (pair % TOP_K) * t_ + tok, n_pairs + spare).astype(jnp.int32)
    n_out_rows = n_pairs + n_experts * MOE_BLOCK
    return (be, n_valid, (tok * nslab).reshape(nblk, 1, MOE_BLOCK), (dst * nslab).reshape(nblk, 1, MOE_BLOCK),
            n_out_rows)


SC_WINDOW = 128


def _sc_gather(src, indices):
    n_idx = indices.shape[0]
    width = src.shape[1]
    mesh = plsc.VectorSubcoreMesh(core_axis_name="c", subcore_axis_name="s")

    @pl.kernel(out_type=jax.ShapeDtypeStruct((n_idx, width), src.dtype), mesh=mesh, scratch_types=[])
    def gather_kernel(x_hbm, i_hbm, o_hbm):
        def body(i_vmem, o_vmem):
            pltpu.sync_copy(x_hbm.at[i_vmem.at[0]], o_vmem)

        pltpu.emit_pipeline(
            body,
            grid=(n_idx // SC_WINDOW,),
            in_specs=[pl.BlockSpec((1, SC_WINDOW), index_map=lambda i: (0, i))],
            out_specs=[pl.BlockSpec((SC_WINDOW, width), index_map=lambda i: (i, 0))],
            core_axis_name=("c", "s"),
            dimension_semantics=(pltpu.PARALLEL,),
        )(i_hbm, o_hbm)

    return gather_kernel(src, indices.reshape(1, n_idx))


def _moe_x_kernel(aliased, be_ref, nv_ref, dst_ref, x_ref, wg_ref, bg_ref, wu_ref, bu_ref, wd_ref, bd_ref, *rest):
    out_hbm, ybuf, wgb, wub, wdb, ssem = rest[1:] if aliased else rest
    i = pl.program_id(0)
    n = pl.num_programs(0)
    slot = i % 2
    rows = MOE_BLOCK
    nslab = x_ref.shape[0] // rows
    blk = rows * nslab
    group = 8

    def slab(ref, start):
        return ref.at[pl.ds(pl.multiple_of(start, nslab), nslab), :]

    def issue_scatter(s):
        def body(gi, carry):
            for j in range(group):
                r = gi * group + j
                pltpu.make_async_copy(slab(ybuf, s * blk + r * nslab), slab(out_hbm, dst_ref[0, 0, r]),
                                      ssem.at[s]).start(priority=j % 2)
            return carry
        lax.fori_loop(0, rows // group, body, 0)

    def wait_scatter(s):
        pltpu.make_async_copy(ybuf.at[pl.ds(pl.multiple_of(s * blk, blk), blk), :], out_hbm.at[pl.ds(0, blk), :],
                              ssem.at[s]).wait()

    prev = jnp.maximum(i - 1, 0)

    @pl.when(jnp.logical_or(i == 0, be_ref[i] != be_ref[prev]))
    def _():
        wgb[...] = wg_ref[0].astype(BF16)
        wub[...] = wu_ref[0].astype(BF16)
        wdb[...] = wd_ref[0].astype(BF16)

    @pl.when(jnp.logical_and(i >= 2, nv_ref[jnp.maximum(i - 2, 0)] > 0))
    def _():
        wait_scatter(slot)

    @pl.when(nv_ref[i] > 0)
    def _():
        base = slot * blk
        xb = jnp.concatenate([x_ref[pl.ds(s, rows, stride=nslab), :] for s in range(nslab)],
                             axis=-1).astype(BF16)
        g = jnp.minimum(_dot(xb, wgb[...]) + bg_ref[0], SWIGLU_LIMIT)
        u = jnp.clip(_dot(xb, wub[...]) + bu_ref[0], -SWIGLU_LIMIT, SWIGLU_LIMIT)
        act = g * _sigmoid(SWIGLU_ALPHA * g) * (u + 1.0)
        y = _dot(act.astype(BF16), wdb[...]) + bd_ref[0]
        for s in range(nslab):
            ybuf[pl.ds(base + s, rows, stride=nslab), :] = y[:, s * LANES:(s + 1) * LANES]
        issue_scatter(slot)

    @pl.when(nv_ref[i] == 0)
    def _():
        ybuf[pl.ds(pl.multiple_of(slot * blk, blk), blk), :] = jnp.zeros((blk, LANES), F32)
        fill = pltpu.make_async_copy(ybuf.at[pl.ds(pl.multiple_of(slot * blk, blk), blk), :],
                                     out_hbm.at[pl.ds(pl.multiple_of(dst_ref[0, 0, 0], nslab), blk), :], ssem.at[slot])
        fill.start()
        fill.wait()

    @pl.when(i == n - 1)
    def _():
        @pl.when(jnp.logical_and(i >= 1, nv_ref[prev] > 0))
        def _():
            wait_scatter(1 - slot)

        @pl.when(nv_ref[i] > 0)
        def _():
            wait_scatter(slot)


def _moe_x_call(block_expert, n_valid, dst, x_sorted, n_out_rows, w_gate, b_gate, w_up, b_up, w_down, b_down, out_prev):
    e_, d, f = w_gate.shape
    nslab = d // LANES
    nblk = dst.shape[0]
    rows = MOE_BLOCK
    aliased = out_prev is not None

    def blk(i, be, nv):
        return (i, 0, 0)

    def expert(i, be, nv):
        return (be[i], 0, 0)

    in_specs = [
        pl.BlockSpec((1, 1, rows), blk, memory_space=pltpu.SMEM),
        pl.BlockSpec((rows * nslab, LANES), lambda i, be, nv: (i, 0)),
        pl.BlockSpec((1, d, f), expert), pl.BlockSpec((1, 1, f), expert),
        pl.BlockSpec((1, d, f), expert), pl.BlockSpec((1, 1, f), expert),
        pl.BlockSpec((1, f, d), expert), pl.BlockSpec((1, 1, d), expert),
    ]
    args = [block_expert, n_valid, dst, x_sorted,
            w_gate, b_gate.reshape(e_, 1, f), w_up, b_up.reshape(e_, 1, f), w_down, b_down.reshape(e_, 1, d)]
    if aliased:
        in_specs.append(pl.BlockSpec(memory_space=pl.ANY))
        args.append(out_prev)
    grid_spec = pltpu.PrefetchScalarGridSpec(
        num_scalar_prefetch=2,
        grid=(nblk,),
        in_specs=in_specs,
        out_specs=pl.BlockSpec(memory_space=pl.ANY),
        scratch_shapes=[
            pltpu.VMEM((2 * rows * nslab, LANES), F32),
            pltpu.VMEM((d, f), BF16), pltpu.VMEM((d, f), BF16), pltpu.VMEM((f, d), BF16),
            pltpu.SemaphoreType.DMA((2,)),
        ],
    )
    return pl.pallas_call(
        functools.partial(_moe_x_kernel, aliased),
        grid_spec=grid_spec,
        out_shape=jax.ShapeDtypeStruct((n_out_rows * nslab, LANES), F32),
        input_output_aliases={len(args) - 1: 0} if aliased else {},
        compiler_params=pltpu.CompilerParams(
            dimension_semantics=("arbitrary",), vmem_limit_bytes=VMEM_LIMIT),
    )(*args)


def _final_kernel(x1_ref, y0_ref, y1_ref, y2_ref, y3_ref, p_ref, gt2_ref, gpost_ref, o_ref):
    p = p_ref[0]
    tm, d = x1_ref.shape[1:]
    nslab = d // LANES
    parts = []
    for s in range(nslab):
        acc = p[:, 0:1] * y0_ref[pl.ds(s, tm, stride=nslab), :]
        for k_, y_ref in ((1, y1_ref), (2, y2_ref), (3, y3_ref)):
            acc = acc + p[:, k_:k_ + 1] * y_ref[pl.ds(s, tm, stride=nslab), :]
        parts.append(acc)
    ffn = jnp.concatenate(parts, axis=-1)
    o_ref[0] = x1_ref[0] + gt2_ref[0] * (_rms(ffn) * gpost_ref[...])


def _final_call(x1, y, probs, gt2, g_post):
    b_, l_, d = x1.shape
    tm = TOKEN_TILE
    nslab = d // LANES
    tiles_per_k = b_ * l_ // tm
    tile = lambda b, t: (b, t, 0)

    def y_spec(k_):
        return pl.BlockSpec((tm * nslab, LANES), lambda b, t: (k_ * tiles_per_k + b * (l_ // tm) + t, 0))

    return pl.pallas_call(
        _final_kernel,
        grid=(b_, l_ // tm),
        in_specs=[
            pl.BlockSpec((1, tm, d), tile),
            y_spec(0), y_spec(1), y_spec(2), y_spec(3),
            pl.BlockSpec((1, tm, LANES), tile),
            pl.BlockSpec((1, 1, d), lambda b, t: (b, 0, 0)),
            pl.BlockSpec(g_post.shape, lambda b, t: (0, 0)),
        ],
        out_specs=pl.BlockSpec((1, tm, d), tile),
        out_shape=jax.ShapeDtypeStruct((b_, l_, d), F32),
        compiler_params=pltpu.CompilerParams(
            dimension_semantics=("arbitrary", "arbitrary"), vmem_limit_bytes=VMEM_LIMIT),
    )(x1, y, y, y, y, probs, gt2, g_post)


def _pad_heads(w, n_heads, dk):
    d = w.shape[0]
    w = w.reshape(d, n_heads, dk)
    return jnp.pad(w, ((0, 0), (0, 0), (0, HEAD_W - dk))).reshape(d, n_heads * HEAD_W)


def kernel(x, c, ctx, c_ctx, w_ada, b_ada, g_pre_mix, g_post_mix, g_pre_ffn, g_post_ffn, w_in, hgrn_lb, hgrn_norm_w, gla_gk_w2, gla_gk_b, gla_norm_w, w_up_a, w_up_b, w_o, w_router, b_router, w_gate, b_gate, w_up, b_up, w_down, b_down):
    b_, l_, d = x.shape
    lc = ctx.shape[1]
    n_experts = w_router.shape[-1]
    layer = 0
    hw = HGRN_HEADS * HGRN_DK
    kw = GLA_HEADS * GLA_DK
    vw = GLA_HEADS * GLA_DV
    rk = GLA_GATE_RANK

    rows = -(-(b_ + 1) // 8) * 8
    cc = jnp.concatenate([c, c_ctx[None, :], jnp.zeros((rows - b_ - 1, d), F32)], axis=0)
    mod = _ada_call(cc, w_ada[layer], b_ada[layer])[:b_ + 1]
    sh1, sc1, gt1, sh2, sc2, gt2 = [m.reshape(b_ + 1, 1, d) for m in jnp.split(mod, 6, axis=-1)]
    a1 = g_pre_mix[layer] * (1.0 + sc1)
    a2 = g_pre_ffn[layer] * (1.0 + sc2)

    w = w_in[layer]
    o0 = 0
    cols = {}
    for name, size in (("qa", hw), ("zf", hw), ("zb", hw), ("ia", hw), ("oga", hw), ("qb", kw), ("kb", kw),
                       ("vb", vw), ("rf", rk), ("rb", rk), ("ogb", vw), ("mga", d), ("mgb", d)):
        cols[name] = w[:, o0:o0 + size]
        o0 += size
    w_scan = jnp.concatenate([
        cols["qa"], cols["zf"], cols["zb"], cols["ia"],
        _pad_heads(cols["qb"] * (GLA_DK ** -0.5), GLA_HEADS, GLA_DK),
        _pad_heads(cols["kb"], GLA_HEADS, GLA_DK), cols["vb"]], axis=1).astype(BF16)
    w_rank = jnp.pad(jnp.concatenate([cols["rf"], cols["rb"]], axis=1), ((0, 0), (0, LANES - 2 * rk))).astype(BF16)
    w2 = jnp.zeros((LANES, 2 * GLA_HEADS * HEAD_W), F32)
    w2 = w2.at[0:rk, 0:GLA_HEADS * HEAD_W].set(_pad_heads(gla_gk_w2[layer, 0], GLA_HEADS, GLA_DK))
    w2 = w2.at[rk:2 * rk, GLA_HEADS * HEAD_W:].set(_pad_heads(gla_gk_w2[layer, 1], GLA_HEADS, GLA_DK))
    w2 = w2.astype(BF16)
    gkb = jnp.concatenate([_pad_heads(gla_gk_b[layer, 0][None, :], GLA_HEADS, GLA_DK),
                           _pad_heads(gla_gk_b[layer, 1][None, :], GLA_HEADS, GLA_DK)], axis=1)
    lb = jnp.cumsum(jax.nn.softmax(hgrn_lb.astype(F32), axis=0), axis=0)[layer]
    w_gates = jnp.concatenate([cols["oga"], cols["ogb"], cols["mga"], cols["mgb"]], axis=1).astype(BF16)
    norm_w = jnp.concatenate([jnp.tile(hgrn_norm_w[layer], HGRN_HEADS), jnp.tile(gla_norm_w[layer], GLA_HEADS)])[None, :]
    w_r = jnp.stack(_split_bf16(jnp.pad(w_router[layer], ((0, 0), (0, LANES - n_experts)))))
    b_r = jnp.pad(b_router[layer], (0, LANES - n_experts), constant_values=-1e30)[None, :]

    q, v, akf, akb, bk, gf, gb = _inproj_call(x, ctx, a1, sh1, w_scan, w_rank, w2, gkb, lb)
    o_f, o_b = _scan_call(q, v, akf, akb, bk, gf, gb, lc)
    x1, h2, top_idx, probs = _merge_call(
        x, a1, sh1, gt1, a2, sh2, o_f, o_b, w_gates, norm_w,
        w_up_a[layer].astype(BF16), w_up_b[layer].astype(BF16), w_o[layer].astype(BF16),
        g_post_mix[layer][None, :], w_r, b_r)

    t_ = b_ * l_
    be, n_valid, tok, dst, n_out_rows = _dispatch_plan(top_idx.reshape(t_, LANES)[:, :TOP_K], n_experts, d // LANES)
    nslab = d // LANES
    nblk = be.shape[0]
    nbg = nblk // MOE_GROUPS
    row_idx = (tok.reshape(-1)[:, None] + jnp.arange(nslab, dtype=jnp.int32)[None, :]).reshape(MOE_GROUPS, -1)
    y = None
    for g in range(MOE_GROUPS):
        blocks = slice(g * nbg, (g + 1) * nbg)
        x_sorted = _sc_gather(h2, row_idx[g])
        y = _moe_x_call(be[blocks], n_valid[blocks], dst[blocks], x_sorted, n_out_rows,
                        w_gate[layer], b_gate[layer], w_up[layer], b_up[layer], w_down[layer], b_down[layer], y)
    return _final_call(x1, y, probs, gt2, g_post_ffn[layer][None, :])
```

```python
import functools

import jax
import jax.numpy as jnp
from jax import lax
from jax.experimental import pallas as pl
from jax.experimental.pallas import tpu as pltpu

F32 = jnp.float32
BF16 = jnp.bfloat16

NORM_EPS = 1e-6
HGRN_HEADS = 4
HGRN_DK = 128
GLA_HEADS = 4
GLA_DK = 64
GLA_DV = 128
GLA_GATE_RANK = 16
GLA_GATE_NORMALIZER = 16.0
N_HEADS = HGRN_HEADS + GLA_HEADS
HEAD_W = 128
TOP_K = 4
SWIGLU_LIMIT = 7.0
SWIGLU_ALPHA = 1.702
MOE_BLOCK = 256
SCAN_CHUNK = 64
SCAN_BLOCK = 256
LOG2E = 1.4426950408889634
TOKEN_TILE = 256
LATENT_TILE = 512
LANES = 128
VMEM_LIMIT = 56 * 1024 * 1024


def _dot(a, b):
    return jnp.dot(a, b, preferred_element_type=F32)


def _dot_nt(a, b):
    return lax.dot_general(a, b, (((1,), (1,)), ((), ())), preferred_element_type=F32)


def _dot_tn(a, b):
    return lax.dot_general(a, b, (((0,), (0,)), ((), ())), preferred_element_type=F32)


def _split_bf16(a):
    hi = a.astype(BF16)
    lo = (a - hi.astype(F32)).astype(BF16)
    return hi, lo


def _dot3(a, b):
    ah, al = _split_bf16(a)
    bh, bl = _split_bf16(b)
    return _dot(ah, bh) + _dot(al, bh) + _dot(ah, bl)


def _rms(x):
    return x * lax.rsqrt(jnp.mean(x * x, axis=-1, keepdims=True) + NORM_EPS)


def _sigmoid(x):
    return 1.0 / (1.0 + jnp.exp(-x))


def _ada_kernel(c_ref, w_ref, b_ref, o_ref):
    c = c_ref[...]
    o_ref[...] = _dot3(c * _sigmoid(c), w_ref[...]) + b_ref[...]


def _ada_call(cc, w_ada, b_ada):
    rows, d = cc.shape
    n = w_ada.shape[1]
    tn = 1536
    return pl.pallas_call(
        _ada_kernel,
        grid=(n // tn,),
        in_specs=[
            pl.BlockSpec((rows, d), lambda j: (0, 0)),
            pl.BlockSpec((d, tn), lambda j: (0, j)),
            pl.BlockSpec((1, tn), lambda j: (0, j)),
        ],
        out_specs=pl.BlockSpec((rows, tn), lambda j: (0, j)),
        out_shape=jax.ShapeDtypeStruct((rows, n), F32),
        compiler_params=pltpu.CompilerParams(
            dimension_semantics=("arbitrary",), vmem_limit_bytes=VMEM_LIMIT),
    )(cc, w_ada, b_ada.reshape(1, n))


def _inproj_kernel(n_ctx_tiles, x_ref, ctx_ref, a_ref, s_ref, w_ref, wr_ref, w2_ref, gkb_ref, lb_ref,
                   q_ref, v_ref, akf_ref, akb_ref, bk_ref, gf_ref, gb_ref):
    t = pl.program_id(1)
    xt = jnp.where(t < n_ctx_tiles, ctx_ref[0], x_ref[0])
    hb = (_rms(xt) * a_ref[0] + s_ref[0]).astype(BF16)

    def seg(lo, hi):
        return _dot(hb, w_ref[:, lo:hi])

    aw = HGRN_HEADS * HGRN_DK
    bw = GLA_HEADS * GLA_DK
    z_f, z_b = seg(aw, 2 * aw), seg(2 * aw, 3 * aw)
    r = _dot(hb, wr_ref[...]).astype(BF16)
    pre = _dot(r, w2_ref[...]) + gkb_ref[...]
    q_a, v_a = seg(0, aw), seg(3 * aw, 4 * aw)
    q_b, k_b = seg(4 * aw, 4 * aw + bw), seg(4 * aw + bw, 4 * aw + 2 * bw)
    v_b = seg(4 * aw + 2 * bw, 5 * aw + 2 * bw)
    for j, z, k_ref, g_ref in ((1, z_f, akf_ref, gf_ref), (2, z_b, akb_ref, gb_ref)):
        lb = lb_ref[j - 1:j, :]
        sg = _sigmoid(z)
        g_ref[0, :, 0:aw] = jnp.log2(lb + (1.0 - lb) * sg)
        k_ref[0] = ((1.0 - lb) * (1.0 - sg)).astype(BF16)
    ls = (jnp.minimum(pre, 0.0) - jnp.log(1.0 + jnp.exp(-jnp.abs(pre)))) * (LOG2E / GLA_GATE_NORMALIZER)
    gf_ref[0, :, aw:aw + bw] = ls[:, 0:bw]
    gb_ref[0, :, aw:aw + bw] = ls[:, bw:2 * bw]
    q_ref[0, :, 0:aw] = q_a.astype(BF16)
    q_ref[0, :, aw:aw + bw] = q_b.astype(BF16)
    v_ref[0, :, 0:aw] = v_a.astype(BF16)
    v_ref[0, :, aw:2 * aw] = v_b.astype(BF16)
    bk_ref[0] = k_b.astype(BF16)


def _inproj_call(x, ctx, mod_a, mod_s, w_scan, w_rank, w2, gkb, lb):
    b_, l_, d = x.shape
    lc = ctx.shape[1]
    tm = TOKEN_TILE
    nct, nlt = lc // tm, l_ // tm
    lt = lc + l_

    def x_map(b, t):
        return (b, jnp.maximum(t - nct, 0), 0)

    def ctx_map(b, t):
        return (b, jnp.minimum(t, nct - 1), 0)

    def mod_map(b, t):
        return (jnp.where(t < nct, b_, b), 0, 0)

    const2 = lambda b, t: (0, 0)
    out_map = lambda b, t: (b, t, 0)
    aw, bw, vw = HGRN_HEADS * HGRN_DK, GLA_HEADS * GLA_DK, GLA_HEADS * GLA_DV
    qk_w, v_w = aw + bw, aw + vw

    def out(width, dtype):
        return jax.ShapeDtypeStruct((b_, lt, width), dtype), pl.BlockSpec((1, tm, width), out_map)

    outs = [out(qk_w, BF16), out(v_w, BF16), out(aw, BF16), out(aw, BF16), out(bw, BF16), out(qk_w, F32), out(qk_w, F32)]
    return pl.pallas_call(
        functools.partial(_inproj_kernel, nct),
        grid=(b_, nct + nlt),
        in_specs=[
            pl.BlockSpec((1, tm, d), x_map),
            pl.BlockSpec((1, tm, d), ctx_map),
            pl.BlockSpec((1, 1, d), mod_map),
            pl.BlockSpec((1, 1, d), mod_map),
            pl.BlockSpec(w_scan.shape, const2),
            pl.BlockSpec(w_rank.shape, const2),
            pl.BlockSpec(w2.shape, const2),
            pl.BlockSpec(gkb.shape, const2),
            pl.BlockSpec(lb.shape, const2),
        ],
        out_specs=[o[1] for o in outs],
        out_shape=[o[0] for o in outs],
        compiler_params=pltpu.CompilerParams(
            dimension_semantics=("arbitrary", "arbitrary"), vmem_limit_bytes=VMEM_LIMIT),
    )(x, ctx, mod_a, mod_s, w_scan, w_rank, w2, gkb, lb)


def _causal(forward):
    row = lax.broadcasted_iota(jnp.int32, (SCAN_CHUNK, SCAN_CHUNK), 0)
    col = lax.broadcasted_iota(jnp.int32, (SCAN_CHUNK, SCAN_CHUNK), 1)
    return (row >= col) if forward else (col >= row)


def _scan_block(with_out, qf_ref, vf_ref, akf_ref, bkf_ref, gf_ref, qb_ref, vb_ref, akb_ref, bkb_ref, gb_ref,
                of_ref, ob_ref, sf_ref, sb_ref):
    c_ = SCAN_CHUNK
    ncb = qf_ref.shape[1] // c_
    aw = HGRN_HEADS * HGRN_DK
    n_pairs = GLA_HEADS // 2
    dirs = [(True, qf_ref, vf_ref, akf_ref, bkf_ref, gf_ref, of_ref, sf_ref),
            (False, qb_ref, vb_ref, akb_ref, bkb_ref, gb_ref, ob_ref, sb_ref)]
    lane = lax.broadcasted_iota(jnp.int32, (c_, HEAD_W), 1)
    low = lane < GLA_DK
    pr = lax.broadcasted_iota(jnp.int32, (c_, 2 * c_), 0)
    pc = lax.broadcasted_iota(jnp.int32, (c_, 2 * c_), 1) % c_
    pair_causal = {True: pr >= pc, False: pc >= pr}
    srow = lax.broadcasted_iota(jnp.int32, (2 * HEAD_W, HEAD_W), 0) < HEAD_W
    scol = lax.broadcasted_iota(jnp.int32, (2 * HEAD_W, HEAD_W), 1) < GLA_DK
    pair_state = srow == scol
    zero_v = jnp.zeros((c_, HEAD_W), BF16)
    cums = {}
    for di, (fw, _, _, _, _, g_ref, _, _) in enumerate(dirs):
        tri = jnp.where(_causal(fw), 1.0, 0.0).astype(BF16)
        for c in range(ncb):
            g_hi, g_lo = _split_bf16(g_ref[0, c * c_:(c + 1) * c_, :])
            cums[di, c] = _dot(tri, g_hi) + _dot(tri, g_lo)
    ops = []
    for di, (fw, q_ref, v_ref, ak_ref, bk_ref, _, _, _) in enumerate(dirs):
        end_row, mid_row = (c_ - 1, c_ // 2 - 1) if fw else (0, c_ // 2)
        for c in range(ncb):
            rows = slice(c * c_, (c + 1) * c_)
            cum = cums[di, c]
            total = cum[end_row:end_row + 1, :]
            mid = cum[mid_row:mid_row + 1, :]
            e_mid = jnp.exp2(mid)
            e_rest = jnp.exp2(total - mid)
            e_tot = jnp.exp2(total)
            for u in range(HGRN_HEADS + n_pairs):
                pair = u >= HGRN_HEADS
                sl = slice(u * HEAD_W, (u + 1) * HEAD_W)
                k_src = bk_ref[0, rows, (u - HGRN_HEADS) * HEAD_W:(u - HGRN_HEADS + 1) * HEAD_W] if pair \
                    else ak_ref[0, rows, sl]
                qs = q_ref[0, rows, sl].astype(F32) * jnp.exp2(cum[:, sl] - mid[:, sl])
                ks = k_src.astype(F32) * jnp.exp2(mid[:, sl] - cum[:, sl])
                ksb = ks.astype(BF16)
                if pair:
                    vl = slice(aw + (u - HGRN_HEADS) * 2 * HEAD_W, aw + (u - HGRN_HEADS + 1) * 2 * HEAD_W)
                    v = v_ref[0, rows, vl]
                    k_rhs = jnp.concatenate([jnp.where(low, ksb, 0), jnp.where(low, 0, ksb)], axis=0)
                    v_rhs = jnp.concatenate([jnp.concatenate([v[:, :HEAD_W], zero_v], axis=1),
                                             jnp.concatenate([zero_v, v[:, HEAD_W:]], axis=1)], axis=0)
                    st_rows = slice(aw + (u - HGRN_HEADS) * 2 * HEAD_W, aw + (u - HGRN_HEADS + 1) * 2 * HEAD_W)
                else:
                    vl = sl
                    v = v_ref[0, rows, vl]
                    k_rhs, v_rhs = ksb, v
                    st_rows = sl
                ops.append(dict(di=di, fw=fw, c=c, pair=pair, rows=rows, vl=vl, st_rows=st_rows, v=v,
                                k_rhs=k_rhs, v_rhs=v_rhs, qs=qs.astype(BF16), qd=(qs * e_mid[:, sl]).astype(BF16),
                                kd=(ks * e_rest[:, sl]).astype(BF16), e_tot=e_tot[:, sl]))
    if with_out:
        for d in ops:
            d["sc"] = _dot_nt(d["qs"], d["k_rhs"])
        for d in ops:
            mask = pair_causal[d["fw"]] if d["pair"] else _causal(d["fw"])
            d["sc"] = jnp.where(mask, d["sc"], 0.0).astype(BF16)
        for d in ops:
            d["o"] = _dot(d["sc"], d["v_rhs"])
    for d in ops:
        d["upd"] = _dot_tn(d["v"], d["kd"])
    for d in ops:
        if d["pair"]:
            d["upd"] = jnp.where(pair_state, d["upd"], 0.0)
    for di, (fw, _, _, _, _, _, o_ref, st_ref) in enumerate(dirs):
        sts = {}
        for c in (range(ncb) if fw else range(ncb - 1, -1, -1)):
            for d in ops:
                if d["di"] != di or d["c"] != c:
                    continue
                key = d["st_rows"].start
                if key not in sts:
                    sts[key] = st_ref[d["st_rows"], :]
                if with_out:
                    o = d["o"] + _dot_nt(d["qd"], sts[key].astype(BF16))
                    o_ref[0, d["rows"], d["vl"]] = o.astype(o_ref.dtype)
                sts[key] = sts[key] * d["e_tot"] + d["upd"]
        for d in ops:
            if d["di"] == di and d["c"] == 0:
                st_ref[d["st_rows"], :] = sts[d["st_rows"].start]


def _scan_kernel(n_ctx_blocks, *refs):
    s = pl.program_id(1)

    @pl.when(s == 0)
    def _():
        refs[-2][...] = jnp.zeros_like(refs[-2])
        refs[-1][...] = jnp.zeros_like(refs[-1])

    @pl.when(s < n_ctx_blocks)
    def _():
        _scan_block(False, *refs)

    @pl.when(s >= n_ctx_blocks)
    def _():
        _scan_block(True, *refs)


def _scan_call(q, v, akf, akb, bk, gf, gb, lc):
    b_, lt, _ = q.shape
    c_ = SCAN_BLOCK
    ncc = lc // c_
    nlc = (lt - lc) // c_
    n = ncc + nlc

    def fwd(b, s):
        return (b, s, 0)

    def bwd(b, s):
        return (b, jnp.where(s < ncc, ncc - 1 - s, n + ncc - 1 - s), 0)

    def out_fwd(b, s):
        return (b, jnp.maximum(s - ncc, 0), 0)

    def out_bwd(b, s):
        return (b, jnp.where(s < ncc, nlc - 1, n - 1 - s), 0)

    def specs(index_map):
        return [pl.BlockSpec((1, c_, a.shape[-1]), index_map) for a in (q, v, akf, bk, gf)]

    v_w = v.shape[-1]
    out = jax.ShapeDtypeStruct((b_, lt - lc, v_w), BF16)
    return pl.pallas_call(
        functools.partial(_scan_kernel, ncc),
        grid=(b_, n),
        in_specs=specs(fwd) + specs(bwd),
        out_specs=[pl.BlockSpec((1, c_, v_w), out_fwd), pl.BlockSpec((1, c_, v_w), out_bwd)],
        out_shape=[out, out],
        scratch_shapes=[pltpu.VMEM((v_w, HEAD_W), F32), pltpu.VMEM((v_w, HEAD_W), F32)],
        compiler_params=pltpu.CompilerParams(
            dimension_semantics=("arbitrary", "arbitrary"), vmem_limit_bytes=VMEM_LIMIT),
    )(q, v, akf, bk, gf, q, v, akb, bk, gb)


def _merge_kernel(x_ref, a1_ref, s1_ref, gt1_ref, a2_ref, s2_ref, of_ref, ob_ref, wg_ref, nw_ref,
                  wua_ref, wub_ref, wo_ref, gpost_ref, wr_ref, br_ref,
                  x1_ref, h2_ref, idx_ref, prob_ref):
    x = x_ref[0]
    hb = (_rms(x) * a1_ref[0] + s1_ref[0]).astype(BF16)
    og = _dot(hb, wg_ref[:, 0:1024])
    mg_a = _dot(hb, wg_ref[:, 1024:2048])
    mg_b = _dot(hb, wg_ref[:, 2048:3072])
    o = of_ref[0].astype(F32) + ob_ref[0].astype(F32)
    heads = []
    for h in range(N_HEADS):
        sl = slice(h * HEAD_W, (h + 1) * HEAD_W)
        heads.append(_rms(o[:, sl]))
    og = og * _sigmoid(og)
    r = (jnp.concatenate(heads, axis=-1) * nw_ref[...] * og).astype(BF16)
    y_a = _dot(r[:, 0:512], wua_ref[...])
    y_b = _dot(r[:, 512:1024], wub_ref[...])
    mix = _dot((_sigmoid(mg_a) * y_a + _sigmoid(mg_b) * y_b).astype(BF16), wo_ref[...])
    x1 = x + gt1_ref[0] * (_rms(mix) * gpost_ref[...])
    x1_ref[0] = x1
    h2 = _rms(x1) * a2_ref[0] + s2_ref[0]
    nslab = h2.shape[-1] // LANES
    for s in range(nslab):
        h2_ref[pl.ds(s, h2.shape[0], stride=nslab), :] = h2[:, s * LANES:(s + 1) * LANES]
    h_hi, h_lo = _split_bf16(h2)
    logits = _dot(h_hi, wr_ref[0]) + _dot(h_lo, wr_ref[0]) + _dot(h_hi, wr_ref[1]) + br_ref[...]
    lane = lax.broadcasted_iota(jnp.int32, logits.shape, 1).astype(F32)
    vals, idxs = [], []
    for _ in range(TOP_K):
        m = jnp.max(logits, axis=-1, keepdims=True)
        sel = jnp.min(jnp.where(logits == m, lane, float(LANES)), axis=-1, keepdims=True)
        vals.append(m)
        idxs.append(sel)
        logits = jnp.where(lane == sel, -jnp.inf, logits)
    exps = [jnp.exp(v_ - vals[0]) for v_ in vals]
    denom = exps[0] + exps[1] + exps[2] + exps[3]
    idx_out = jnp.zeros_like(lane)
    prob_out = jnp.zeros_like(lane)
    for k_ in range(TOP_K):
        idx_out = jnp.where(lane == float(k_), idxs[k_], idx_out)
        prob_out = jnp.where(lane == float(k_), exps[k_] / denom, prob_out)
    idx_ref[0] = idx_out.astype(jnp.int32)
    prob_ref[0] = prob_out


def _merge_call(x, a1, s1, gt1, a2, s2, o_f, o_b, w_gates, norm_w, w_up_a, w_up_b, w_o, g_post, w_r, b_r):
    b_, l_, d = x.shape
    tm = LATENT_TILE
    tile = lambda b, t: (b, t, 0)
    per_b = lambda b, t: (b, 0, 0)
    const2 = lambda b, t: (0, 0)
    return pl.pallas_call(
        _merge_kernel,
        grid=(b_, l_ // tm),
        in_specs=[
            pl.BlockSpec((1, tm, d), tile),
            pl.BlockSpec((1, 1, d), per_b), pl.BlockSpec((1, 1, d), per_b), pl.BlockSpec((1, 1, d), per_b),
            pl.BlockSpec((1, 1, d), per_b), pl.BlockSpec((1, 1, d), per_b),
            pl.BlockSpec((1, tm, 1024), tile), pl.BlockSpec((1, tm, 1024), tile),
            pl.BlockSpec(w_gates.shape, const2),
            pl.BlockSpec(norm_w.shape, const2),
            pl.BlockSpec(w_up_a.shape, const2),
            pl.BlockSpec(w_up_b.shape, const2),
            pl.BlockSpec(w_o.shape, const2),
            pl.BlockSpec(g_post.shape, const2),
            pl.BlockSpec(w_r.shape, lambda b, t: (0, 0, 0)),
            pl.BlockSpec(b_r.shape, const2),
        ],
        out_specs=[
            pl.BlockSpec((1, tm, d), tile),
            pl.BlockSpec((tm * (d // LANES), LANES), lambda b, t: (b * (l_ // tm) + t, 0)),
            pl.BlockSpec((1, tm, LANES), tile), pl.BlockSpec((1, tm, LANES), tile),
        ],
        out_shape=[
            jax.ShapeDtypeStruct((b_, l_, d), F32), jax.ShapeDtypeStruct((b_ * l_ * (d // LANES), LANES), F32),
            jax.ShapeDtypeStruct((b_, l_, LANES), jnp.int32), jax.ShapeDtypeStruct((b_, l_, LANES), F32),
        ],
        compiler_params=pltpu.CompilerParams(
            dimension_semantics=("arbitrary", "arbitrary"), vmem_limit_bytes=VMEM_LIMIT),
    )(x, a1, s1, gt1, a2, s2, o_f, o_b, w_gates, norm_w, w_up_a, w_up_b, w_o, g_post, w_r, b_r)


def _moe_kernel(be_ref, nv_ref, tok_ref, tokn_ref, dst_ref, h2_hbm,
                wg_ref, bg_ref, wu_ref, bu_ref, wd_ref, bd_ref, out_hbm,
                xbuf, ybuf, wgb, wub, wdb, gsem, ssem):
    i = pl.program_id(0)
    n = pl.num_programs(0)
    slot = i % 2
    rows = MOE_BLOCK
    nslab = xbuf.shape[0] // (2 * rows)
    blk = rows * nslab

    def slab(ref, start):
        return ref.at[pl.ds(pl.multiple_of(start, nslab), nslab), :]

    group = 8

    def issue_gather(idx_ref, s):
        def body(gi, carry):
            for j in range(group):
                r = gi * group + j
                pltpu.make_async_copy(slab(h2_hbm, idx_ref[0, 0, r]), slab(xbuf, s * blk + r * nslab),
                                      gsem.at[s]).start(priority=j % 2)
            return carry
        lax.fori_loop(0, rows // group, body, 0)

    def wait_gather(s):
        pltpu.make_async_copy(h2_hbm.at[pl.ds(0, blk), :], xbuf.at[pl.ds(pl.multiple_of(s * blk, blk), blk), :],
                              gsem.at[s]).wait()

    def issue_scatter(s):
        def body(gi, carry):
            for j in range(group):
                r = gi * group + j
                pltpu.make_async_copy(slab(ybuf, s * blk + r * nslab), slab(out_hbm, dst_ref[0, 0, r]),
                                      ssem.at[s]).start(priority=j % 2)
            return carry
        lax.fori_loop(0, rows // group, body, 0)

    def wait_scatter(s):
        pltpu.make_async_copy(ybuf.at[pl.ds(pl.multiple_of(s * blk, blk), blk), :], out_hbm.at[pl.ds(0, blk), :],
                              ssem.at[s]).wait()

    @pl.when(jnp.logical_and(i == 0, nv_ref[0] > 0))
    def _():
        issue_gather(tok_ref, 0)

    nxt = jnp.minimum(i + 1, n - 1)

    @pl.when(jnp.logical_and(i + 1 < n, nv_ref[nxt] > 0))
    def _():
        issue_gather(tokn_ref, 1 - slot)

    prev = jnp.maximum(i - 1, 0)

    @pl.when(jnp.logical_or(i == 0, be_ref[i] != be_ref[prev]))
    def _():
        wgb[...] = wg_ref[0].astype(BF16)
        wub[...] = wu_ref[0].astype(BF16)
        wdb[...] = wd_ref[0].astype(BF16)

    @pl.when(jnp.logical_and(i >= 2, nv_ref[jnp.maximum(i - 2, 0)] > 0))
    def _():
        wait_scatter(slot)

    @pl.when(nv_ref[i] > 0)
    def _():
        wait_gather(slot)
        base = slot * blk
        xb = jnp.concatenate([xbuf[pl.ds(base + s, rows, stride=nslab), :] for s in range(nslab)],
                             axis=-1).astype(BF16)
        g = jnp.minimum(_dot(xb, wgb[...]) + bg_ref[0], SWIGLU_LIMIT)
        u = jnp.clip(_dot(xb, wub[...]) + bu_ref[0], -SWIGLU_LIMIT, SWIGLU_LIMIT)
        act = g * _sigmoid(SWIGLU_ALPHA * g) * (u + 1.0)
        y = _dot(act.astype(BF16), wdb[...]) + bd_ref[0]
        for s in range(nslab):
            ybuf[pl.ds(base + s, rows, stride=nslab), :] = y[:, s * LANES:(s + 1) * LANES]
        issue_scatter(slot)

    @pl.when(nv_ref[i] == 0)
    def _():
        ybuf[pl.ds(pl.multiple_of(slot * blk, blk), blk), :] = jnp.zeros((blk, LANES), F32)
        fill = pltpu.make_async_copy(ybuf.at[pl.ds(pl.multiple_of(slot * blk, blk), blk), :],
                                     out_hbm.at[pl.ds(pl.multiple_of(dst_ref[0, 0, 0], nslab), blk), :], ssem.at[slot])
        fill.start()
        fill.wait()

    @pl.when(i == n - 1)
    def _():
        @pl.when(jnp.logical_and(i >= 1, nv_ref[prev] > 0))
        def _():
            wait_scatter(1 - slot)

        @pl.when(nv_ref[i] > 0)
        def _():
            wait_scatter(slot)


def _moe_call(block_expert, n_valid, tok, dst, h2, n_out_rows, w_gate, b_gate, w_up, b_up, w_down, b_down):
    e_, d, f = w_gate.shape
    nslab = d // LANES
    nblk = tok.shape[0]
    rows = MOE_BLOCK

    def blk(i, be, nv):
        return (i, 0, 0)

    def blk_next(i, be, nv):
        return (jnp.minimum(i + 1, nblk - 1), 0, 0)

    def expert(i, be, nv):
        return (be[i], 0, 0)

    smem_rows = functools.partial(pl.BlockSpec, (1, 1, rows), memory_space=pltpu.SMEM)
    grid_spec = pltpu.PrefetchScalarGridSpec(
        num_scalar_prefetch=2,
        grid=(nblk,),
        in_specs=[
            smem_rows(blk), smem_rows(blk_next), smem_rows(blk),
            pl.BlockSpec(memory_space=pl.ANY),
            pl.BlockSpec((1, d, f), expert), pl.BlockSpec((1, 1, f), expert),
            pl.BlockSpec((1, d, f), expert), pl.BlockSpec((1, 1, f), expert),
            pl.BlockSpec((1, f, d), expert), pl.BlockSpec((1, 1, d), expert),
        ],
        out_specs=pl.BlockSpec(memory_space=pl.ANY),
        scratch_shapes=[
            pltpu.VMEM((2 * rows * nslab, LANES), F32), pltpu.VMEM((2 * rows * nslab, LANES), F32),
            pltpu.VMEM((d, f), BF16), pltpu.VMEM((d, f), BF16), pltpu.VMEM((f, d), BF16),
            pltpu.SemaphoreType.DMA((2,)), pltpu.SemaphoreType.DMA((2,)),
        ],
    )
    return pl.pallas_call(
        _moe_kernel,
        grid_spec=grid_spec,
        out_shape=jax.ShapeDtypeStruct((n_out_rows * nslab, LANES), F32),
        compiler_params=pltpu.CompilerParams(
            dimension_semantics=("arbitrary",), vmem_limit_bytes=VMEM_LIMIT),
    )(block_expert, n_valid, tok, tok, dst, h2,
      w_gate, b_gate.reshape(e_, 1, f), w_up, b_up.reshape(e_, 1, f), w_down, b_down.reshape(e_, 1, d))


def _dispatch_plan(top_idx, n_experts, nslab):
    t_ = top_idx.shape[0]
    n_pairs = t_ * TOP_K
    flat_e = top_idx.reshape(-1)
    order = jnp.argsort(flat_e).astype(jnp.int32)
    counts = jnp.sum((flat_e[:, None] == jnp.arange(n_experts, dtype=jnp.int32)[None, :]).astype(jnp.int32), axis=0)
    nblk_e = (counts + MOE_BLOCK - 1) // MOE_BLOCK
    blk_end = jnp.cumsum(nblk_e)
    blk_start = blk_end - nblk_e
    start = jnp.cumsum(counts) - counts
    nblk = n_pairs // MOE_BLOCK + n_experts
    b = jnp.arange(nblk, dtype=jnp.int32)
    be = jnp.minimum(jnp.sum((b[:, None] >= blk_end[None, :]).astype(jnp.int32), axis=1), n_experts - 1)
    first_row = (b - blk_start[be]) * MOE_BLOCK
    n_valid = jnp.clip(counts[be] - first_row, 0, MOE_BLOCK).astype(jnp.int32)
    r = jnp.arange(MOE_BLOCK, dtype=jnp.int32)[None, :]
    valid = r < n_valid[:, None]
    src = jnp.clip(start[be][:, None] + first_row[:, None] + r, 0, n_pairs - 1)
    pair = jnp.where(valid, order[src], 0).astype(jnp.int32)
    tok = pair // TOP_K
    spare = (jnp.cumsum((~valid).reshape(-1).astype(jnp.int32)) - 1).reshape(nblk, MOE_BLOCK)
    dst = jnp.where(valid, (pair % TOP_K) * t_ + tok, n_pairs + spare).astype(jnp.int32)
    n_out_rows = n_pairs + n_experts * MOE_BLOCK
    return (be, n_valid, (tok * nslab).reshape(nblk, 1, MOE_BLOCK), (dst * nslab).reshape(nblk, 1, MOE_BLOCK),
            n_out_rows)


def _final_kernel(x1_ref, y0_ref, y1_ref, y2_ref, y3_ref, p_ref, gt2_ref, gpost_ref, o_ref):
    p = p_ref[0]
    tm, d = x1_ref.shape[1:]
    nslab = d // LANES
    parts = []
    for s in range(nslab):
        acc = p[:, 0:1] * y0_ref[pl.ds(s, tm, stride=nslab), :]
        for k_, y_ref in ((1, y1_ref), (2, y2_ref), (3, y3_ref)):
            acc = acc + p[:, k_:k_ + 1] * y_ref[pl.ds(s, tm, stride=nslab), :]
        parts.append(acc)
    ffn = jnp.concatenate(parts, axis=-1)
    o_ref[0] = x1_ref[0] + gt2_ref[0] * (_rms(ffn) * gpost_ref[...])


def _final_call(x1, y, probs, gt2, g_post):
    b_, l_, d = x1.shape
    tm = LATENT_TILE
    nslab = d // LANES
    tiles_per_k = b_ * l_ // tm
    tile = lambda b, t: (b, t, 0)

    def y_spec(k_):
        return pl.BlockSpec((tm * nslab, LANES), lambda b, t: (k_ * tiles_per_k + b * (l_ // tm) + t, 0))

    return pl.pallas_call(
        _final_kernel,
        grid=(b_, l_ // tm),
        in_specs=[
            pl.BlockSpec((1, tm, d), tile),
            y_spec(0), y_spec(1), y_spec(2), y_spec(3),
            pl.BlockSpec((1, tm, LANES), tile),
            pl.BlockSpec((1, 1, d), lambda b, t: (b, 0, 0)),
            pl.BlockSpec(g_post.shape, lambda b, t: (0, 0)),
        ],
        out_specs=pl.BlockSpec((1, tm, d), tile),
        out_shape=jax.ShapeDtypeStruct((b_, l_, d), F32),
        compiler_params=pltpu.CompilerParams(
            dimension_semantics=("arbitrary", "arbitrary"), vmem_limit_bytes=VMEM_LIMIT),
    )(x1, y, y, y, y, probs, gt2, g_post)


def kernel(x, c, ctx, c_ctx, w_ada, b_ada, g_pre_mix, g_post_mix, g_pre_ffn, g_post_ffn, w_in, hgrn_lb, hgrn_norm_w, gla_gk_w2, gla_gk_b, gla_norm_w, w_up_a, w_up_b, w_o, w_router, b_router, w_gate, b_gate, w_up, b_up, w_down, b_down):
    b_, l_, d = x.shape
    lc = ctx.shape[1]
    n_experts = w_router.shape[-1]
    layer = 0
    hw = HGRN_HEADS * HGRN_DK
    kw = GLA_HEADS * GLA_DK
    vw = GLA_HEADS * GLA_DV
    rk = GLA_GATE_RANK

    rows = -(-(b_ + 1) // 8) * 8
    cc = jnp.concatenate([c, c_ctx[None, :], jnp.zeros((rows - b_ - 1, d), F32)], axis=0)
    mod = _ada_call(cc, w_ada[layer], b_ada[layer])[:b_ + 1]
    sh1, sc1, gt1, sh2, sc2, gt2 = [m.reshape(b_ + 1, 1, d) for m in jnp.split(mod, 6, axis=-1)]
    a1 = g_pre_mix[layer] * (1.0 + sc1)
    a2 = g_pre_ffn[layer] * (1.0 + sc2)

    w = w_in[layer]
    o0 = 0
    cols = {}
    for name, size in (("qa", hw), ("zf", hw), ("zb", hw), ("ia", hw), ("oga", hw), ("qb", kw), ("kb", kw),
                       ("vb", vw), ("rf", rk), ("rb", rk), ("ogb", vw), ("mga", d), ("mgb", d)):
        cols[name] = w[:, o0:o0 + size]
        o0 += size
    w_scan = jnp.concatenate([
        cols["qa"], cols["zf"], cols["zb"], cols["ia"],
        cols["qb"] * (GLA_DK ** -0.5), cols["kb"], cols["vb"]], axis=1).astype(BF16)
    w_rank = jnp.pad(jnp.concatenate([cols["rf"], cols["rb"]], axis=1), ((0, 0), (0, LANES - 2 * rk))).astype(BF16)
    w2 = jnp.zeros((LANES, 2 * kw), F32)
    w2 = w2.at[0:rk, 0:kw].set(gla_gk_w2[layer, 0])
    w2 = w2.at[rk:2 * rk, kw:].set(gla_gk_w2[layer, 1])
    w2 = w2.astype(BF16)
    gkb = jnp.concatenate([gla_gk_b[layer, 0], gla_gk_b[layer, 1]])[None, :]
    lb = jnp.cumsum(jax.nn.softmax(hgrn_lb.astype(F32), axis=0), axis=0)[layer]
    w_gates = jnp.concatenate([cols["oga"], cols["ogb"], cols["mga"], cols["mgb"]], axis=1).astype(BF16)
    norm_w = jnp.concatenate([jnp.tile(hgrn_norm_w[layer], HGRN_HEADS), jnp.tile(gla_norm_w[layer], GLA_HEADS)])[None, :]
    w_r = jnp.stack(_split_bf16(jnp.pad(w_router[layer], ((0, 0), (0, LANES - n_experts)))))
    b_r = jnp.pad(b_router[layer], (0, LANES - n_experts), constant_values=-1e30)[None, :]

    q, v, akf, akb, bk, gf, gb = _inproj_call(x, ctx, a1, sh1, w_scan, w_rank, w2, gkb, lb)
    o_f, o_b = _scan_call(q, v, akf, akb, bk, gf, gb, lc)
    x1, h2, top_idx, probs = _merge_call(
        x, a1, sh1, gt1, a2, sh2, o_f, o_b, w_gates, norm_w,
        w_up_a[layer].astype(BF16), w_up_b[layer].astype(BF16), w_o[layer].astype(BF16),
        g_post_mix[layer][None, :], w_r, b_r)

    t_ = b_ * l_
    be, n_valid, tok, dst, n_out_rows = _dispatch_plan(top_idx.reshape(t_, LANES)[:, :TOP_K], n_experts, d // LANES)
    y = _moe_call(be, n_valid, tok, dst, h2, n_out_rows,
                  w_gate[layer], b_gate[layer], w_up[layer], b_up[layer], w_down[layer], b_down[layer])
    return _final_call(x1, y, probs, gt2, g_post_ffn[layer][None, :])
```

```python
import functools

import jax
import jax.numpy as jnp
from jax import lax
from jax.experimental import pallas as pl
from jax.experimental.pallas import tpu as pltpu

F32 = jnp.float32
BF16 = jnp.bfloat16

NORM_EPS = 1e-6
HGRN_HEADS = 4
HGRN_DK = 128
GLA_HEADS = 4
GLA_DK = 64
GLA_DV = 128
GLA_GATE_RANK = 16
GLA_GATE_NORMALIZER = 16.0
N_HEADS = HGRN_HEADS + GLA_HEADS
HEAD_W = 128
TOP_K = 4
SWIGLU_LIMIT = 7.0
SWIGLU_ALPHA = 1.702
MOE_BLOCK = 256
SCAN_CHUNK = 64
SCAN_BLOCK = 256
LOG2E = 1.4426950408889634
TOKEN_TILE = 256
LATENT_TILE = 512
LANES = 128
VMEM_LIMIT = 56 * 1024 * 1024


def _dot(a, b):
    return jnp.dot(a, b, preferred_element_type=F32)


def _dot_nt(a, b):
    return lax.dot_general(a, b, (((1,), (1,)), ((), ())), preferred_element_type=F32)


def _dot_tn(a, b):
    return lax.dot_general(a, b, (((0,), (0,)), ((), ())), preferred_element_type=F32)


def _split_bf16(a):
    hi = a.astype(BF16)
    lo = (a - hi.astype(F32)).astype(BF16)
    return hi, lo


def _dot3(a, b):
    ah, al = _split_bf16(a)
    bh, bl = _split_bf16(b)
    return _dot(ah, bh) + _dot(al, bh) + _dot(ah, bl)


def _rms(x):
    return x * lax.rsqrt(jnp.mean(x * x, axis=-1, keepdims=True) + NORM_EPS)


def _sigmoid(x):
    return 1.0 / (1.0 + jnp.exp(-x))


def _ada_kernel(c_ref, w_ref, b_ref, o_ref):
    c = c_ref[...]
    o_ref[...] = _dot3(c * _sigmoid(c), w_ref[...]) + b_ref[...]


def _ada_call(cc, w_ada, b_ada):
    rows, d = cc.shape
    n = w_ada.shape[1]
    tn = 1536
    return pl.pallas_call(
        _ada_kernel,
        grid=(n // tn,),
        in_specs=[
            pl.BlockSpec((rows, d), lambda j: (0, 0)),
            pl.BlockSpec((d, tn), lambda j: (0, j)),
            pl.BlockSpec((1, tn), lambda j: (0, j)),
        ],
        out_specs=pl.BlockSpec((rows, tn), lambda j: (0, j)),
        out_shape=jax.ShapeDtypeStruct((rows, n), F32),
        compiler_params=pltpu.CompilerParams(
            dimension_semantics=("arbitrary",), vmem_limit_bytes=VMEM_LIMIT),
    )(cc, w_ada, b_ada.reshape(1, n))


def _inproj_kernel(n_ctx_tiles, x_ref, ctx_ref, a_ref, s_ref, w_ref, wr_ref, w2_ref, gkb_ref, lb_ref,
                   q_ref, v_ref, akf_ref, akb_ref, bk_ref, gf_ref, gb_ref):
    t = pl.program_id(1)
    xt = jnp.where(t < n_ctx_tiles, ctx_ref[0], x_ref[0])
    hb = (_rms(xt) * a_ref[0] + s_ref[0]).astype(BF16)

    def seg(lo, hi):
        return _dot(hb, w_ref[:, lo:hi])

    aw = HGRN_HEADS * HGRN_DK
    bw = GLA_HEADS * GLA_DK
    z_f, z_b = seg(aw, 2 * aw), seg(2 * aw, 3 * aw)
    r = _dot(hb, wr_ref[...]).astype(BF16)
    pre = _dot(r, w2_ref[...]) + gkb_ref[...]
    q_a, v_a = seg(0, aw), seg(3 * aw, 4 * aw)
    q_b, k_b = seg(4 * aw, 4 * aw + bw), seg(4 * aw + bw, 4 * aw + 2 * bw)
    v_b = seg(4 * aw + 2 * bw, 5 * aw + 2 * bw)
    for j, z, k_ref, g_ref in ((1, z_f, akf_ref, gf_ref), (2, z_b, akb_ref, gb_ref)):
        lb = lb_ref[j - 1:j, :]
        sg = _sigmoid(z)
        g_ref[0, :, 0:aw] = jnp.log2(lb + (1.0 - lb) * sg)
        k_ref[0] = ((1.0 - lb) * (1.0 - sg)).astype(BF16)
    ls = (jnp.minimum(pre, 0.0) - jnp.log(1.0 + jnp.exp(-jnp.abs(pre)))) * (LOG2E / GLA_GATE_NORMALIZER)
    gf_ref[0, :, aw:aw + bw] = ls[:, 0:bw]
    gb_ref[0, :, aw:aw + bw] = ls[:, bw:2 * bw]
    q_ref[0, :, 0:aw] = q_a.astype(BF16)
    q_ref[0, :, aw:aw + bw] = q_b.astype(BF16)
    v_ref[0, :, 0:aw] = v_a.astype(BF16)
    v_ref[0, :, aw:2 * aw] = v_b.astype(BF16)
    bk_ref[0] = k_b.astype(BF16)


def _inproj_call(x, ctx, mod_a, mod_s, w_scan, w_rank, w2, gkb, lb):
    b_, l_, d = x.shape
    lc = ctx.shape[1]
    tm = TOKEN_TILE
    nct, nlt = lc // tm, l_ // tm
    lt = lc + l_

    def x_map(b, t):
        return (b, jnp.maximum(t - nct, 0), 0)

    def ctx_map(b, t):
        return (b, jnp.minimum(t, nct - 1), 0)

    def mod_map(b, t):
        return (jnp.where(t < nct, b_, b), 0, 0)

    const2 = lambda b, t: (0, 0)
    out_map = lambda b, t: (b, t, 0)
    aw, bw, vw = HGRN_HEADS * HGRN_DK, GLA_HEADS * GLA_DK, GLA_HEADS * GLA_DV
    qk_w, v_w = aw + bw, aw + vw

    def out(width, dtype):
        return jax.ShapeDtypeStruct((b_, lt, width), dtype), pl.BlockSpec((1, tm, width), out_map)

    outs = [out(qk_w, BF16), out(v_w, BF16), out(aw, BF16), out(aw, BF16), out(bw, BF16), out(qk_w, F32), out(qk_w, F32)]
    return pl.pallas_call(
        functools.partial(_inproj_kernel, nct),
        grid=(b_, nct + nlt),
        in_specs=[
            pl.BlockSpec((1, tm, d), x_map),
            pl.BlockSpec((1, tm, d), ctx_map),
            pl.BlockSpec((1, 1, d), mod_map),
            pl.BlockSpec((1, 1, d), mod_map),
            pl.BlockSpec(w_scan.shape, const2),
            pl.BlockSpec(w_rank.shape, const2),
            pl.BlockSpec(w2.shape, const2),
            pl.BlockSpec(gkb.shape, const2),
            pl.BlockSpec(lb.shape, const2),
        ],
        out_specs=[o[1] for o in outs],
        out_shape=[o[0] for o in outs],
        compiler_params=pltpu.CompilerParams(
            dimension_semantics=("arbitrary", "arbitrary"), vmem_limit_bytes=VMEM_LIMIT),
    )(x, ctx, mod_a, mod_s, w_scan, w_rank, w2, gkb, lb)


def _causal(forward):
    row = lax.broadcasted_iota(jnp.int32, (SCAN_CHUNK, SCAN_CHUNK), 0)
    col = lax.broadcasted_iota(jnp.int32, (SCAN_CHUNK, SCAN_CHUNK), 1)
    return (row >= col) if forward else (col >= row)


def _scan_block(with_out, qf_ref, vf_ref, akf_ref, bkf_ref, gf_ref, qb_ref, vb_ref, akb_ref, bkb_ref, gb_ref,
                of_ref, ob_ref, sf_ref, sb_ref):
    c_ = SCAN_CHUNK
    ncb = qf_ref.shape[1] // c_
    aw = HGRN_HEADS * HGRN_DK
    n_pairs = GLA_HEADS // 2
    dirs = [(True, qf_ref, vf_ref, akf_ref, bkf_ref, gf_ref, of_ref, sf_ref),
            (False, qb_ref, vb_ref, akb_ref, bkb_ref, gb_ref, ob_ref, sb_ref)]
    lane = lax.broadcasted_iota(jnp.int32, (c_, HEAD_W), 1)
    low = lane < GLA_DK
    pr = lax.broadcasted_iota(jnp.int32, (c_, 2 * c_), 0)
    pc = lax.broadcasted_iota(jnp.int32, (c_, 2 * c_), 1) % c_
    pair_causal = {True: pr >= pc, False: pc >= pr}
    srow = lax.broadcasted_iota(jnp.int32, (2 * HEAD_W, HEAD_W), 0) < HEAD_W
    scol = lax.broadcasted_iota(jnp.int32, (2 * HEAD_W, HEAD_W), 1) < GLA_DK
    pair_state = srow == scol
    zero_v = jnp.zeros((c_, HEAD_W), BF16)
    cums = {}
    for di, (fw, _, _, _, _, g_ref, _, _) in enumerate(dirs):
        tri = jnp.where(_causal(fw), 1.0, 0.0).astype(BF16)
        for c in range(ncb):
            g_hi, g_lo = _split_bf16(g_ref[0, c * c_:(c + 1) * c_, :])
            cums[di, c] = _dot(tri, g_hi) + _dot(tri, g_lo)
    ops = []
    for di, (fw, q_ref, v_ref, ak_ref, bk_ref, _, _, _) in enumerate(dirs):
        end_row, mid_row = (c_ - 1, c_ // 2 - 1) if fw else (0, c_ // 2)
        for c in range(ncb):
            rows = slice(c * c_, (c + 1) * c_)
            cum = cums[di, c]
            total = cum[end_row:end_row + 1, :]
            mid = cum[mid_row:mid_row + 1, :]
            e_mid = jnp.exp2(mid)
            e_rest = jnp.exp2(total - mid)
            e_tot = jnp.exp2(total)
            for u in range(HGRN_HEADS + n_pairs):
                pair = u >= HGRN_HEADS
                sl = slice(u * HEAD_W, (u + 1) * HEAD_W)
                k_src = bk_ref[0, rows, (u - HGRN_HEADS) * HEAD_W:(u - HGRN_HEADS + 1) * HEAD_W] if pair \
                    else ak_ref[0, rows, sl]
                qs = q_ref[0, rows, sl].astype(F32) * jnp.exp2(cum[:, sl] - mid[:, sl])
                ks = k_src.astype(F32) * jnp.exp2(mid[:, sl] - cum[:, sl])
                ksb = ks.astype(BF16)
                if pair:
                    vl = slice(aw + (u - HGRN_HEADS) * 2 * HEAD_W, aw + (u - HGRN_HEADS + 1) * 2 * HEAD_W)
                    v = v_ref[0, rows, vl]
                    k_rhs = jnp.concatenate([jnp.where(low, ksb, 0), jnp.where(low, 0, ksb)], axis=0)
                    v_rhs = jnp.concatenate([jnp.concatenate([v[:, :HEAD_W], zero_v], axis=1),
                                             jnp.concatenate([zero_v, v[:, HEAD_W:]], axis=1)], axis=0)
                    st_rows = slice(aw + (u - HGRN_HEADS) * 2 * HEAD_W, aw + (u - HGRN_HEADS + 1) * 2 * HEAD_W)
                else:
                    vl = sl
                    v = v_ref[0, rows, vl]
                    k_rhs, v_rhs = ksb, v
                    st_rows = sl
                ops.append(dict(di=di, fw=fw, c=c, pair=pair, rows=rows, vl=vl, st_rows=st_rows, v=v,
                                k_rhs=k_rhs, v_rhs=v_rhs, qs=qs.astype(BF16), qd=(qs * e_mid[:, sl]).astype(BF16),
                                kd=(ks * e_rest[:, sl]).astype(BF16), e_tot=e_tot[:, sl]))
    if with_out:
        for d in ops:
            d["sc"] = _dot_nt(d["qs"], d["k_rhs"])
        for d in ops:
            mask = pair_causal[d["fw"]] if d["pair"] else _causal(d["fw"])
            d["sc"] = jnp.where(mask, d["sc"], 0.0).astype(BF16)
        for d in ops:
            d["o"] = _dot(d["sc"], d["v_rhs"])
    for d in ops:
        d["upd"] = _dot_tn(d["v"], d["kd"])
    for d in ops:
        if d["pair"]:
            d["upd"] = jnp.where(pair_state, d["upd"], 0.0)
    for di, (fw, _, _, _, _, _, o_ref, st_ref) in enumerate(dirs):
        sts = {}
        for c in (range(ncb) if fw else range(ncb - 1, -1, -1)):
            for d in ops:
                if d["di"] != di or d["c"] != c:
                    continue
                key = d["st_rows"].start
                if key not in sts:
                    sts[key] = st_ref[d["st_rows"], :]
                if with_out:
                    o = d["o"] + _dot_nt(d["qd"], sts[key].astype(BF16))
                    o_ref[0, d["rows"], d["vl"]] = o.astype(o_ref.dtype)
                sts[key] = sts[key] * d["e_tot"] + d["upd"]
        for d in ops:
            if d["di"] == di and d["c"] == 0:
                st_ref[d["st_rows"], :] = sts[d["st_rows"].start]


def _scan_kernel(n_ctx_blocks, *refs):
    s = pl.program_id(1)

    @pl.when(s == 0)
    def _():
        refs[-2][...] = jnp.zeros_like(refs[-2])
        refs[-1][...] = jnp.zeros_like(refs[-1])

    @pl.when(s < n_ctx_blocks)
    def _():
        _scan_block(False, *refs)

    @pl.when(s >= n_ctx_blocks)
    def _():
        _scan_block(True, *refs)


def _scan_call(q, v, akf, akb, bk, gf, gb, lc):
    b_, lt, _ = q.shape
    c_ = SCAN_BLOCK
    ncc = lc // c_
    nlc = (lt - lc) // c_
    n = ncc + nlc

    def fwd(b, s):
        return (b, s, 0)

    def bwd(b, s):
        return (b, jnp.where(s < ncc, ncc - 1 - s, n + ncc - 1 - s), 0)

    def out_fwd(b, s):
        return (b, jnp.maximum(s - ncc, 0), 0)

    def out_bwd(b, s):
        return (b, jnp.where(s < ncc, nlc - 1, n - 1 - s), 0)

    def specs(index_map):
        return [pl.BlockSpec((1, c_, a.shape[-1]), index_map) for a in (q, v, akf, bk, gf)]

    v_w = v.shape[-1]
    out = jax.ShapeDtypeStruct((b_, lt - lc, v_w), BF16)
    return pl.pallas_call(
        functools.partial(_scan_kernel, ncc),
        grid=(b_, n),
        in_specs=specs(fwd) + specs(bwd),
        out_specs=[pl.BlockSpec((1, c_, v_w), out_fwd), pl.BlockSpec((1, c_, v_w), out_bwd)],
        out_shape=[out, out],
        scratch_shapes=[pltpu.VMEM((v_w, HEAD_W), F32), pltpu.VMEM((v_w, HEAD_W), F32)],
        compiler_params=pltpu.CompilerParams(
            dimension_semantics=("arbitrary", "arbitrary"), vmem_limit_bytes=VMEM_LIMIT),
    )(q, v, akf, bk, gf, q, v, akb, bk, gb)


def _merge_kernel(x_ref, a1_ref, s1_ref, gt1_ref, a2_ref, s2_ref, of_ref, ob_ref, wg_ref, nw_ref,
                  wua_ref, wub_ref, wo_ref, gpost_ref, wr_ref, br_ref,
                  x1_ref, h2_ref, idx_ref, prob_ref):
    tm, d = x_ref.shape[1:]
    hr = tm // 2
    nslab = d // LANES
    halves = [dict(i=i, rows=slice(i * hr, (i + 1) * hr)) for i in range(2)]
    for h in halves:
        h["x"] = x_ref[0, h["rows"], :]
        h["hb"] = (_rms(h["x"]) * a1_ref[0] + s1_ref[0]).astype(BF16)
    for h in halves:
        h["og"] = _dot(h["hb"], wg_ref[:, 0:1024])
        h["mg_a"] = _dot(h["hb"], wg_ref[:, 1024:2048])
        h["mg_b"] = _dot(h["hb"], wg_ref[:, 2048:3072])
    for h in halves:
        o = of_ref[0, h["rows"], :].astype(F32) + ob_ref[0, h["rows"], :].astype(F32)
        heads = [_rms(o[:, j * HEAD_W:(j + 1) * HEAD_W]) for j in range(N_HEADS)]
        og = h["og"]
        h["r"] = (jnp.concatenate(heads, axis=-1) * nw_ref[...] * (og * _sigmoid(og))).astype(BF16)
    for h in halves:
        h["y_a"] = _dot(h["r"][:, 0:512], wua_ref[...])
        h["y_b"] = _dot(h["r"][:, 512:1024], wub_ref[...])
    for h in halves:
        h["m"] = (_sigmoid(h["mg_a"]) * h["y_a"] + _sigmoid(h["mg_b"]) * h["y_b"]).astype(BF16)
    for h in halves:
        h["mix"] = _dot(h["m"], wo_ref[...])
    for h in halves:
        x1 = h["x"] + gt1_ref[0] * (_rms(h["mix"]) * gpost_ref[...])
        x1_ref[0, h["rows"], :] = x1
        h2 = _rms(x1) * a2_ref[0] + s2_ref[0]
        for s in range(nslab):
            h2_ref[pl.ds(h["i"] * hr * nslab + s, hr, stride=nslab), :] = h2[:, s * LANES:(s + 1) * LANES]
        h["h_hi"], h["h_lo"] = _split_bf16(h2)
    for h in halves:
        h["logits"] = (_dot(h["h_hi"], wr_ref[0]) + _dot(h["h_lo"], wr_ref[0]) + _dot(h["h_hi"], wr_ref[1])
                       + br_ref[...])
    for h in halves:
        logits = h["logits"]
        lane = lax.broadcasted_iota(jnp.int32, logits.shape, 1).astype(F32)
        vals, idxs = [], []
        for _ in range(TOP_K):
            m = jnp.max(logits, axis=-1, keepdims=True)
            sel = jnp.min(jnp.where(logits == m, lane, float(LANES)), axis=-1, keepdims=True)
            vals.append(m)
            idxs.append(sel)
            logits = jnp.where(lane == sel, -jnp.inf, logits)
        exps = [jnp.exp(v_ - vals[0]) for v_ in vals]
        denom = exps[0] + exps[1] + exps[2] + exps[3]
        idx_out = jnp.zeros_like(lane)
        prob_out = jnp.zeros_like(lane)
        for k_ in range(TOP_K):
            idx_out = jnp.where(lane == float(k_), idxs[k_], idx_out)
            prob_out = jnp.where(lane == float(k_), exps[k_] / denom, prob_out)
        idx_ref[0, h["rows"], :] = idx_out.astype(jnp.int32)
        prob_ref[0, h["rows"], :] = prob_out


def _merge_call(x, a1, s1, gt1, a2, s2, o_f, o_b, w_gates, norm_w, w_up_a, w_up_b, w_o, g_post, w_r, b_r):
    b_, l_, d = x.shape
    tm = LATENT_TILE
    tile = lambda b, t: (b, t, 0)
    per_b = lambda b, t: (b, 0, 0)
    const2 = lambda b, t: (0, 0)
    return pl.pallas_call(
        _merge_kernel,
        grid=(b_, l_ // tm),
        in_specs=[
            pl.BlockSpec((1, tm, d), tile),
            pl.BlockSpec((1, 1, d), per_b), pl.BlockSpec((1, 1, d), per_b), pl.BlockSpec((1, 1, d), per_b),
            pl.BlockSpec((1, 1, d), per_b), pl.BlockSpec((1, 1, d), per_b),
            pl.BlockSpec((1, tm, 1024), tile), pl.BlockSpec((1, tm, 1024), tile),
            pl.BlockSpec(w_gates.shape, const2),
            pl.BlockSpec(norm_w.shape, const2),
            pl.BlockSpec(w_up_a.shape, const2),
            pl.BlockSpec(w_up_b.shape, const2),
            pl.BlockSpec(w_o.shape, const2),
            pl.BlockSpec(g_post.shape, const2),
            pl.BlockSpec(w_r.shape, lambda b, t: (0, 0, 0)),
            pl.BlockSpec(b_r.shape, const2),
        ],
        out_specs=[
            pl.BlockSpec((1, tm, d), tile),
            pl.BlockSpec((tm * (d // LANES), LANES), lambda b, t: (b * (l_ // tm) + t, 0)),
            pl.BlockSpec((1, tm, LANES), tile), pl.BlockSpec((1, tm, LANES), tile),
        ],
        out_shape=[
            jax.ShapeDtypeStruct((b_, l_, d), F32), jax.ShapeDtypeStruct((b_ * l_ * (d // LANES), LANES), F32),
            jax.ShapeDtypeStruct((b_, l_, LANES), jnp.int32), jax.ShapeDtypeStruct((b_, l_, LANES), F32),
        ],
        compiler_params=pltpu.CompilerParams(
            dimension_semantics=("arbitrary", "arbitrary"), vmem_limit_bytes=VMEM_LIMIT),
    )(x, a1, s1, gt1, a2, s2, o_f, o_b, w_gates, norm_w, w_up_a, w_up_b, w_o, g_post, w_r, b_r)


def _moe_kernel(be_ref, nv_ref, tok_ref, tokn_ref, dst_ref, h2_hbm,
                wg_ref, bg_ref, wu_ref, bu_ref, wd_ref, bd_ref, out_hbm,
                xbuf, ybuf, wgb, wub, wdb, gsem, ssem):
    i = pl.program_id(0)
    n = pl.num_programs(0)
    slot = i % 2
    rows = MOE_BLOCK
    nslab = xbuf.shape[0] // (2 * rows)
    blk = rows * nslab

    def slab(ref, start):
        return ref.at[pl.ds(pl.multiple_of(start, nslab), nslab), :]

    def for_slot(fn):
        for s_static in range(2):
            @pl.when(slot == s_static)
            def _():
                fn(s_static)

    def issue_gather(idx_ref, s):
        for r in range(rows):
            pltpu.make_async_copy(slab(h2_hbm, idx_ref[0, 0, r]), xbuf.at[pl.ds(s * blk + r * nslab, nslab), :],
                                  gsem.at[s]).start(priority=r % 2)

    def wait_gather(s):
        pltpu.make_async_copy(h2_hbm.at[pl.ds(0, blk), :], xbuf.at[pl.ds(pl.multiple_of(s * blk, blk), blk), :],
                              gsem.at[s]).wait()

    def issue_scatter(s):
        for r in range(rows):
            pltpu.make_async_copy(ybuf.at[pl.ds(s * blk + r * nslab, nslab), :], slab(out_hbm, dst_ref[0, 0, r]),
                                  ssem.at[s]).start(priority=r % 2)

    def wait_scatter(s):
        pltpu.make_async_copy(ybuf.at[pl.ds(pl.multiple_of(s * blk, blk), blk), :], out_hbm.at[pl.ds(0, blk), :],
                              ssem.at[s]).wait()

    @pl.when(jnp.logical_and(i == 0, nv_ref[0] > 0))
    def _():
        issue_gather(tok_ref, 0)

    nxt = jnp.minimum(i + 1, n - 1)

    @pl.when(jnp.logical_and(i + 1 < n, nv_ref[nxt] > 0))
    def _():
        for_slot(lambda s: issue_gather(tokn_ref, 1 - s))

    prev = jnp.maximum(i - 1, 0)

    @pl.when(jnp.logical_or(i == 0, be_ref[i] != be_ref[prev]))
    def _():
        wgb[...] = wg_ref[0].astype(BF16)
        wub[...] = wu_ref[0].astype(BF16)
        wdb[...] = wd_ref[0].astype(BF16)

    @pl.when(jnp.logical_and(i >= 2, nv_ref[jnp.maximum(i - 2, 0)] > 0))
    def _():
        wait_scatter(slot)

    @pl.when(nv_ref[i] > 0)
    def _():
        wait_gather(slot)
        base = slot * blk
        xb = jnp.concatenate([xbuf[pl.ds(base + s, rows, stride=nslab), :] for s in range(nslab)],
                             axis=-1).astype(BF16)
        g = jnp.minimum(_dot(xb, wgb[...]) + bg_ref[0], SWIGLU_LIMIT)
        u = jnp.clip(_dot(xb, wub[...]) + bu_ref[0], -SWIGLU_LIMIT, SWIGLU_LIMIT)
        act = g * _sigmoid(SWIGLU_ALPHA * g) * (u + 1.0)
        y = _dot(act.astype(BF16), wdb[...]) + bd_ref[0]
        for s in range(nslab):
            ybuf[pl.ds(base + s, rows, stride=nslab), :] = y[:, s * LANES:(s + 1) * LANES]
        for_slot(issue_scatter)

    @pl.when(nv_ref[i] == 0)
    def _():
        ybuf[pl.ds(pl.multiple_of(slot * blk, blk), blk), :] = jnp.zeros((blk, LANES), F32)
        fill = pltpu.make_async_copy(ybuf.at[pl.ds(pl.multiple_of(slot * blk, blk), blk), :],
                                     out_hbm.at[pl.ds(pl.multiple_of(dst_ref[0, 0, 0], nslab), blk), :], ssem.at[slot])
        fill.start()
        fill.wait()

    @pl.when(i == n - 1)
    def _():
        @pl.when(jnp.logical_and(i >= 1, nv_ref[prev] > 0))
        def _():
            wait_scatter(1 - slot)

        @pl.when(nv_ref[i] > 0)
        def _():
            wait_scatter(slot)


def _moe_call(block_expert, n_valid, tok, dst, h2, n_out_rows, w_gate, b_gate, w_up, b_up, w_down, b_down):
    e_, d, f = w_gate.shape
    nslab = d // LANES
    nblk = tok.shape[0]
    rows = MOE_BLOCK

    def blk(i, be, nv):
        return (i, 0, 0)

    def blk_next(i, be, nv):
        return (jnp.minimum(i + 1, nblk - 1), 0, 0)

    def expert(i, be, nv):
        return (be[i], 0, 0)

    smem_rows = functools.partial(pl.BlockSpec, (1, 1, rows), memory_space=pltpu.SMEM)
    grid_spec = pltpu.PrefetchScalarGridSpec(
        num_scalar_prefetch=2,
        grid=(nblk,),
        in_specs=[
            smem_rows(blk), smem_rows(blk_next), smem_rows(blk),
            pl.BlockSpec(memory_space=pl.ANY),
            pl.BlockSpec((1, d, f), expert), pl.BlockSpec((1, 1, f), expert),
            pl.BlockSpec((1, d, f), expert), pl.BlockSpec((1, 1, f), expert),
            pl.BlockSpec((1, f, d), expert), pl.BlockSpec((1, 1, d), expert),
        ],
        out_specs=pl.BlockSpec(memory_space=pl.ANY),
        scratch_shapes=[
            pltpu.VMEM((2 * rows * nslab, LANES), F32), pltpu.VMEM((2 * rows * nslab, LANES), F32),
            pltpu.VMEM((d, f), BF16), pltpu.VMEM((d, f), BF16), pltpu.VMEM((f, d), BF16),
            pltpu.SemaphoreType.DMA((2,)), pltpu.SemaphoreType.DMA((2,)),
        ],
    )
    return pl.pallas_call(
        _moe_kernel,
        grid_spec=grid_spec,
        out_shape=jax.ShapeDtypeStruct((n_out_rows * nslab, LANES), F32),
        compiler_params=pltpu.CompilerParams(
            dimension_semantics=("arbitrary",), vmem_limit_bytes=VMEM_LIMIT),
    )(block_expert, n_valid, tok, tok, dst, h2,
      w_gate, b_gate.reshape(e_, 1, f), w_up, b_up.reshape(e_, 1, f), w_down, b_down.reshape(e_, 1, d))


def _dispatch_plan(top_idx, n_experts, nslab):
    t_ = top_idx.shape[0]
    n_pairs = t_ * TOP_K
    flat_e = top_idx.reshape(-1)
    order = jnp.argsort(flat_e).astype(jnp.int32)
    counts = jnp.sum((flat_e[:, None] == jnp.arange(n_experts, dtype=jnp.int32)[None, :]).astype(jnp.int32), axis=0)
    nblk_e = (counts + MOE_BLOCK - 1) // MOE_BLOCK
    blk_end = jnp.cumsum(nblk_e)
    blk_start = blk_end - nblk_e
    start = jnp.cumsum(counts) - counts
    nblk = n_pairs // MOE_BLOCK + n_experts
    b = jnp.arange(nblk, dtype=jnp.int32)
    be = jnp.minimum(jnp.sum((b[:, None] >= blk_end[None, :]).astype(jnp.int32), axis=1), n_experts - 1)
    first_row = (b - blk_start[be]) * MOE_BLOCK
    n_valid = jnp.clip(counts[be] - first_row, 0, MOE_BLOCK).astype(jnp.int32)
    r = jnp.arange(MOE_BLOCK, dtype=jnp.int32)[None, :]
    valid = r < n_valid[:, None]
    src = jnp.clip(start[be][:, None] + first_row[:, None] + r, 0, n_pairs - 1)
    pair = jnp.where(valid, order[src], 0).astype(jnp.int32)
    tok = pair // TOP_K
    spare = (jnp.cumsum((~valid).reshape(-1).astype(jnp.int32)) - 1).reshape(nblk, MOE_BLOCK)
    dst = jnp.where(valid, (pair % TOP_K) * t_ + tok, n_pairs + spare).astype(jnp.int32)
    n_out_rows = n_pairs + n_experts * MOE_BLOCK
    return (be, n_valid, (tok * nslab).reshape(nblk, 1, MOE_BLOCK), (dst * nslab).reshape(nblk, 1, MOE_BLOCK),
            n_out_rows)


def _final_kernel(x1_ref, y0_ref, y1_ref, y2_ref, y3_ref, p_ref, gt2_ref, gpost_ref, o_ref):
    p = p_ref[0]
    tm, d = x1_ref.shape[1:]
    nslab = d // LANES
    parts = []
    for s in range(nslab):
        acc = p[:, 0:1] * y0_ref[pl.ds(s, tm, stride=nslab), :]
        for k_, y_ref in ((1, y1_ref), (2, y2_ref), (3, y3_ref)):
            acc = acc + p[:, k_:k_ + 1] * y_ref[pl.ds(s, tm, stride=nslab), :]
        parts.append(acc)
    ffn = jnp.concatenate(parts, axis=-1)
    o_ref[0] = x1_ref[0] + gt2_ref[0] * (_rms(ffn) * gpost_ref[...])


def _final_call(x1, y, probs, gt2, g_post):
    b_, l_, d = x1.shape
    tm = LATENT_TILE
    nslab = d // LANES
    tiles_per_k = b_ * l_ // tm
    tile = lambda b, t: (b, t, 0)

    def y_spec(k_):
        return pl.BlockSpec((tm * nslab, LANES), lambda b, t: (k_ * tiles_per_k + b * (l_ // tm) + t, 0))

    return pl.pallas_call(
        _final_kernel,
        grid=(b_, l_ // tm),
        in_specs=[
            pl.BlockSpec((1, tm, d), tile),
            y_spec(0), y_spec(1), y_spec(2), y_spec(3),
            pl.BlockSpec((1, tm, LANES), tile),
            pl.BlockSpec((1, 1, d), lambda b, t: (b, 0, 0)),
            pl.BlockSpec(g_post.shape, lambda b, t: (0, 0)),
        ],
        out_specs=pl.BlockSpec((1, tm, d), tile),
        out_shape=jax.ShapeDtypeStruct((b_, l_, d), F32),
        compiler_params=pltpu.CompilerParams(
            dimension_semantics=("arbitrary", "arbitrary"), vmem_limit_bytes=VMEM_LIMIT),
    )(x1, y, y, y, y, probs, gt2, g_post)


def kernel(x, c, ctx, c_ctx, w_ada, b_ada, g_pre_mix, g_post_mix, g_pre_ffn, g_post_ffn, w_in, hgrn_lb, hgrn_norm_w, gla_gk_w2, gla_gk_b, gla_norm_w, w_up_a, w_up_b, w_o, w_router, b_router, w_gate, b_gate, w_up, b_up, w_down, b_down):
    b_, l_, d = x.shape
    lc = ctx.shape[1]
    n_experts = w_router.shape[-1]
    layer = 0
    hw = HGRN_HEADS * HGRN_DK
    kw = GLA_HEADS * GLA_DK
    vw = GLA_HEADS * GLA_DV
    rk = GLA_GATE_RANK

    rows = -(-(b_ + 1) // 8) * 8
    cc = jnp.concatenate([c, c_ctx[None, :], jnp.zeros((rows - b_ - 1, d), F32)], axis=0)
    mod = _ada_call(cc, w_ada[layer], b_ada[layer])[:b_ + 1]
    sh1, sc1, gt1, sh2, sc2, gt2 = [m.reshape(b_ + 1, 1, d) for m in jnp.split(mod, 6, axis=-1)]
    a1 = g_pre_mix[layer] * (1.0 + sc1)
    a2 = g_pre_ffn[layer] * (1.0 + sc2)

    w = w_in[layer]
    o0 = 0
    cols = {}
    for name, size in (("qa", hw), ("zf", hw), ("zb", hw), ("ia", hw), ("oga", hw), ("qb", kw), ("kb", kw),
                       ("vb", vw), ("rf", rk), ("rb", rk), ("ogb", vw), ("mga", d), ("mgb", d)):
        cols[name] = w[:, o0:o0 + size]
        o0 += size
    w_scan = jnp.concatenate([
        cols["qa"], cols["zf"], cols["zb"], cols["ia"],
        cols["qb"] * (GLA_DK ** -0.5), cols["kb"], cols["vb"]], axis=1).astype(BF16)
    w_rank = jnp.pad(jnp.concatenate([cols["rf"], cols["rb"]], axis=1), ((0, 0), (0, LANES - 2 * rk))).astype(BF16)
    w2 = jnp.zeros((LANES, 2 * kw), F32)
    w2 = w2.at[0:rk, 0:kw].set(gla_gk_w2[layer, 0])
    w2 = w2.at[rk:2 * rk, kw:].set(gla_gk_w2[layer, 1])
    w2 = w2.astype(BF16)
    gkb = jnp.concatenate([gla_gk_b[layer, 0], gla_gk_b[layer, 1]])[None, :]
    lb = jnp.cumsum(jax.nn.softmax(hgrn_lb.astype(F32), axis=0), axis=0)[layer]
    w_gates = jnp.concatenate([cols["oga"], cols["ogb"], cols["mga"], cols["mgb"]], axis=1).astype(BF16)
    norm_w = jnp.concatenate([jnp.tile(hgrn_norm_w[layer], HGRN_HEADS), jnp.tile(gla_norm_w[layer], GLA_HEADS)])[None, :]
    w_r = jnp.stack(_split_bf16(jnp.pad(w_router[layer], ((0, 0), (0, LANES - n_experts)))))
    b_r = jnp.pad(b_router[layer], (0, LANES - n_experts), constant_values=-1e30)[None, :]

    q, v, akf, akb, bk, gf, gb = _inproj_call(x, ctx, a1, sh1, w_scan, w_rank, w2, gkb, lb)
    o_f, o_b = _scan_call(q, v, akf, akb, bk, gf, gb, lc)
    x1, h2, top_idx, probs = _merge_call(
        x, a1, sh1, gt1, a2, sh2, o_f, o_b, w_gates, norm_w,
        w_up_a[layer].astype(BF16), w_up_b[layer].astype(BF16), w_o[layer].astype(BF16),
        g_post_mix[layer][None, :], w_r, b_r)

    t_ = b_ * l_
    be, n_valid, tok, dst, n_out_rows = _dispatch_plan(top_idx.reshape(t_, LANES)[:, :TOP_K], n_experts, d // LANES)
    y = _moe_call(be, n_valid, tok, dst, h2, n_out_rows,
                  w_gate[layer], b_gate[layer], w_up[layer], b_up[layer], w_down[layer], b_down[layer])
    return _final_call(x1, y, probs, gt2, g_post_ffn[layer][None, :])
```

```python
import functools

import jax
import jax.numpy as jnp
from jax import lax
from jax.experimental import pallas as pl
from jax.experimental.pallas import tpu as pltpu

F32 = jnp.float32
BF16 = jnp.bfloat16

NORM_EPS = 1e-6
HGRN_HEADS = 4
HGRN_DK = 128
GLA_HEADS = 4
GLA_DK = 64
GLA_DV = 128
GLA_GATE_RANK = 16
GLA_GATE_NORMALIZER = 16.0
N_HEADS = HGRN_HEADS + GLA_HEADS
HEAD_W = 128
TOP_K = 4
SWIGLU_LIMIT = 7.0
SWIGLU_ALPHA = 1.702
MOE_BLOCK = 256
SCAN_CHUNK = 64
SCAN_BLOCK = 256
LOG2E = 1.4426950408889634
TOKEN_TILE = 256
LATENT_TILE = 512
LANES = 128
VMEM_LIMIT = 56 * 1024 * 1024


def _dot(a, b):
    return jnp.dot(a, b, preferred_element_type=F32)


def _dot_nt(a, b):
    return lax.dot_general(a, b, (((1,), (1,)), ((), ())), preferred_element_type=F32)


def _dot_tn(a, b):
    return lax.dot_general(a, b, (((0,), (0,)), ((), ())), preferred_element_type=F32)


def _split_bf16(a):
    hi = a.astype(BF16)
    lo = (a - hi.astype(F32)).astype(BF16)
    return hi, lo


def _dot3(a, b):
    ah, al = _split_bf16(a)
    bh, bl = _split_bf16(b)
    return _dot(ah, bh) + _dot(al, bh) + _dot(ah, bl)


def _rms(x):
    return x * lax.rsqrt(jnp.mean(x * x, axis=-1, keepdims=True) + NORM_EPS)


def _sigmoid(x):
    return 1.0 / (1.0 + jnp.exp(-x))


def _ada_kernel(c_ref, w_ref, b_ref, o_ref):
    c = c_ref[...]
    o_ref[...] = _dot3(c * _sigmoid(c), w_ref[...]) + b_ref[...]


def _ada_call(cc, w_ada, b_ada):
    rows, d = cc.shape
    n = w_ada.shape[1]
    tn = 1536
    return pl.pallas_call(
        _ada_kernel,
        grid=(n // tn,),
        in_specs=[
            pl.BlockSpec((rows, d), lambda j: (0, 0)),
            pl.BlockSpec((d, tn), lambda j: (0, j)),
            pl.BlockSpec((1, tn), lambda j: (0, j)),
        ],
        out_specs=pl.BlockSpec((rows, tn), lambda j: (0, j)),
        out_shape=jax.ShapeDtypeStruct((rows, n), F32),
        compiler_params=pltpu.CompilerParams(
            dimension_semantics=("arbitrary",), vmem_limit_bytes=VMEM_LIMIT),
    )(cc, w_ada, b_ada.reshape(1, n))


def _inproj_kernel(n_ctx_tiles, x_ref, ctx_ref, a_ref, s_ref, actx_ref, sctx_ref, w_ref, wr_ref, w2_ref, gkb_ref,
                   lb_ref, q_ref, v_ref, akf_ref, akb_ref, bk_ref, gf_ref, gb_ref):
    t = pl.program_id(1)
    is_ctx = t < n_ctx_tiles
    aw = HGRN_HEADS * HGRN_DK
    bw = GLA_HEADS * GLA_DK
    parts = [dict(i=i) for i in range(x_ref.shape[0])]
    for p in parts:
        i = p["i"]
        xt = jnp.where(is_ctx, ctx_ref[i], x_ref[i])
        a = jnp.where(is_ctx, actx_ref[0], a_ref[i])
        s = jnp.where(is_ctx, sctx_ref[0], s_ref[i])
        p["hb"] = (_rms(xt) * a + s).astype(BF16)
    for p in parts:
        hb = p["hb"]

        def seg(lo, hi):
            return _dot(hb, w_ref[:, lo:hi])

        p["z_f"], p["z_b"] = seg(aw, 2 * aw), seg(2 * aw, 3 * aw)
        r = _dot(hb, wr_ref[...]).astype(BF16)
        p["pre"] = _dot(r, w2_ref[...]) + gkb_ref[...]
        p["q_a"], p["v_a"] = seg(0, aw), seg(3 * aw, 4 * aw)
        p["q_b"], p["k_b"] = seg(4 * aw, 4 * aw + bw), seg(4 * aw + bw, 4 * aw + 2 * bw)
        p["v_b"] = seg(4 * aw + 2 * bw, 5 * aw + 2 * bw)
    for p in parts:
        i = p["i"]
        for j, z, k_ref, g_ref in ((1, p["z_f"], akf_ref, gf_ref), (2, p["z_b"], akb_ref, gb_ref)):
            lb = lb_ref[j - 1:j, :]
            sg = _sigmoid(z)
            g_ref[i, :, 0:aw] = jnp.log2(lb + (1.0 - lb) * sg)
            k_ref[i] = ((1.0 - lb) * (1.0 - sg)).astype(BF16)
        pre = p["pre"]
        ls = (jnp.minimum(pre, 0.0) - jnp.log(1.0 + jnp.exp(-jnp.abs(pre)))) * (LOG2E / GLA_GATE_NORMALIZER)
        gf_ref[i, :, aw:aw + bw] = ls[:, 0:bw]
        gb_ref[i, :, aw:aw + bw] = ls[:, bw:2 * bw]
        q_ref[i, :, 0:aw] = p["q_a"].astype(BF16)
        q_ref[i, :, aw:aw + bw] = p["q_b"].astype(BF16)
        v_ref[i, :, 0:aw] = p["v_a"].astype(BF16)
        v_ref[i, :, aw:2 * aw] = p["v_b"].astype(BF16)
        bk_ref[i] = p["k_b"].astype(BF16)


def _inproj_call(x, ctx, mod_a, mod_s, w_scan, w_rank, w2, gkb, lb):
    b_, l_, d = x.shape
    lc = ctx.shape[1]
    tm = TOKEN_TILE
    nb = 2 if b_ % 2 == 0 else 1
    nct, nlt = lc // tm, l_ // tm
    lt = lc + l_

    def x_map(b, t):
        return (b, jnp.maximum(t - nct, 0), 0)

    def ctx_map(b, t):
        return (b, jnp.minimum(t, nct - 1), 0)

    per_b = lambda b, t: (b, 0, 0)
    ctx_row = lambda b, t: (b_, 0, 0)
    const2 = lambda b, t: (0, 0)
    out_map = lambda b, t: (b, t, 0)
    aw, bw, vw = HGRN_HEADS * HGRN_DK, GLA_HEADS * GLA_DK, GLA_HEADS * GLA_DV
    qk_w, v_w = aw + bw, aw + vw

    def out(width, dtype):
        return jax.ShapeDtypeStruct((b_, lt, width), dtype), pl.BlockSpec((nb, tm, width), out_map)

    outs = [out(qk_w, BF16), out(v_w, BF16), out(aw, BF16), out(aw, BF16), out(bw, BF16), out(qk_w, F32), out(qk_w, F32)]
    return pl.pallas_call(
        functools.partial(_inproj_kernel, nct),
        grid=(b_ // nb, nct + nlt),
        in_specs=[
            pl.BlockSpec((nb, tm, d), x_map),
            pl.BlockSpec((nb, tm, d), ctx_map),
            pl.BlockSpec((nb, 1, d), per_b),
            pl.BlockSpec((nb, 1, d), per_b),
            pl.BlockSpec((1, 1, d), ctx_row),
            pl.BlockSpec((1, 1, d), ctx_row),
            pl.BlockSpec(w_scan.shape, const2),
            pl.BlockSpec(w_rank.shape, const2),
            pl.BlockSpec(w2.shape, const2),
            pl.BlockSpec(gkb.shape, const2),
            pl.BlockSpec(lb.shape, const2),
        ],
        out_specs=[o[1] for o in outs],
        out_shape=[o[0] for o in outs],
        compiler_params=pltpu.CompilerParams(
            dimension_semantics=("arbitrary", "arbitrary"), vmem_limit_bytes=VMEM_LIMIT),
    )(x, ctx, mod_a, mod_s, mod_a, mod_s, w_scan, w_rank, w2, gkb, lb)


def _causal(forward):
    row = lax.broadcasted_iota(jnp.int32, (SCAN_CHUNK, SCAN_CHUNK), 0)
    col = lax.broadcasted_iota(jnp.int32, (SCAN_CHUNK, SCAN_CHUNK), 1)
    return (row >= col) if forward else (col >= row)


def _scan_block(with_out, qf_ref, vf_ref, akf_ref, bkf_ref, gf_ref, qb_ref, vb_ref, akb_ref, bkb_ref, gb_ref,
                of_ref, ob_ref, sf_ref, sb_ref):
    c_ = SCAN_CHUNK
    ncb = qf_ref.shape[1] // c_
    aw = HGRN_HEADS * HGRN_DK
    n_pairs = GLA_HEADS // 2
    dirs = [(True, qf_ref, vf_ref, akf_ref, bkf_ref, gf_ref, of_ref, sf_ref),
            (False, qb_ref, vb_ref, akb_ref, bkb_ref, gb_ref, ob_ref, sb_ref)]
    lane = lax.broadcasted_iota(jnp.int32, (c_, HEAD_W), 1)
    low = lane < GLA_DK
    pr = lax.broadcasted_iota(jnp.int32, (c_, 2 * c_), 0)
    pc = lax.broadcasted_iota(jnp.int32, (c_, 2 * c_), 1) % c_
    pair_causal = {True: pr >= pc, False: pc >= pr}
    srow = lax.broadcasted_iota(jnp.int32, (2 * HEAD_W, HEAD_W), 0) < HEAD_W
    scol = lax.broadcasted_iota(jnp.int32, (2 * HEAD_W, HEAD_W), 1) < GLA_DK
    pair_state = srow == scol
    zero_v = jnp.zeros((c_, HEAD_W), BF16)
    cums = {}
    for di, (fw, _, _, _, _, g_ref, _, _) in enumerate(dirs):
        tri = jnp.where(_causal(fw), 1.0, 0.0).astype(BF16)
        for c in range(ncb):
            g_hi, g_lo = _split_bf16(g_ref[0, c * c_:(c + 1) * c_, :])
            cums[di, c] = _dot(tri, g_hi) + _dot(tri, g_lo)
    ops = []
    for di, (fw, q_ref, v_ref, ak_ref, bk_ref, _, _, _) in enumerate(dirs):
        end_row, mid_row = (c_ - 1, c_ // 2 - 1) if fw else (0, c_ // 2)
        for c in range(ncb):
            rows = slice(c * c_, (c + 1) * c_)
            cum = cums[di, c]
            total = cum[end_row:end_row + 1, :]
            mid = cum[mid_row:mid_row + 1, :]
            e_mid = jnp.exp2(mid)
            e_rest = jnp.exp2(total - mid)
            e_tot = jnp.exp2(total)
            for u in range(HGRN_HEADS + n_pairs):
                pair = u >= HGRN_HEADS
                sl = slice(u * HEAD_W, (u + 1) * HEAD_W)
                k_src = bk_ref[0, rows, (u - HGRN_HEADS) * HEAD_W:(u - HGRN_HEADS + 1) * HEAD_W] if pair \
                    else ak_ref[0, rows, sl]
                qs = q_ref[0, rows, sl].astype(F32) * jnp.exp2(cum[:, sl] - mid[:, sl])
                ks = k_src.astype(F32) * jnp.exp2(mid[:, sl] - cum[:, sl])
                ksb = ks.astype(BF16)
                if pair:
                    vl = slice(aw + (u - HGRN_HEADS) * 2 * HEAD_W, aw + (u - HGRN_HEADS + 1) * 2 * HEAD_W)
                    v = v_ref[0, rows, vl]
                    k_rhs = jnp.concatenate([jnp.where(low, ksb, 0), jnp.where(low, 0, ksb)], axis=0)
                    v_rhs = jnp.concatenate([jnp.concatenate([v[:, :HEAD_W], zero_v], axis=1),
                                             jnp.concatenate([zero_v, v[:, HEAD_W:]], axis=1)], axis=0)
                    st_rows = slice(aw + (u - HGRN_HEADS) * 2 * HEAD_W, aw + (u - HGRN_HEADS + 1) * 2 * HEAD_W)
                else:
                    vl = sl
                    v = v_ref[0, rows, vl]
                    k_rhs, v_rhs = ksb, v
                    st_rows = sl
                ops.append(dict(di=di, fw=fw, c=c, pair=pair, rows=rows, vl=vl, st_rows=st_rows, v=v,
                                k_rhs=k_rhs, v_rhs=v_rhs, qs=qs.astype(BF16), qd=(qs * e_mid[:, sl]).astype(BF16),
                                kd=(ks * e_rest[:, sl]).astype(BF16), e_tot=e_tot[:, sl]))
    if with_out:
        for d in ops:
            d["sc"] = _dot_nt(d["qs"], d["k_rhs"])
        for d in ops:
            mask = pair_causal[d["fw"]] if d["pair"] else _causal(d["fw"])
            d["sc"] = jnp.where(mask, d["sc"], 0.0).astype(BF16)
        for d in ops:
            d["o"] = _dot(d["sc"], d["v_rhs"])
    for d in ops:
        d["upd"] = _dot_tn(d["v"], d["kd"])
    for d in ops:
        if d["pair"]:
            d["upd"] = jnp.where(pair_state, d["upd"], 0.0)
    for di, (fw, _, _, _, _, _, o_ref, st_ref) in enumerate(dirs):
        sts = {}
        for c in (range(ncb) if fw else range(ncb - 1, -1, -1)):
            for d in ops:
                if d["di"] != di or d["c"] != c:
                    continue
                key = d["st_rows"].start
                if key not in sts:
                    sts[key] = st_ref[d["st_rows"], :]
                if with_out:
                    o = d["o"] + _dot_nt(d["qd"], sts[key].astype(BF16))
                    o_ref[0, d["rows"], d["vl"]] = o.astype(o_ref.dtype)
                sts[key] = sts[key] * d["e_tot"] + d["upd"]
        for d in ops:
            if d["di"] == di and d["c"] == 0:
                st_ref[d["st_rows"], :] = sts[d["st_rows"].start]


def _scan_kernel(n_ctx_blocks, *refs):
    s = pl.program_id(1)

    @pl.when(s == 0)
    def _():
        refs[-2][...] = jnp.zeros_like(refs[-2])
        refs[-1][...] = jnp.zeros_like(refs[-1])

    @pl.when(s < n_ctx_blocks)
    def _():
        _scan_block(False, *refs)

    @pl.when(s >= n_ctx_blocks)
    def _():
        _scan_block(True, *refs)


def _scan_call(q, v, akf, akb, bk, gf, gb, lc):
    b_, lt, _ = q.shape
    c_ = SCAN_BLOCK
    ncc = lc // c_
    nlc = (lt - lc) // c_
    n = ncc + nlc

    def fwd(b, s):
        return (b, s, 0)

    def bwd(b, s):
        return (b, jnp.where(s < ncc, ncc - 1 - s, n + ncc - 1 - s), 0)

    def out_fwd(b, s):
        return (b, jnp.maximum(s - ncc, 0), 0)

    def out_bwd(b, s):
        return (b, jnp.where(s < ncc, nlc - 1, n - 1 - s), 0)

    def specs(index_map):
        return [pl.BlockSpec((1, c_, a.shape[-1]), index_map) for a in (q, v, akf, bk, gf)]

    v_w = v.shape[-1]
    out = jax.ShapeDtypeStruct((b_, lt - lc, v_w), BF16)
    return pl.pallas_call(
        functools.partial(_scan_kernel, ncc),
        grid=(b_, n),
        in_specs=specs(fwd) + specs(bwd),
        out_specs=[pl.BlockSpec((1, c_, v_w), out_fwd), pl.BlockSpec((1, c_, v_w), out_bwd)],
        out_shape=[out, out],
        scratch_shapes=[pltpu.VMEM((v_w, HEAD_W), F32), pltpu.VMEM((v_w, HEAD_W), F32)],
        compiler_params=pltpu.CompilerParams(
            dimension_semantics=("arbitrary", "arbitrary"), vmem_limit_bytes=VMEM_LIMIT),
    )(q, v, akf, bk, gf, q, v, akb, bk, gb)


def _merge_kernel(x_ref, a1_ref, s1_ref, gt1_ref, a2_ref, s2_ref, of_ref, ob_ref, wg_ref, nw_ref,
                  wua_ref, wub_ref, wo_ref, gpost_ref, wr_ref, br_ref,
                  x1_ref, h2_ref, idx_ref, prob_ref):
    tm, d = x_ref.shape[1:]
    hr = tm // 2
    nslab = d // LANES
    halves = [dict(i=i, rows=slice(i * hr, (i + 1) * hr)) for i in range(2)]
    for h in halves:
        h["x"] = x_ref[0, h["rows"], :]
        h["hb"] = (_rms(h["x"]) * a1_ref[0] + s1_ref[0]).astype(BF16)
    for h in halves:
        h["og"] = _dot(h["hb"], wg_ref[:, 0:1024])
        h["mg_a"] = _dot(h["hb"], wg_ref[:, 1024:2048])
        h["mg_b"] = _dot(h["hb"], wg_ref[:, 2048:3072])
    for h in halves:
        o = of_ref[0, h["rows"], :].astype(F32) + ob_ref[0, h["rows"], :].astype(F32)
        heads = [_rms(o[:, j * HEAD_W:(j + 1) * HEAD_W]) for j in range(N_HEADS)]
        og = h["og"]
        h["r"] = (jnp.concatenate(heads, axis=-1) * nw_ref[...] * (og * _sigmoid(og))).astype(BF16)
    for h in halves:
        h["y_a"] = _dot(h["r"][:, 0:512], wua_ref[...])
        h["y_b"] = _dot(h["r"][:, 512:1024], wub_ref[...])
    for h in halves:
        h["m"] = (_sigmoid(h["mg_a"]) * h["y_a"] + _sigmoid(h["mg_b"]) * h["y_b"]).astype(BF16)
    for h in halves:
        h["mix"] = _dot(h["m"], wo_ref[...])
    for h in halves:
        x1 = h["x"] + gt1_ref[0] * (_rms(h["mix"]) * gpost_ref[...])
        x1_ref[0, h["rows"], :] = x1
        h2 = _rms(x1) * a2_ref[0] + s2_ref[0]
        for s in range(nslab):
            h2_ref[pl.ds(h["i"] * hr * nslab + s, hr, stride=nslab), :] = h2[:, s * LANES:(s + 1) * LANES]
        h["h_hi"], h["h_lo"] = _split_bf16(h2)
    for h in halves:
        h["logits"] = (_dot(h["h_hi"], wr_ref[0]) + _dot(h["h_lo"], wr_ref[0]) + _dot(h["h_hi"], wr_ref[1])
                       + br_ref[...])
    for h in halves:
        logits = h["logits"]
        lane = lax.broadcasted_iota(jnp.int32, logits.shape, 1).astype(F32)
        vals, idxs = [], []
        for _ in range(TOP_K):
            m = jnp.max(logits, axis=-1, keepdims=True)
            sel = jnp.min(jnp.where(logits == m, lane, float(LANES)), axis=-1, keepdims=True)
            vals.append(m)
            idxs.append(sel)
            logits = jnp.where(lane == sel, -jnp.inf, logits)
        exps = [jnp.exp(v_ - vals[0]) for v_ in vals]
        denom = exps[0] + exps[1] + exps[2] + exps[3]
        idx_out = jnp.zeros_like(lane)
        prob_out = jnp.zeros_like(lane)
        for k_ in range(TOP_K):
            idx_out = jnp.where(lane == float(k_), idxs[k_], idx_out)
            prob_out = jnp.where(lane == float(k_), exps[k_] / denom, prob_out)
        idx_ref[0, h["rows"], :] = idx_out.astype(jnp.int32)
        prob_ref[0, h["rows"], :] = prob_out


def _merge_call(x, a1, s1, gt1, a2, s2, o_f, o_b, w_gates, norm_w, w_up_a, w_up_b, w_o, g_post, w_r, b_r):
    b_, l_, d = x.shape
    tm = LATENT_TILE
    tile = lambda b, t: (b, t, 0)
    per_b = lambda b, t: (b, 0, 0)
    const2 = lambda b, t: (0, 0)
    return pl.pallas_call(
        _merge_kernel,
        grid=(b_, l_ // tm),
        in_specs=[
            pl.BlockSpec((1, tm, d), tile),
            pl.BlockSpec((1, 1, d), per_b), pl.BlockSpec((1, 1, d), per_b), pl.BlockSpec((1, 1, d), per_b),
            pl.BlockSpec((1, 1, d), per_b), pl.BlockSpec((1, 1, d), per_b),
            pl.BlockSpec((1, tm, 1024), tile), pl.BlockSpec((1, tm, 1024), tile),
            pl.BlockSpec(w_gates.shape, const2),
            pl.BlockSpec(norm_w.shape, const2),
            pl.BlockSpec(w_up_a.shape, const2),
            pl.BlockSpec(w_up_b.shape, const2),
            pl.BlockSpec(w_o.shape, const2),
            pl.BlockSpec(g_post.shape, const2),
            pl.BlockSpec(w_r.shape, lambda b, t: (0, 0, 0)),
            pl.BlockSpec(b_r.shape, const2),
        ],
        out_specs=[
            pl.BlockSpec((1, tm, d), tile),
            pl.BlockSpec((tm * (d // LANES), LANES), lambda b, t: (b * (l_ // tm) + t, 0)),
            pl.BlockSpec((1, tm, LANES), tile), pl.BlockSpec((1, tm, LANES), tile),
        ],
        out_shape=[
            jax.ShapeDtypeStruct((b_, l_, d), F32), jax.ShapeDtypeStruct((b_ * l_ * (d // LANES), LANES), F32),
            jax.ShapeDtypeStruct((b_, l_, LANES), jnp.int32), jax.ShapeDtypeStruct((b_, l_, LANES), F32),
        ],
        compiler_params=pltpu.CompilerParams(
            dimension_semantics=("arbitrary", "arbitrary"), vmem_limit_bytes=VMEM_LIMIT),
    )(x, a1, s1, gt1, a2, s2, o_f, o_b, w_gates, norm_w, w_up_a, w_up_b, w_o, g_post, w_r, b_r)


def _moe_kernel(be_ref, nv_ref, tok_ref, tokn_ref, dst_ref, h2_hbm,
                wg_ref, bg_ref, wu_ref, bu_ref, wd_ref, bd_ref, out_hbm,
                xbuf, ybuf, wgb, wub, wdb, gsem, ssem):
    i = pl.program_id(0)
    n = pl.num_programs(0)
    slot = i % 2
    rows = MOE_BLOCK
    nslab = xbuf.shape[0] // (2 * rows)
    blk = rows * nslab

    def slab(ref, start):
        return ref.at[pl.ds(pl.multiple_of(start, nslab), nslab), :]

    def for_slot(fn):
        for s_static in range(2):
            @pl.when(slot == s_static)
            def _():
                fn(s_static)

    def issue_gather(idx_ref, s):
        for r in range(rows):
            pltpu.make_async_copy(slab(h2_hbm, idx_ref[0, 0, r]), xbuf.at[pl.ds(s * blk + r * nslab, nslab), :],
                                  gsem.at[s]).start(priority=r % 2)

    def wait_gather(s):
        pltpu.make_async_copy(h2_hbm.at[pl.ds(0, blk), :], xbuf.at[pl.ds(pl.multiple_of(s * blk, blk), blk), :],
                              gsem.at[s]).wait()

    def issue_scatter(s):
        for r in range(rows):
            pltpu.make_async_copy(ybuf.at[pl.ds(s * blk + r * nslab, nslab), :], slab(out_hbm, dst_ref[0, 0, r]),
                                  ssem.at[s]).start(priority=r % 2)

    def wait_scatter(s):
        pltpu.make_async_copy(ybuf.at[pl.ds(pl.multiple_of(s * blk, blk), blk), :], out_hbm.at[pl.ds(0, blk), :],
                              ssem.at[s]).wait()

    @pl.when(jnp.logical_and(i == 0, nv_ref[0] > 0))
    def _():
        issue_gather(tok_ref, 0)

    nxt = jnp.minimum(i + 1, n - 1)

    @pl.when(jnp.logical_and(i + 1 < n, nv_ref[nxt] > 0))
    def _():
        for_slot(lambda s: issue_gather(tokn_ref, 1 - s))

    prev = jnp.maximum(i - 1, 0)

    @pl.when(jnp.logical_or(i == 0, be_ref[i] != be_ref[prev]))
    def _():
        wgb[...] = wg_ref[0].astype(BF16)
        wub[...] = wu_ref[0].astype(BF16)
        wdb[...] = wd_ref[0].astype(BF16)

    @pl.when(jnp.logical_and(i >= 2, nv_ref[jnp.maximum(i - 2, 0)] > 0))
    def _():
        wait_scatter(slot)

    @pl.when(nv_ref[i] > 0)
    def _():
        wait_gather(slot)
        base = slot * blk
        xb = jnp.concatenate([xbuf[pl.ds(base + s, rows, stride=nslab), :] for s in range(nslab)],
                             axis=-1).astype(BF16)
        g = jnp.minimum(_dot(xb, wgb[...]) + bg_ref[0], SWIGLU_LIMIT)
        u = jnp.clip(_dot(xb, wub[...]) + bu_ref[0], -SWIGLU_LIMIT, SWIGLU_LIMIT)
        act = g * _sigmoid(SWIGLU_ALPHA * g) * (u + 1.0)
        y = _dot(act.astype(BF16), wdb[...]) + bd_ref[0]
        for s in range(nslab):
            ybuf[pl.ds(base + s, rows, stride=nslab), :] = y[:, s * LANES:(s + 1) * LANES]
        for_slot(issue_scatter)

    @pl.when(nv_ref[i] == 0)
    def _():
        ybuf[pl.ds(pl.multiple_of(slot * blk, blk), blk), :] = jnp.zeros((blk, LANES), F32)
        fill = pltpu.make_async_copy(ybuf.at[pl.ds(pl.multiple_of(slot * blk, blk), blk), :],
                                     out_hbm.at[pl.ds(pl.multiple_of(dst_ref[0, 0, 0], nslab), blk), :], ssem.at[slot])
        fill.start()
        fill.wait()

    @pl.when(i == n - 1)
    def _():
        @pl.when(jnp.logical_and(i >= 1, nv_ref[prev] > 0))
        def _():
            wait_scatter(1 - slot)

        @pl.when(nv_ref[i] > 0)
        def _():
            wait_scatter(slot)


def _moe_call(block_expert, n_valid, tok, dst, h2, n_out_rows, w_gate, b_gate, w_up, b_up, w_down, b_down):
    e_, d, f = w_gate.shape
    nslab = d // LANES
    nblk = tok.shape[0]
    rows = MOE_BLOCK

    def blk(i, be, nv):
        return (i, 0, 0)

    def blk_next(i, be, nv):
        return (jnp.minimum(i + 1, nblk - 1), 0, 0)

    def expert(i, be, nv):
        return (be[i], 0, 0)

    smem_rows = functools.partial(pl.BlockSpec, (1, 1, rows), memory_space=pltpu.SMEM)
    grid_spec = pltpu.PrefetchScalarGridSpec(
        num_scalar_prefetch=2,
        grid=(nblk,),
        in_specs=[
            smem_rows(blk), smem_rows(blk_next), smem_rows(blk),
            pl.BlockSpec(memory_space=pl.ANY),
            pl.BlockSpec((1, d, f), expert), pl.BlockSpec((1, 1, f), expert),
            pl.BlockSpec((1, d, f), expert), pl.BlockSpec((1, 1, f), expert),
            pl.BlockSpec((1, f, d), expert), pl.BlockSpec((1, 1, d), expert),
        ],
        out_specs=pl.BlockSpec(memory_space=pl.ANY),
        scratch_shapes=[
            pltpu.VMEM((2 * rows * nslab, LANES), F32), pltpu.VMEM((2 * rows * nslab, LANES), F32),
            pltpu.VMEM((d, f), BF16), pltpu.VMEM((d, f), BF16), pltpu.VMEM((f, d), BF16),
            pltpu.SemaphoreType.DMA((2,)), pltpu.SemaphoreType.DMA((2,)),
        ],
    )
    return pl.pallas_call(
        _moe_kernel,
        grid_spec=grid_spec,
        out_shape=jax.ShapeDtypeStruct((n_out_rows * nslab, LANES), F32),
        compiler_params=pltpu.CompilerParams(
            dimension_semantics=("arbitrary",), vmem_limit_bytes=VMEM_LIMIT),
    )(block_expert, n_valid, tok, tok, dst, h2,
      w_gate, b_gate.reshape(e_, 1, f), w_up, b_up.reshape(e_, 1, f), w_down, b_down.reshape(e_, 1, d))


def _dispatch_plan(top_idx, n_experts, nslab):
    t_ = top_idx.shape[0]
    n_pairs = t_ * TOP_K
    flat_e = top_idx.reshape(-1)
    pair_bits = max(n_pairs - 1, 1).bit_length()
    keys = jnp.sort(flat_e * (1 << pair_bits) + jnp.arange(n_pairs, dtype=jnp.int32))
    order = keys & ((1 << pair_bits) - 1)
    counts = jnp.sum((flat_e[:, None] == jnp.arange(n_experts, dtype=jnp.int32)[None, :]).astype(jnp.int32), axis=0)
    nblk_e = (counts + MOE_BLOCK - 1) // MOE_BLOCK
    blk_end = jnp.cumsum(nblk_e)
    blk_start = blk_end - nblk_e
    start = jnp.cumsum(counts) - counts
    nblk = n_pairs // MOE_BLOCK + n_experts
    b = jnp.arange(nblk, dtype=jnp.int32)
    be = jnp.minimum(jnp.sum((b[:, None] >= blk_end[None, :]).astype(jnp.int32), axis=1), n_experts - 1)
    first_row = (b - blk_start[be]) * MOE_BLOCK
    n_valid = jnp.clip(counts[be] - first_row, 0, MOE_BLOCK).astype(jnp.int32)
    r = jnp.arange(MOE_BLOCK, dtype=jnp.int32)[None, :]
    valid = r < n_valid[:, None]
    src = jnp.clip(start[be][:, None] + first_row[:, None] + r, 0, n_pairs - 1)
    pair = jnp.where(valid, order[src], 0).astype(jnp.int32)
    tok = pair // TOP_K
    spare = (jnp.cumsum((~valid).reshape(-1).astype(jnp.int32)) - 1).reshape(nblk, MOE_BLOCK)
    dst = jnp.where(valid, (pair % TOP_K) * t_ + tok, n_pairs + spare).astype(jnp.int32)
    n_out_rows = n_pairs + n_experts * MOE_BLOCK
    return (be, n_valid, (tok * nslab).reshape(nblk, 1, MOE_BLOCK), (dst * nslab).reshape(nblk, 1, MOE_BLOCK),
            n_out_rows)


def _final_kernel(x1_ref, y0_ref, y1_ref, y2_ref, y3_ref, p_ref, gt2_ref, gpost_ref, o_ref):
    p = p_ref[0]
    tm, d = x1_ref.shape[1:]
    nslab = d // LANES
    parts = []
    for s in range(nslab):
        acc = p[:, 0:1] * y0_ref[pl.ds(s, tm, stride=nslab), :]
        for k_, y_ref in ((1, y1_ref), (2, y2_ref), (3, y3_ref)):
            acc = acc + p[:, k_:k_ + 1] * y_ref[pl.ds(s, tm, stride=nslab), :]
        parts.append(acc)
    ffn = jnp.concatenate(parts, axis=-1)
    o_ref[0] = x1_ref[0] + gt2_ref[0] * (_rms(ffn) * gpost_ref[...])


def _final_call(x1, y, probs, gt2, g_post):
    b_, l_, d = x1.shape
    tm = LATENT_TILE
    nslab = d // LANES
    tiles_per_k = b_ * l_ // tm
    tile = lambda b, t: (b, t, 0)

    def y_spec(k_):
        return pl.BlockSpec((tm * nslab, LANES), lambda b, t: (k_ * tiles_per_k + b * (l_ // tm) + t, 0))

    return pl.pallas_call(
        _final_kernel,
        grid=(b_, l_ // tm),
        in_specs=[
            pl.BlockSpec((1, tm, d), tile),
            y_spec(0), y_spec(1), y_spec(2), y_spec(3),
            pl.BlockSpec((1, tm, LANES), tile),
            pl.BlockSpec((1, 1, d), lambda b, t: (b, 0, 0)),
            pl.BlockSpec(g_post.shape, lambda b, t: (0, 0)),
        ],
        out_specs=pl.BlockSpec((1, tm, d), tile),
        out_shape=jax.ShapeDtypeStruct((b_, l_, d), F32),
        compiler_params=pltpu.CompilerParams(
            dimension_semantics=("arbitrary", "arbitrary"), vmem_limit_bytes=VMEM_LIMIT),
    )(x1, y, y, y, y, probs, gt2, g_post)


def kernel(x, c, ctx, c_ctx, w_ada, b_ada, g_pre_mix, g_post_mix, g_pre_ffn, g_post_ffn, w_in, hgrn_lb, hgrn_norm_w, gla_gk_w2, gla_gk_b, gla_norm_w, w_up_a, w_up_b, w_o, w_router, b_router, w_gate, b_gate, w_up, b_up, w_down, b_down):
    b_, l_, d = x.shape
    lc = ctx.shape[1]
    n_experts = w_router.shape[-1]
    layer = 0
    hw = HGRN_HEADS * HGRN_DK
    kw = GLA_HEADS * GLA_DK
    vw = GLA_HEADS * GLA_DV
    rk = GLA_GATE_RANK

    rows = -(-(b_ + 1) // 8) * 8
    cc = jnp.concatenate([c, c_ctx[None, :], jnp.zeros((rows - b_ - 1, d), F32)], axis=0)
    mod = _ada_call(cc, w_ada[layer], b_ada[layer])[:b_ + 1]
    sh1, sc1, gt1, sh2, sc2, gt2 = [m.reshape(b_ + 1, 1, d) for m in jnp.split(mod, 6, axis=-1)]
    a1 = g_pre_mix[layer] * (1.0 + sc1)
    a2 = g_pre_ffn[layer] * (1.0 + sc2)

    w = w_in[layer]
    o0 = 0
    cols = {}
    for name, size in (("qa", hw), ("zf", hw), ("zb", hw), ("ia", hw), ("oga", hw), ("qb", kw), ("kb", kw),
                       ("vb", vw), ("rf", rk), ("rb", rk), ("ogb", vw), ("mga", d), ("mgb", d)):
        cols[name] = w[:, o0:o0 + size]
        o0 += size
    w_scan = jnp.concatenate([
        cols["qa"], cols["zf"], cols["zb"], cols["ia"],
        cols["qb"] * (GLA_DK ** -0.5), cols["kb"], cols["vb"]], axis=1).astype(BF16)
    w_rank = jnp.pad(jnp.concatenate([cols["rf"], cols["rb"]], axis=1), ((0, 0), (0, LANES - 2 * rk))).astype(BF16)
    w2 = jnp.zeros((LANES, 2 * kw), F32)
    w2 = w2.at[0:rk, 0:kw].set(gla_gk_w2[layer, 0])
    w2 = w2.at[rk:2 * rk, kw:].set(gla_gk_w2[layer, 1])
    w2 = w2.astype(BF16)
    gkb = jnp.concatenate([gla_gk_b[layer, 0], gla_gk_b[layer, 1]])[None, :]
    lb = jnp.cumsum(jax.nn.softmax(hgrn_lb.astype(F32), axis=0), axis=0)[layer]
    w_gates = jnp.concatenate([cols["oga"], cols["ogb"], cols["mga"], cols["mgb"]], axis=1).astype(BF16)
    norm_w = jnp.concatenate([jnp.tile(hgrn_norm_w[layer], HGRN_HEADS), jnp.tile(gla_norm_w[layer], GLA_HEADS)])[None, :]
    w_r = jnp.stack(_split_bf16(jnp.pad(w_router[layer], ((0, 0), (0, LANES - n_experts)))))
    b_r = jnp.pad(b_router[layer], (0, LANES - n_experts), constant_values=-1e30)[None, :]

    q, v, akf, akb, bk, gf, gb = _inproj_call(x, ctx, a1, sh1, w_scan, w_rank, w2, gkb, lb)
    o_f, o_b = _scan_call(q, v, akf, akb, bk, gf, gb, lc)
    x1, h2, top_idx, probs = _merge_call(
        x, a1, sh1, gt1, a2, sh2, o_f, o_b, w_gates, norm_w,
        w_up_a[layer].astype(BF16), w_up_b[layer].astype(BF16), w_o[layer].astype(BF16),
        g_post_mix[layer][None, :], w_r, b_r)

    t_ = b_ * l_
    be, n_valid, tok, dst, n_out_rows = _dispatch_plan(top_idx.reshape(t_, LANES)[:, :TOP_K], n_experts, d // LANES)
    y = _moe_call(be, n_valid, tok, dst, h2, n_out_rows,
                  w_gate[layer], b_gate[layer], w_up[layer], b_up[layer], w_down[layer], b_down[layer])
    return _final_call(x1, y, probs, gt2, g_post_ffn[layer][None, :])
```

```python
import functools

import jax
import jax.numpy as jnp
from jax import lax
from jax.experimental import pallas as pl
from jax.experimental.pallas import tpu as pltpu

F32 = jnp.float32
BF16 = jnp.bfloat16

NORM_EPS = 1e-6
HGRN_HEADS = 4
HGRN_DK = 128
GLA_HEADS = 4
GLA_DK = 64
GLA_DV = 128
GLA_GATE_RANK = 16
GLA_GATE_NORMALIZER = 16.0
N_HEADS = HGRN_HEADS + GLA_HEADS
HEAD_W = 128
TOP_K = 4
SWIGLU_LIMIT = 7.0
SWIGLU_ALPHA = 1.702
MOE_BLOCK = 256
SCAN_CHUNK = 64
SCAN_BLOCK = 256
LOG2E = 1.4426950408889634
TOKEN_TILE = 256
LATENT_TILE = 512
LANES = 128
VMEM_LIMIT = 56 * 1024 * 1024


def _dot(a, b):
    return jnp.dot(a, b, preferred_element_type=F32)


def _dot_nt(a, b):
    return lax.dot_general(a, b, (((1,), (1,)), ((), ())), preferred_element_type=F32)


def _dot_tn(a, b):
    return lax.dot_general(a, b, (((0,), (0,)), ((), ())), preferred_element_type=F32)


def _split_bf16(a):
    hi = a.astype(BF16)
    lo = (a - hi.astype(F32)).astype(BF16)
    return hi, lo


def _dot3(a, b):
    ah, al = _split_bf16(a)
    bh, bl = _split_bf16(b)
    return _dot(ah, bh) + _dot(al, bh) + _dot(ah, bl)


def _rms(x):
    return x * lax.rsqrt(jnp.mean(x * x, axis=-1, keepdims=True) + NORM_EPS)


def _sigmoid(x):
    return 1.0 / (1.0 + jnp.exp(-x))


def _ada_kernel(c_ref, w_ref, b_ref, o_ref):
    c = c_ref[...]
    o_ref[...] = _dot3(c * _sigmoid(c), w_ref[...]) + b_ref[...]


def _ada_call(cc, w_ada, b_ada):
    rows, d = cc.shape
    n = w_ada.shape[1]
    tn = 1536
    return pl.pallas_call(
        _ada_kernel,
        grid=(n // tn,),
        in_specs=[
            pl.BlockSpec((rows, d), lambda j: (0, 0)),
            pl.BlockSpec((d, tn), lambda j: (0, j)),
            pl.BlockSpec((1, tn), lambda j: (0, j)),
        ],
        out_specs=pl.BlockSpec((rows, tn), lambda j: (0, j)),
        out_shape=jax.ShapeDtypeStruct((rows, n), F32),
        compiler_params=pltpu.CompilerParams(
            dimension_semantics=("arbitrary",), vmem_limit_bytes=VMEM_LIMIT),
    )(cc, w_ada, b_ada.reshape(1, n))


def _inproj_kernel(n_ctx_tiles, x_ref, ctx_ref, a_ref, s_ref, actx_ref, sctx_ref, w_ref, wr_ref, w2_ref, gkb_ref,
                   lb_ref, q_ref, v_ref, akf_ref, akb_ref, bk_ref, gf_ref, gb_ref):
    t = pl.program_id(1)
    is_ctx = t < n_ctx_tiles
    aw = HGRN_HEADS * HGRN_DK
    bw = GLA_HEADS * GLA_DK
    parts = [dict(i=i) for i in range(x_ref.shape[0])]
    for p in parts:
        i = p["i"]
        xt = jnp.where(is_ctx, ctx_ref[i], x_ref[i])
        a = jnp.where(is_ctx, actx_ref[0], a_ref[i])
        s = jnp.where(is_ctx, sctx_ref[0], s_ref[i])
        p["hb"] = (_rms(xt) * a + s).astype(BF16)
    for p in parts:
        hb = p["hb"]

        def seg(lo, hi):
            return _dot(hb, w_ref[:, lo:hi])

        p["z_f"], p["z_b"] = seg(aw, 2 * aw), seg(2 * aw, 3 * aw)
        r = _dot(hb, wr_ref[...]).astype(BF16)
        p["pre"] = _dot(r, w2_ref[...]) + gkb_ref[...]
        p["q_a"], p["v_a"] = seg(0, aw), seg(3 * aw, 4 * aw)
        p["q_b"], p["k_b"] = seg(4 * aw, 4 * aw + bw), seg(4 * aw + bw, 4 * aw + 2 * bw)
        p["v_b"] = seg(4 * aw + 2 * bw, 5 * aw + 2 * bw)
    for p in parts:
        i = p["i"]
        for j, z, k_ref, g_ref in ((1, p["z_f"], akf_ref, gf_ref), (2, p["z_b"], akb_ref, gb_ref)):
            lb = lb_ref[j - 1:j, :]
            sg = _sigmoid(z)
            g_ref[i, :, 0:aw] = jnp.log2(lb + (1.0 - lb) * sg)
            k_ref[i] = ((1.0 - lb) * (1.0 - sg)).astype(BF16)
        pre = p["pre"]
        ls = (jnp.minimum(pre, 0.0) - jnp.log(1.0 + jnp.exp(-jnp.abs(pre)))) * (LOG2E / GLA_GATE_NORMALIZER)
        gf_ref[i, :, aw:aw + bw] = ls[:, 0:bw]
        gb_ref[i, :, aw:aw + bw] = ls[:, bw:2 * bw]
        q_ref[i, :, 0:aw] = p["q_a"].astype(BF16)
        q_ref[i, :, aw:aw + bw] = p["q_b"].astype(BF16)
        v_ref[i, :, 0:aw] = p["v_a"].astype(BF16)
        v_ref[i, :, aw:2 * aw] = p["v_b"].astype(BF16)
        bk_ref[i] = p["k_b"].astype(BF16)


def _inproj_call(x, ctx, mod_a, mod_s, w_scan, w_rank, w2, gkb, lb):
    b_, l_, d = x.shape
    lc = ctx.shape[1]
    tm = TOKEN_TILE
    nb = 2 if b_ % 2 == 0 else 1
    nct, nlt = lc // tm, l_ // tm
    lt = lc + l_

    def x_map(b, t):
        return (b, jnp.maximum(t - nct, 0), 0)

    def ctx_map(b, t):
        return (b, jnp.minimum(t, nct - 1), 0)

    per_b = lambda b, t: (b, 0, 0)
    ctx_row = lambda b, t: (b_, 0, 0)
    const2 = lambda b, t: (0, 0)
    out_map = lambda b, t: (b, t, 0)
    aw, bw, vw = HGRN_HEADS * HGRN_DK, GLA_HEADS * GLA_DK, GLA_HEADS * GLA_DV
    qk_w, v_w = aw + bw, aw + vw

    def out(width, dtype):
        return jax.ShapeDtypeStruct((b_, lt, width), dtype), pl.BlockSpec((nb, tm, width), out_map)

    outs = [out(qk_w, BF16), out(v_w, BF16), out(aw, BF16), out(aw, BF16), out(bw, BF16), out(qk_w, F32), out(qk_w, F32)]
    return pl.pallas_call(
        functools.partial(_inproj_kernel, nct),
        grid=(b_ // nb, nct + nlt),
        in_specs=[
            pl.BlockSpec((nb, tm, d), x_map),
            pl.BlockSpec((nb, tm, d), ctx_map),
            pl.BlockSpec((nb, 1, d), per_b),
            pl.BlockSpec((nb, 1, d), per_b),
            pl.BlockSpec((1, 1, d), ctx_row),
            pl.BlockSpec((1, 1, d), ctx_row),
            pl.BlockSpec(w_scan.shape, const2),
            pl.BlockSpec(w_rank.shape, const2),
            pl.BlockSpec(w2.shape, const2),
            pl.BlockSpec(gkb.shape, const2),
            pl.BlockSpec(lb.shape, const2),
        ],
        out_specs=[o[1] for o in outs],
        out_shape=[o[0] for o in outs],
        compiler_params=pltpu.CompilerParams(
            dimension_semantics=("arbitrary", "arbitrary"), vmem_limit_bytes=VMEM_LIMIT),
    )(x, ctx, mod_a, mod_s, mod_a, mod_s, w_scan, w_rank, w2, gkb, lb)


def _causal(forward):
    row = lax.broadcasted_iota(jnp.int32, (SCAN_CHUNK, SCAN_CHUNK), 0)
    col = lax.broadcasted_iota(jnp.int32, (SCAN_CHUNK, SCAN_CHUNK), 1)
    return (row >= col) if forward else (col >= row)


def _scan_block(with_out, qf_ref, vf_ref, akf_ref, bkf_ref, gf_ref, qb_ref, vb_ref, akb_ref, bkb_ref, gb_ref,
                of_ref, ob_ref, sf_ref, sb_ref):
    c_ = SCAN_CHUNK
    ncb = qf_ref.shape[1] // c_
    aw = HGRN_HEADS * HGRN_DK
    n_pairs = GLA_HEADS // 2
    dirs = [(True, qf_ref, vf_ref, akf_ref, bkf_ref, gf_ref, of_ref, sf_ref),
            (False, qb_ref, vb_ref, akb_ref, bkb_ref, gb_ref, ob_ref, sb_ref)]
    lane = lax.broadcasted_iota(jnp.int32, (c_, HEAD_W), 1)
    low = lane < GLA_DK
    pr = lax.broadcasted_iota(jnp.int32, (c_, 2 * c_), 0)
    pc = lax.broadcasted_iota(jnp.int32, (c_, 2 * c_), 1) % c_
    pair_causal = {True: pr >= pc, False: pc >= pr}
    srow = lax.broadcasted_iota(jnp.int32, (2 * HEAD_W, HEAD_W), 0) < HEAD_W
    scol = lax.broadcasted_iota(jnp.int32, (2 * HEAD_W, HEAD_W), 1) < GLA_DK
    pair_state = srow == scol
    zero_v = jnp.zeros((c_, HEAD_W), BF16)
    cums = {}
    for di, (fw, _, _, _, _, g_ref, _, _) in enumerate(dirs):
        tri = jnp.where(_causal(fw), 1.0, 0.0).astype(BF16)
        for c in range(ncb):
            g_hi, g_lo = _split_bf16(g_ref[0, c * c_:(c + 1) * c_, :])
            cums[di, c] = _dot(tri, g_hi) + _dot(tri, g_lo)
    ops = []
    for di, (fw, q_ref, v_ref, ak_ref, bk_ref, _, _, _) in enumerate(dirs):
        end_row, mid_row = (c_ - 1, c_ // 2 - 1) if fw else (0, c_ // 2)
        for c in range(ncb):
            rows = slice(c * c_, (c + 1) * c_)
            cum = cums[di, c]
            total = cum[end_row:end_row + 1, :]
            mid = cum[mid_row:mid_row + 1, :]
            e_mid = jnp.exp2(mid)
            e_rest = jnp.exp2(total - mid)
            e_tot = jnp.exp2(total)
            for u in range(HGRN_HEADS + n_pairs):
                pair = u >= HGRN_HEADS
                sl = slice(u * HEAD_W, (u + 1) * HEAD_W)
                k_src = bk_ref[0, rows, (u - HGRN_HEADS) * HEAD_W:(u - HGRN_HEADS + 1) * HEAD_W] if pair \
                    else ak_ref[0, rows, sl]
                qs = q_ref[0, rows, sl].astype(F32) * jnp.exp2(cum[:, sl] - mid[:, sl])
                ks = k_src.astype(F32) * jnp.exp2(mid[:, sl] - cum[:, sl])
                ksb = ks.astype(BF16)
                if pair:
                    vl = slice(aw + (u - HGRN_HEADS) * 2 * HEAD_W, aw + (u - HGRN_HEADS + 1) * 2 * HEAD_W)
                    v = v_ref[0, rows, vl]
                    k_rhs = jnp.concatenate([jnp.where(low, ksb, 0), jnp.where(low, 0, ksb)], axis=0)
                    v_rhs = jnp.concatenate([jnp.concatenate([v[:, :HEAD_W], zero_v], axis=1),
                                             jnp.concatenate([zero_v, v[:, HEAD_W:]], axis=1)], axis=0)
                    st_rows = slice(aw + (u - HGRN_HEADS) * 2 * HEAD_W, aw + (u - HGRN_HEADS + 1) * 2 * HEAD_W)
                else:
                    vl = sl
                    v = v_ref[0, rows, vl]
                    k_rhs, v_rhs = ksb, v
                    st_rows = sl
                ops.append(dict(di=di, fw=fw, c=c, pair=pair, rows=rows, vl=vl, st_rows=st_rows, v=v,
                                k_rhs=k_rhs, v_rhs=v_rhs, qs=qs.astype(BF16), qd=(qs * e_mid[:, sl]).astype(BF16),
                                kd=(ks * e_rest[:, sl]).astype(BF16), e_tot=e_tot[:, sl]))
    if with_out:
        for d in ops:
            d["sc"] = _dot_nt(d["qs"], d["k_rhs"])
        for d in ops:
            mask = pair_causal[d["fw"]] if d["pair"] else _causal(d["fw"])
            d["sc"] = jnp.where(mask, d["sc"], 0.0).astype(BF16)
        for d in ops:
            d["o"] = _dot(d["sc"], d["v_rhs"])
    for d in ops:
        d["upd"] = _dot_tn(d["v"], d["kd"])
    for d in ops:
        if d["pair"]:
            d["upd"] = jnp.where(pair_state, d["upd"], 0.0)
    for di, (fw, _, _, _, _, _, o_ref, st_ref) in enumerate(dirs):
        sts = {}
        for c in (range(ncb) if fw else range(ncb - 1, -1, -1)):
            for d in ops:
                if d["di"] != di or d["c"] != c:
                    continue
                key = d["st_rows"].start
                if key not in sts:
                    sts[key] = st_ref[d["st_rows"], :]
                if with_out:
                    o = d["o"] + _dot_nt(d["qd"], sts[key].astype(BF16))
                    o_ref[0, d["rows"], d["vl"]] = o.astype(o_ref.dtype)
                sts[key] = sts[key] * d["e_tot"] + d["upd"]
        for d in ops:
            if d["di"] == di and d["c"] == 0:
                st_ref[d["st_rows"], :] = sts[d["st_rows"].start]


def _scan_kernel(n_ctx_blocks, *refs):
    s = pl.program_id(1)

    @pl.when(s == 0)
    def _():
        refs[-2][...] = jnp.zeros_like(refs[-2])
        refs[-1][...] = jnp.zeros_like(refs[-1])

    @pl.when(s < n_ctx_blocks)
    def _():
        _scan_block(False, *refs)

    @pl.when(s >= n_ctx_blocks)
    def _():
        _scan_block(True, *refs)


def _scan_call(q, v, akf, akb, bk, gf, gb, lc):
    b_, lt, _ = q.shape
    c_ = SCAN_BLOCK
    ncc = lc // c_
    nlc = (lt - lc) // c_
    n = ncc + nlc

    def fwd(b, s):
        return (b, s, 0)

    def bwd(b, s):
        return (b, jnp.where(s < ncc, ncc - 1 - s, n + ncc - 1 - s), 0)

    def out_fwd(b, s):
        return (b, jnp.maximum(s - ncc, 0), 0)

    def out_bwd(b, s):
        return (b, jnp.where(s < ncc, nlc - 1, n - 1 - s), 0)

    def specs(index_map):
        return [pl.BlockSpec((1, c_, a.shape[-1]), index_map) for a in (q, v, akf, bk, gf)]

    v_w = v.shape[-1]
    out = jax.ShapeDtypeStruct((b_, lt - lc, v_w), BF16)
    return pl.pallas_call(
        functools.partial(_scan_kernel, ncc),
        grid=(b_, n),
        in_specs=specs(fwd) + specs(bwd),
        out_specs=[pl.BlockSpec((1, c_, v_w), out_fwd), pl.BlockSpec((1, c_, v_w), out_bwd)],
        out_shape=[out, out],
        scratch_shapes=[pltpu.VMEM((v_w, HEAD_W), F32), pltpu.VMEM((v_w, HEAD_W), F32)],
        compiler_params=pltpu.CompilerParams(
            dimension_semantics=("arbitrary", "arbitrary"), vmem_limit_bytes=VMEM_LIMIT),
    )(q, v, akf, bk, gf, q, v, akb, bk, gb)


def _merge_kernel(x_ref, a1_ref, s1_ref, gt1_ref, a2_ref, s2_ref, of_ref, ob_ref, wg_ref, nw_ref,
                  wua_ref, wub_ref, wo_ref, gpost_ref, wr_ref, br_ref,
                  x1_ref, h2_ref, idx_ref, prob_ref):
    tm, d = x_ref.shape[1:]
    hr = tm // 2
    nslab = d // LANES
    halves = [dict(i=i, rows=slice(i * hr, (i + 1) * hr)) for i in range(2)]
    for h in halves:
        h["x"] = x_ref[0, h["rows"], :]
        h["hb"] = (_rms(h["x"]) * a1_ref[0] + s1_ref[0]).astype(BF16)
    for h in halves:
        h["og"] = _dot(h["hb"], wg_ref[:, 0:1024])
        h["mg_a"] = _dot(h["hb"], wg_ref[:, 1024:2048])
        h["mg_b"] = _dot(h["hb"], wg_ref[:, 2048:3072])
    for h in halves:
        o = of_ref[0, h["rows"], :].astype(F32) + ob_ref[0, h["rows"], :].astype(F32)
        heads = [_rms(o[:, j * HEAD_W:(j + 1) * HEAD_W]) for j in range(N_HEADS)]
        og = h["og"]
        h["r"] = (jnp.concatenate(heads, axis=-1) * nw_ref[...] * (og * _sigmoid(og))).astype(BF16)
    for h in halves:
        h["y_a"] = _dot(h["r"][:, 0:512], wua_ref[...])
        h["y_b"] = _dot(h["r"][:, 512:1024], wub_ref[...])
    for h in halves:
        h["m"] = (_sigmoid(h["mg_a"]) * h["y_a"] + _sigmoid(h["mg_b"]) * h["y_b"]).astype(BF16)
    for h in halves:
        h["mix"] = _dot(h["m"], wo_ref[...])
    for h in halves:
        x1 = h["x"] + gt1_ref[0] * (_rms(h["mix"]) * gpost_ref[...])
        x1_ref[0, h["rows"], :] = x1
        h2 = _rms(x1) * a2_ref[0] + s2_ref[0]
        for s in range(nslab):
            h2_ref[pl.ds(h["i"] * hr * nslab + s, hr, stride=nslab), :] = h2[:, s * LANES:(s + 1) * LANES]
        h["h_hi"], h["h_lo"] = _split_bf16(h2)
    for h in halves:
        h["logits"] = (_dot(h["h_hi"], wr_ref[0]) + _dot(h["h_lo"], wr_ref[0]) + _dot(h["h_hi"], wr_ref[1])
                       + br_ref[...])
    for h in halves:
        logits = h["logits"]
        lane = lax.broadcasted_iota(jnp.int32, logits.shape, 1).astype(F32)
        vals, idxs = [], []
        for _ in range(TOP_K):
            m = jnp.max(logits, axis=-1, keepdims=True)
            sel = jnp.min(jnp.where(logits == m, lane, float(LANES)), axis=-1, keepdims=True)
            vals.append(m)
            idxs.append(sel)
            logits = jnp.where(lane == sel, -jnp.inf, logits)
        exps = [jnp.exp(v_ - vals[0]) for v_ in vals]
        denom = exps[0] + exps[1] + exps[2] + exps[3]
        idx_out = jnp.zeros_like(lane)
        prob_out = jnp.zeros_like(lane)
        for k_ in range(TOP_K):
            idx_out = jnp.where(lane == float(k_), idxs[k_], idx_out)
            prob_out = jnp.where(lane == float(k_), exps[k_] / denom, prob_out)
        idx_ref[:, h["rows"]] = jnp.transpose(idx_out)[0:8, :].astype(jnp.int32)
        prob_ref[0, h["rows"], :] = prob_out


def _merge_call(x, a1, s1, gt1, a2, s2, o_f, o_b, w_gates, norm_w, w_up_a, w_up_b, w_o, g_post, w_r, b_r):
    b_, l_, d = x.shape
    tm = LATENT_TILE
    tile = lambda b, t: (b, t, 0)
    per_b = lambda b, t: (b, 0, 0)
    const2 = lambda b, t: (0, 0)
    return pl.pallas_call(
        _merge_kernel,
        grid=(b_, l_ // tm),
        in_specs=[
            pl.BlockSpec((1, tm, d), tile),
            pl.BlockSpec((1, 1, d), per_b), pl.BlockSpec((1, 1, d), per_b), pl.BlockSpec((1, 1, d), per_b),
            pl.BlockSpec((1, 1, d), per_b), pl.BlockSpec((1, 1, d), per_b),
            pl.BlockSpec((1, tm, 1024), tile), pl.BlockSpec((1, tm, 1024), tile),
            pl.BlockSpec(w_gates.shape, const2),
            pl.BlockSpec(norm_w.shape, const2),
            pl.BlockSpec(w_up_a.shape, const2),
            pl.BlockSpec(w_up_b.shape, const2),
            pl.BlockSpec(w_o.shape, const2),
            pl.BlockSpec(g_post.shape, const2),
            pl.BlockSpec(w_r.shape, lambda b, t: (0, 0, 0)),
            pl.BlockSpec(b_r.shape, const2),
        ],
        out_specs=[
            pl.BlockSpec((1, tm, d), tile),
            pl.BlockSpec((tm * (d // LANES), LANES), lambda b, t: (b * (l_ // tm) + t, 0)),
            pl.BlockSpec((8, tm), lambda b, t: (0, b * (l_ // tm) + t)), pl.BlockSpec((1, tm, LANES), tile),
        ],
        out_shape=[
            jax.ShapeDtypeStruct((b_, l_, d), F32), jax.ShapeDtypeStruct((b_ * l_ * (d // LANES), LANES), F32),
            jax.ShapeDtypeStruct((8, b_ * l_), jnp.int32), jax.ShapeDtypeStruct((b_, l_, LANES), F32),
        ],
        compiler_params=pltpu.CompilerParams(
            dimension_semantics=("arbitrary", "arbitrary"), vmem_limit_bytes=VMEM_LIMIT),
    )(x, a1, s1, gt1, a2, s2, o_f, o_b, w_gates, norm_w, w_up_a, w_up_b, w_o, g_post, w_r, b_r)


def _moe_kernel(be_ref, nv_ref, tok_ref, tokn_ref, dst_ref, h2_hbm,
                wg_ref, bg_ref, wu_ref, bu_ref, wd_ref, bd_ref, out_hbm,
                xbuf, ybuf, wgb, wub, wdb, gsem, ssem):
    i = pl.program_id(0)
    n = pl.num_programs(0)
    slot = i % 2
    rows = MOE_BLOCK
    nslab = xbuf.shape[0] // (2 * rows)
    blk = rows * nslab

    def slab(ref, start):
        return ref.at[pl.ds(pl.multiple_of(start, nslab), nslab), :]

    def for_slot(fn):
        for s_static in range(2):
            @pl.when(slot == s_static)
            def _():
                fn(s_static)

    def issue_gather(idx_ref, s):
        for r in range(rows):
            pltpu.make_async_copy(slab(h2_hbm, idx_ref[0, 0, r]), xbuf.at[pl.ds(s * blk + r * nslab, nslab), :],
                                  gsem.at[s]).start(priority=r % 2)

    def wait_gather(s):
        pltpu.make_async_copy(h2_hbm.at[pl.ds(0, blk), :], xbuf.at[pl.ds(pl.multiple_of(s * blk, blk), blk), :],
                              gsem.at[s]).wait()

    def issue_scatter(s):
        for r in range(rows):
            pltpu.make_async_copy(ybuf.at[pl.ds(s * blk + r * nslab, nslab), :], slab(out_hbm, dst_ref[0, 0, r]),
                                  ssem.at[s]).start(priority=r % 2)

    def wait_scatter(s):
        pltpu.make_async_copy(ybuf.at[pl.ds(pl.multiple_of(s * blk, blk), blk), :], out_hbm.at[pl.ds(0, blk), :],
                              ssem.at[s]).wait()

    @pl.when(jnp.logical_and(i == 0, nv_ref[0] > 0))
    def _():
        issue_gather(tok_ref, 0)

    nxt = jnp.minimum(i + 1, n - 1)

    @pl.when(jnp.logical_and(i + 1 < n, nv_ref[nxt] > 0))
    def _():
        for_slot(lambda s: issue_gather(tokn_ref, 1 - s))

    prev = jnp.maximum(i - 1, 0)

    @pl.when(jnp.logical_or(i == 0, be_ref[i] != be_ref[prev]))
    def _():
        wgb[...] = wg_ref[0].astype(BF16)
        wub[...] = wu_ref[0].astype(BF16)
        wdb[...] = wd_ref[0].astype(BF16)

    @pl.when(jnp.logical_and(i >= 2, nv_ref[jnp.maximum(i - 2, 0)] > 0))
    def _():
        wait_scatter(slot)

    @pl.when(nv_ref[i] > 0)
    def _():
        wait_gather(slot)
        base = slot * blk
        xb = jnp.concatenate([xbuf[pl.ds(base + s, rows, stride=nslab), :] for s in range(nslab)],
                             axis=-1).astype(BF16)
        g = jnp.minimum(_dot(xb, wgb[...]) + bg_ref[0], SWIGLU_LIMIT)
        u = jnp.clip(_dot(xb, wub[...]) + bu_ref[0], -SWIGLU_LIMIT, SWIGLU_LIMIT)
        act = g * _sigmoid(SWIGLU_ALPHA * g) * (u + 1.0)
        y = _dot(act.astype(BF16), wdb[...]) + bd_ref[0]
        for s in range(nslab):
            ybuf[pl.ds(base + s, rows, stride=nslab), :] = y[:, s * LANES:(s + 1) * LANES]
        for_slot(issue_scatter)

    @pl.when(nv_ref[i] == 0)
    def _():
        ybuf[pl.ds(pl.multiple_of(slot * blk, blk), blk), :] = jnp.zeros((blk, LANES), F32)
        fill = pltpu.make_async_copy(ybuf.at[pl.ds(pl.multiple_of(slot * blk, blk), blk), :],
                                     out_hbm.at[pl.ds(pl.multiple_of(dst_ref[0, 0, 0], nslab), blk), :], ssem.at[slot])
        fill.start()
        fill.wait()

    @pl.when(i == n - 1)
    def _():
        @pl.when(jnp.logical_and(i >= 1, nv_ref[prev] > 0))
        def _():
            wait_scatter(1 - slot)

        @pl.when(nv_ref[i] > 0)
        def _():
            wait_scatter(slot)


def _moe_call(block_expert, n_valid, tok, dst, h2, n_out_rows, w_gate, b_gate, w_up, b_up, w_down, b_down):
    e_, d, f = w_gate.shape
    nslab = d // LANES
    nblk = tok.shape[0]
    rows = MOE_BLOCK

    def blk(i, be, nv):
        return (i, 0, 0)

    def blk_next(i, be, nv):
        return (jnp.minimum(i + 1, nblk - 1), 0, 0)

    def expert(i, be, nv):
        return (be[i], 0, 0)

    smem_rows = functools.partial(pl.BlockSpec, (1, 1, rows), memory_space=pltpu.SMEM)
    grid_spec = pltpu.PrefetchScalarGridSpec(
        num_scalar_prefetch=2,
        grid=(nblk,),
        in_specs=[
            smem_rows(blk), smem_rows(blk_next), smem_rows(blk),
            pl.BlockSpec(memory_space=pl.ANY),
            pl.BlockSpec((1, d, f), expert), pl.BlockSpec((1, 1, f), expert),
            pl.BlockSpec((1, d, f), expert), pl.BlockSpec((1, 1, f), expert),
            pl.BlockSpec((1, f, d), expert), pl.BlockSpec((1, 1, d), expert),
        ],
        out_specs=pl.BlockSpec(memory_space=pl.ANY),
        scratch_shapes=[
            pltpu.VMEM((2 * rows * nslab, LANES), F32), pltpu.VMEM((2 * rows * nslab, LANES), F32),
            pltpu.VMEM((d, f), BF16), pltpu.VMEM((d, f), BF16), pltpu.VMEM((f, d), BF16),
            pltpu.SemaphoreType.DMA((2,)), pltpu.SemaphoreType.DMA((2,)),
        ],
    )
    return pl.pallas_call(
        _moe_kernel,
        grid_spec=grid_spec,
        out_shape=jax.ShapeDtypeStruct((n_out_rows * nslab, LANES), F32),
        compiler_params=pltpu.CompilerParams(
            dimension_semantics=("arbitrary",), vmem_limit_bytes=VMEM_LIMIT),
    )(block_expert, n_valid, tok, tok, dst, h2,
      w_gate, b_gate.reshape(e_, 1, f), w_up, b_up.reshape(e_, 1, f), w_down, b_down.reshape(e_, 1, d))


def _dispatch_plan(top_idx, n_experts, nslab):
    t_ = top_idx.shape[0]
    n_pairs = t_ * TOP_K
    flat_e = top_idx.reshape(-1)
    pair_bits = max(n_pairs - 1, 1).bit_length()
    keys = jnp.sort(flat_e * (1 << pair_bits) + jnp.arange(n_pairs, dtype=jnp.int32))
    order = keys & ((1 << pair_bits) - 1)
    counts = jnp.sum((flat_e[:, None] == jnp.arange(n_experts, dtype=jnp.int32)[None, :]).astype(jnp.int32), axis=0)
    nblk_e = (counts + MOE_BLOCK - 1) // MOE_BLOCK
    blk_end = jnp.cumsum(nblk_e)
    blk_start = blk_end - nblk_e
    start = jnp.cumsum(counts) - counts
    nblk = n_pairs // MOE_BLOCK + n_experts
    b = jnp.arange(nblk, dtype=jnp.int32)
    be = jnp.minimum(jnp.sum((b[:, None] >= blk_end[None, :]).astype(jnp.int32), axis=1), n_experts - 1)
    first_row = (b - blk_start[be]) * MOE_BLOCK
    n_valid = jnp.clip(counts[be] - first_row, 0, MOE_BLOCK).astype(jnp.int32)
    r = jnp.arange(MOE_BLOCK, dtype=jnp.int32)[None, :]
    valid = r < n_valid[:, None]
    src = jnp.clip(start[be][:, None] + first_row[:, None] + r, 0, n_pairs - 1)
    pair = jnp.where(valid, order[src], 0).astype(jnp.int32)
    tok = pair // TOP_K
    pad_before = jnp.cumsum(MOE_BLOCK - n_valid) - (MOE_BLOCK - n_valid)
    spare = pad_before[:, None] + r - n_valid[:, None]
    dst = jnp.where(valid, (pair % TOP_K) * t_ + tok, n_pairs + spare).astype(jnp.int32)
    n_out_rows = n_pairs + n_experts * MOE_BLOCK
    return (be, n_valid, (tok * nslab).reshape(nblk, 1, MOE_BLOCK), (dst * nslab).reshape(nblk, 1, MOE_BLOCK),
            n_out_rows)


def _final_kernel(x1_ref, y0_ref, y1_ref, y2_ref, y3_ref, p_ref, gt2_ref, gpost_ref, o_ref):
    p = p_ref[0]
    tm, d = x1_ref.shape[1:]
    nslab = d // LANES
    parts = []
    for s in range(nslab):
        acc = p[:, 0:1] * y0_ref[pl.ds(s, tm, stride=nslab), :]
        for k_, y_ref in ((1, y1_ref), (2, y2_ref), (3, y3_ref)):
            acc = acc + p[:, k_:k_ + 1] * y_ref[pl.ds(s, tm, stride=nslab), :]
        parts.append(acc)
    ffn = jnp.concatenate(parts, axis=-1)
    o_ref[0] = x1_ref[0] + gt2_ref[0] * (_rms(ffn) * gpost_ref[...])


def _final_call(x1, y, probs, gt2, g_post):
    b_, l_, d = x1.shape
    tm = LATENT_TILE
    nslab = d // LANES
    tiles_per_k = b_ * l_ // tm
    tile = lambda b, t: (b, t, 0)

    def y_spec(k_):
        return pl.BlockSpec((tm * nslab, LANES), lambda b, t: (k_ * tiles_per_k + b * (l_ // tm) + t, 0))

    return pl.pallas_call(
        _final_kernel,
        grid=(b_, l_ // tm),
        in_specs=[
            pl.BlockSpec((1, tm, d), tile),
            y_spec(0), y_spec(1), y_spec(2), y_spec(3),
            pl.BlockSpec((1, tm, LANES), tile),
            pl.BlockSpec((1, 1, d), lambda b, t: (b, 0, 0)),
            pl.BlockSpec(g_post.shape, lambda b, t: (0, 0)),
        ],
        out_specs=pl.BlockSpec((1, tm, d), tile),
        out_shape=jax.ShapeDtypeStruct((b_, l_, d), F32),
        compiler_params=pltpu.CompilerParams(
            dimension_semantics=("arbitrary", "arbitrary"), vmem_limit_bytes=VMEM_LIMIT),
    )(x1, y, y, y, y, probs, gt2, g_post)


def kernel(x, c, ctx, c_ctx, w_ada, b_ada, g_pre_mix, g_post_mix, g_pre_ffn, g_post_ffn, w_in, hgrn_lb, hgrn_norm_w, gla_gk_w2, gla_gk_b, gla_norm_w, w_up_a, w_up_b, w_o, w_router, b_router, w_gate, b_gate, w_up, b_up, w_down, b_down):
    b_, l_, d = x.shape
    lc = ctx.shape[1]
    n_experts = w_router.shape[-1]
    layer = 0
    hw = HGRN_HEADS * HGRN_DK
    kw = GLA_HEADS * GLA_DK
    vw = GLA_HEADS * GLA_DV
    rk = GLA_GATE_RANK

    rows = -(-(b_ + 1) // 8) * 8
    cc = jnp.concatenate([c, c_ctx[None, :], jnp.zeros((rows - b_ - 1, d), F32)], axis=0)
    mod = _ada_call(cc, w_ada[layer], b_ada[layer])[:b_ + 1]
    sh1, sc1, gt1, sh2, sc2, gt2 = [m.reshape(b_ + 1, 1, d) for m in jnp.split(mod, 6, axis=-1)]
    a1 = g_pre_mix[layer] * (1.0 + sc1)
    a2 = g_pre_ffn[layer] * (1.0 + sc2)

    w = w_in[layer]
    o0 = 0
    cols = {}
    for name, size in (("qa", hw), ("zf", hw), ("zb", hw), ("ia", hw), ("oga", hw), ("qb", kw), ("kb", kw),
                       ("vb", vw), ("rf", rk), ("rb", rk), ("ogb", vw), ("mga", d), ("mgb", d)):
        cols[name] = w[:, o0:o0 + size]
        o0 += size
    w_scan = jnp.concatenate([
        cols["qa"], cols["zf"], cols["zb"], cols["ia"],
        cols["qb"] * (GLA_DK ** -0.5), cols["kb"], cols["vb"]], axis=1).astype(BF16)
    w_rank = jnp.pad(jnp.concatenate([cols["rf"], cols["rb"]], axis=1), ((0, 0), (0, LANES - 2 * rk))).astype(BF16)
    w2 = jnp.zeros((LANES, 2 * kw), F32)
    w2 = w2.at[0:rk, 0:kw].set(gla_gk_w2[layer, 0])
    w2 = w2.at[rk:2 * rk, kw:].set(gla_gk_w2[layer, 1])
    w2 = w2.astype(BF16)
    gkb = jnp.concatenate([gla_gk_b[layer, 0], gla_gk_b[layer, 1]])[None, :]
    lb = jnp.cumsum(jax.nn.softmax(hgrn_lb.astype(F32), axis=0), axis=0)[layer]
    w_gates = jnp.concatenate([cols["oga"], cols["ogb"], cols["mga"], cols["mgb"]], axis=1).astype(BF16)
    norm_w = jnp.concatenate([jnp.tile(hgrn_norm_w[layer], HGRN_HEADS), jnp.tile(gla_norm_w[layer], GLA_HEADS)])[None, :]
    w_r = jnp.stack(_split_bf16(jnp.pad(w_router[layer], ((0, 0), (0, LANES - n_experts)))))
    b_r = jnp.pad(b_router[layer], (0, LANES - n_experts), constant_values=-1e30)[None, :]

    q, v, akf, akb, bk, gf, gb = _inproj_call(x, ctx, a1, sh1, w_scan, w_rank, w2, gkb, lb)
    o_f, o_b = _scan_call(q, v, akf, akb, bk, gf, gb, lc)
    x1, h2, top_idx, probs = _merge_call(
        x, a1, sh1, gt1, a2, sh2, o_f, o_b, w_gates, norm_w,
        w_up_a[layer].astype(BF16), w_up_b[layer].astype(BF16), w_o[layer].astype(BF16),
        g_post_mix[layer][None, :], w_r, b_r)

    t_ = b_ * l_
    be, n_valid, tok, dst, n_out_rows = _dispatch_plan(top_idx[:TOP_K].T, n_experts, d // LANES)
    y = _moe_call(be, n_valid, tok, dst, h2, n_out_rows,
                  w_gate[layer], b_gate[layer], w_up[layer], b_up[layer], w_down[layer], b_down[layer])
    return _final_call(x1, y, probs, gt2, g_post_ffn[layer][None, :])
```

```python
import functools

import jax
import jax.numpy as jnp
from jax import lax
from jax.experimental import pallas as pl
from jax.experimental.pallas import tpu as pltpu

F32 = jnp.float32
BF16 = jnp.bfloat16

NORM_EPS = 1e-6
HGRN_HEADS = 4
HGRN_DK = 128
GLA_HEADS = 4
GLA_DK = 64
GLA_DV = 128
GLA_GATE_RANK = 16
GLA_GATE_NORMALIZER = 16.0
N_HEADS = HGRN_HEADS + GLA_HEADS
HEAD_W = 128
TOP_K = 4
SWIGLU_LIMIT = 7.0
SWIGLU_ALPHA = 1.702
MOE_BLOCK = 256
SCAN_CHUNK = 64
SCAN_BLOCK = 256
LOG2E = 1.4426950408889634
TOKEN_TILE = 256
LATENT_TILE = 512
LANES = 128
VMEM_LIMIT = 56 * 1024 * 1024


def _dot(a, b):
    return jnp.dot(a, b, preferred_element_type=F32)


def _dot_nt(a, b):
    return lax.dot_general(a, b, (((1,), (1,)), ((), ())), preferred_element_type=F32)


def _dot_tn(a, b):
    return lax.dot_general(a, b, (((0,), (0,)), ((), ())), preferred_element_type=F32)


def _split_bf16(a):
    hi = a.astype(BF16)
    lo = (a - hi.astype(F32)).astype(BF16)
    return hi, lo


def _dot3(a, b):
    ah, al = _split_bf16(a)
    bh, bl = _split_bf16(b)
    return _dot(ah, bh) + _dot(al, bh) + _dot(ah, bl)


def _rms(x):
    return x * lax.rsqrt(jnp.mean(x * x, axis=-1, keepdims=True) + NORM_EPS)


def _sigmoid(x):
    return 1.0 / (1.0 + jnp.exp(-x))


def _ada_kernel(c_ref, w_ref, b_ref, o_ref):
    c = c_ref[...]
    o_ref[...] = _dot3(c * _sigmoid(c), w_ref[...]) + b_ref[...]


def _ada_call(cc, w_ada, b_ada):
    rows, d = cc.shape
    n = w_ada.shape[1]
    tn = 1536
    return pl.pallas_call(
        _ada_kernel,
        grid=(n // tn,),
        in_specs=[
            pl.BlockSpec((rows, d), lambda j: (0, 0)),
            pl.BlockSpec((d, tn), lambda j: (0, j)),
            pl.BlockSpec((1, tn), lambda j: (0, j)),
        ],
        out_specs=pl.BlockSpec((rows, tn), lambda j: (0, j)),
        out_shape=jax.ShapeDtypeStruct((rows, n), F32),
        compiler_params=pltpu.CompilerParams(
            dimension_semantics=("arbitrary",), vmem_limit_bytes=VMEM_LIMIT),
    )(cc, w_ada, b_ada.reshape(1, n))


def _inproj_kernel(n_ctx_tiles, x_ref, ctx_ref, a_ref, s_ref, actx_ref, sctx_ref, w_ref, wr_ref, w2_ref, gkb_ref,
                   lb_ref, q_ref, v_ref, akf_ref, akb_ref, bk_ref, gf_ref, gb_ref):
    t = pl.program_id(1)
    is_ctx = t < n_ctx_tiles
    aw = HGRN_HEADS * HGRN_DK
    bw = GLA_HEADS * GLA_DK
    parts = [dict(i=i) for i in range(x_ref.shape[0])]
    for p in parts:
        i = p["i"]
        xt = jnp.where(is_ctx, ctx_ref[i], x_ref[i])
        a = jnp.where(is_ctx, actx_ref[0], a_ref[i])
        s = jnp.where(is_ctx, sctx_ref[0], s_ref[i])
        p["hb"] = (_rms(xt) * a + s).astype(BF16)
    for p in parts:
        hb = p["hb"]

        def seg(lo, hi):
            return _dot(hb, w_ref[:, lo:hi])

        p["z_f"], p["z_b"] = seg(aw, 2 * aw), seg(2 * aw, 3 * aw)
        r = _dot(hb, wr_ref[...]).astype(BF16)
        p["pre"] = _dot(r, w2_ref[...]) + gkb_ref[...]
        p["q_a"], p["v_a"] = seg(0, aw), seg(3 * aw, 4 * aw)
        p["q_b"], p["k_b"] = seg(4 * aw, 4 * aw + bw), seg(4 * aw + bw, 4 * aw + 2 * bw)
        p["v_b"] = seg(4 * aw + 2 * bw, 5 * aw + 2 * bw)
    for p in parts:
        i = p["i"]
        for j, z, k_ref, g_ref in ((1, p["z_f"], akf_ref, gf_ref), (2, p["z_b"], akb_ref, gb_ref)):
            lb = lb_ref[j - 1:j, :]
            sg = _sigmoid(z)
            g_ref[i, :, 0:aw] = jnp.log2(lb + (1.0 - lb) * sg)
            k_ref[i] = ((1.0 - lb) * (1.0 - sg)).astype(BF16)
        pre = p["pre"]
        ls = (jnp.minimum(pre, 0.0) - jnp.log(1.0 + jnp.exp(-jnp.abs(pre)))) * (LOG2E / GLA_GATE_NORMALIZER)
        gf_ref[i, :, aw:aw + bw] = ls[:, 0:bw]
        gb_ref[i, :, aw:aw + bw] = ls[:, bw:2 * bw]
        q_ref[i, :, 0:aw] = p["q_a"].astype(BF16)
        q_ref[i, :, aw:aw + bw] = p["q_b"].astype(BF16)
        v_ref[i, :, 0:aw] = p["v_a"].astype(BF16)
        v_ref[i, :, aw:2 * aw] = p["v_b"].astype(BF16)
        bk_ref[i] = p["k_b"].astype(BF16)


def _inproj_call(x, ctx, mod_a, mod_s, w_scan, w_rank, w2, gkb, lb):
    b_, l_, d = x.shape
    lc = ctx.shape[1]
    tm = TOKEN_TILE
    nb = max(k for k in (4, 2, 1) if b_ % k == 0)
    nct, nlt = lc // tm, l_ // tm
    lt = lc + l_

    def x_map(b, t):
        return (b, jnp.maximum(t - nct, 0), 0)

    def ctx_map(b, t):
        return (b, jnp.minimum(t, nct - 1), 0)

    per_b = lambda b, t: (b, 0, 0)
    ctx_row = lambda b, t: (b_, 0, 0)
    const2 = lambda b, t: (0, 0)
    out_map = lambda b, t: (b, t, 0)
    aw, bw, vw = HGRN_HEADS * HGRN_DK, GLA_HEADS * GLA_DK, GLA_HEADS * GLA_DV
    qk_w, v_w = aw + bw, aw + vw

    def out(width, dtype):
        return jax.ShapeDtypeStruct((b_, lt, width), dtype), pl.BlockSpec((nb, tm, width), out_map)

    outs = [out(qk_w, BF16), out(v_w, BF16), out(aw, BF16), out(aw, BF16), out(bw, BF16), out(qk_w, F32), out(qk_w, F32)]
    return pl.pallas_call(
        functools.partial(_inproj_kernel, nct),
        grid=(b_ // nb, nct + nlt),
        in_specs=[
            pl.BlockSpec((nb, tm, d), x_map),
            pl.BlockSpec((nb, tm, d), ctx_map),
            pl.BlockSpec((nb, 1, d), per_b),
            pl.BlockSpec((nb, 1, d), per_b),
            pl.BlockSpec((1, 1, d), ctx_row),
            pl.BlockSpec((1, 1, d), ctx_row),
            pl.BlockSpec(w_scan.shape, const2),
            pl.BlockSpec(w_rank.shape, const2),
            pl.BlockSpec(w2.shape, const2),
            pl.BlockSpec(gkb.shape, const2),
            pl.BlockSpec(lb.shape, const2),
        ],
        out_specs=[o[1] for o in outs],
        out_shape=[o[0] for o in outs],
        compiler_params=pltpu.CompilerParams(
            dimension_semantics=("arbitrary", "arbitrary"), vmem_limit_bytes=VMEM_LIMIT),
    )(x, ctx, mod_a, mod_s, mod_a, mod_s, w_scan, w_rank, w2, gkb, lb)


def _causal(forward):
    row = lax.broadcasted_iota(jnp.int32, (SCAN_CHUNK, SCAN_CHUNK), 0)
    col = lax.broadcasted_iota(jnp.int32, (SCAN_CHUNK, SCAN_CHUNK), 1)
    return (row >= col) if forward else (col >= row)


def _scan_block(with_out, qf_ref, vf_ref, akf_ref, bkf_ref, gf_ref, qb_ref, vb_ref, akb_ref, bkb_ref, gb_ref,
                of_ref, ob_ref, sf_ref, sb_ref):
    c_ = SCAN_CHUNK
    ncb = qf_ref.shape[1] // c_
    aw = HGRN_HEADS * HGRN_DK
    n_pairs = GLA_HEADS // 2
    dirs = []
    for i in range(qf_ref.shape[0]):
        dirs.append((True,) + tuple(r.at[i] for r in (qf_ref, vf_ref, akf_ref, bkf_ref, gf_ref, of_ref, sf_ref)))
        dirs.append((False,) + tuple(r.at[i] for r in (qb_ref, vb_ref, akb_ref, bkb_ref, gb_ref, ob_ref, sb_ref)))
    lane = lax.broadcasted_iota(jnp.int32, (c_, HEAD_W), 1)
    low = lane < GLA_DK
    pr = lax.broadcasted_iota(jnp.int32, (c_, 2 * c_), 0)
    pc = lax.broadcasted_iota(jnp.int32, (c_, 2 * c_), 1) % c_
    pair_causal = {True: pr >= pc, False: pc >= pr}
    srow = lax.broadcasted_iota(jnp.int32, (2 * HEAD_W, HEAD_W), 0) < HEAD_W
    scol = lax.broadcasted_iota(jnp.int32, (2 * HEAD_W, HEAD_W), 1) < GLA_DK
    pair_state = srow == scol
    zero_v = jnp.zeros((c_, HEAD_W), BF16)
    cums = {}
    for di, (fw, _, _, _, _, g_ref, _, _) in enumerate(dirs):
        tri = jnp.where(_causal(fw), 1.0, 0.0).astype(BF16)
        for c in range(ncb):
            g_hi, g_lo = _split_bf16(g_ref[c * c_:(c + 1) * c_, :])
            cums[di, c] = _dot(tri, g_hi) + _dot(tri, g_lo)
    ops = []
    for di, (fw, q_ref, v_ref, ak_ref, bk_ref, _, _, _) in enumerate(dirs):
        end_row, mid_row = (c_ - 1, c_ // 2 - 1) if fw else (0, c_ // 2)
        for c in range(ncb):
            rows = slice(c * c_, (c + 1) * c_)
            cum = cums[di, c]
            total = cum[end_row:end_row + 1, :]
            mid = cum[mid_row:mid_row + 1, :]
            e_mid = jnp.exp2(mid)
            e_rest = jnp.exp2(total - mid)
            e_tot = jnp.exp2(total)
            for u in range(HGRN_HEADS + n_pairs):
                pair = u >= HGRN_HEADS
                sl = slice(u * HEAD_W, (u + 1) * HEAD_W)
                k_src = bk_ref[rows, (u - HGRN_HEADS) * HEAD_W:(u - HGRN_HEADS + 1) * HEAD_W] if pair \
                    else ak_ref[rows, sl]
                qs = q_ref[rows, sl].astype(F32) * jnp.exp2(cum[:, sl] - mid[:, sl])
                ks = k_src.astype(F32) * jnp.exp2(mid[:, sl] - cum[:, sl])
                ksb = ks.astype(BF16)
                if pair:
                    vl = slice(aw + (u - HGRN_HEADS) * 2 * HEAD_W, aw + (u - HGRN_HEADS + 1) * 2 * HEAD_W)
                    v = v_ref[rows, vl]
                    k_rhs = jnp.concatenate([jnp.where(low, ksb, 0), jnp.where(low, 0, ksb)], axis=0)
                    v_rhs = jnp.concatenate([jnp.concatenate([v[:, :HEAD_W], zero_v], axis=1),
                                             jnp.concatenate([zero_v, v[:, HEAD_W:]], axis=1)], axis=0)
                    st_rows = slice(aw + (u - HGRN_HEADS) * 2 * HEAD_W, aw + (u - HGRN_HEADS + 1) * 2 * HEAD_W)
                else:
                    vl = sl
                    v = v_ref[rows, vl]
                    k_rhs, v_rhs = ksb, v
                    st_rows = sl
                ops.append(dict(di=di, fw=fw, c=c, pair=pair, rows=rows, vl=vl, st_rows=st_rows, v=v,
                                k_rhs=k_rhs, v_rhs=v_rhs, qs=qs.astype(BF16), qd=(qs * e_mid[:, sl]).astype(BF16),
                                kd=(ks * e_rest[:, sl]).astype(BF16), e_tot=e_tot[:, sl]))
    if with_out:
        for d in ops:
            d["sc"] = _dot_nt(d["qs"], d["k_rhs"])
        for d in ops:
            mask = pair_causal[d["fw"]] if d["pair"] else _causal(d["fw"])
            d["sc"] = jnp.where(mask, d["sc"], 0.0).astype(BF16)
        for d in ops:
            d["o"] = _dot(d["sc"], d["v_rhs"])
    for d in ops:
        d["upd"] = _dot_tn(d["v"], d["kd"])
    for d in ops:
        if d["pair"]:
            d["upd"] = jnp.where(pair_state, d["upd"], 0.0)
    for di, (fw, _, _, _, _, _, o_ref, st_ref) in enumerate(dirs):
        sts = {}
        for c in (range(ncb) if fw else range(ncb - 1, -1, -1)):
            for d in ops:
                if d["di"] != di or d["c"] != c:
                    continue
                key = d["st_rows"].start
                if key not in sts:
                    sts[key] = st_ref[d["st_rows"], :]
                if with_out:
                    o = d["o"] + _dot_nt(d["qd"], sts[key].astype(BF16))
                    o_ref[d["rows"], d["vl"]] = o.astype(o_ref.dtype)
                sts[key] = sts[key] * d["e_tot"] + d["upd"]
        for d in ops:
            if d["di"] == di and d["c"] == 0:
                st_ref[d["st_rows"], :] = sts[d["st_rows"].start]


def _scan_kernel(n_ctx_blocks, *refs):
    s = pl.program_id(1)

    @pl.when(s == 0)
    def _():
        refs[-2][...] = jnp.zeros_like(refs[-2])
        refs[-1][...] = jnp.zeros_like(refs[-1])

    @pl.when(s < n_ctx_blocks)
    def _():
        _scan_block(False, *refs)

    @pl.when(s >= n_ctx_blocks)
    def _():
        _scan_block(True, *refs)


def _scan_call(q, v, akf, akb, bk, gf, gb, lc):
    b_, lt, _ = q.shape
    c_ = SCAN_BLOCK
    ncc = lc // c_
    nlc = (lt - lc) // c_
    n = ncc + nlc

    def fwd(b, s):
        return (b, s, 0)

    def bwd(b, s):
        return (b, jnp.where(s < ncc, ncc - 1 - s, n + ncc - 1 - s), 0)

    def out_fwd(b, s):
        return (b, jnp.maximum(s - ncc, 0), 0)

    def out_bwd(b, s):
        return (b, jnp.where(s < ncc, nlc - 1, n - 1 - s), 0)

    nb = 2 if b_ % 2 == 0 else 1

    def specs(index_map):
        return [pl.BlockSpec((nb, c_, a.shape[-1]), index_map) for a in (q, v, akf, bk, gf)]

    v_w = v.shape[-1]
    out = jax.ShapeDtypeStruct((b_, lt - lc, v_w), BF16)
    state = pltpu.VMEM((nb, v_w, HEAD_W), F32)
    return pl.pallas_call(
        functools.partial(_scan_kernel, ncc),
        grid=(b_ // nb, n),
        in_specs=specs(fwd) + specs(bwd),
        out_specs=[pl.BlockSpec((nb, c_, v_w), out_fwd), pl.BlockSpec((nb, c_, v_w), out_bwd)],
        out_shape=[out, out],
        scratch_shapes=[state, state],
        compiler_params=pltpu.CompilerParams(
            dimension_semantics=("arbitrary", "arbitrary"), vmem_limit_bytes=VMEM_LIMIT),
    )(q, v, akf, bk, gf, q, v, akb, bk, gb)


def _merge_kernel(x_ref, a1_ref, s1_ref, gt1_ref, a2_ref, s2_ref, of_ref, ob_ref, wg_ref, nw_ref,
                  wua_ref, wub_ref, wo_ref, gpost_ref, wr_ref, br_ref,
                  x1_ref, h2_ref, idx_ref, prob_ref):
    tm, d = x_ref.shape[1:]
    hr = tm // 2
    nslab = d // LANES
    halves = [dict(i=i, rows=slice(i * hr, (i + 1) * hr)) for i in range(2)]
    for h in halves:
        h["x"] = x_ref[0, h["rows"], :]
        h["hb"] = (_rms(h["x"]) * a1_ref[0] + s1_ref[0]).astype(BF16)
    for h in halves:
        h["og"] = _dot(h["hb"], wg_ref[:, 0:1024])
        h["mg_a"] = _dot(h["hb"], wg_ref[:, 1024:2048])
        h["mg_b"] = _dot(h["hb"], wg_ref[:, 2048:3072])
    for h in halves:
        o = of_ref[0, h["rows"], :].astype(F32) + ob_ref[0, h["rows"], :].astype(F32)
        heads = [_rms(o[:, j * HEAD_W:(j + 1) * HEAD_W]) for j in range(N_HEADS)]
        og = h["og"]
        h["r"] = (jnp.concatenate(heads, axis=-1) * nw_ref[...] * (og * _sigmoid(og))).astype(BF16)
    for h in halves:
        h["y_a"] = _dot(h["r"][:, 0:512], wua_ref[...])
        h["y_b"] = _dot(h["r"][:, 512:1024], wub_ref[...])
    for h in halves:
        h["m"] = (_sigmoid(h["mg_a"]) * h["y_a"] + _sigmoid(h["mg_b"]) * h["y_b"]).astype(BF16)
    for h in halves:
        h["mix"] = _dot(h["m"], wo_ref[...])
    for h in halves:
        x1 = h["x"] + gt1_ref[0] * (_rms(h["mix"]) * gpost_ref[...])
        x1_ref[0, h["rows"], :] = x1
        h2 = _rms(x1) * a2_ref[0] + s2_ref[0]
        for s in range(nslab):
            h2_ref[pl.ds(h["i"] * hr * nslab + s, hr, stride=nslab), :] = h2[:, s * LANES:(s + 1) * LANES]
        h["h_hi"], h["h_lo"] = _split_bf16(h2)
    for h in halves:
        h["logits"] = (_dot(h["h_hi"], wr_ref[0]) + _dot(h["h_lo"], wr_ref[0]) + _dot(h["h_hi"], wr_ref[1])
                       + br_ref[...])
    for h in halves:
        logits = h["logits"]
        lane = lax.broadcasted_iota(jnp.int32, logits.shape, 1).astype(F32)
        vals, idxs = [], []
        for _ in range(TOP_K):
            m = jnp.max(logits, axis=-1, keepdims=True)
            sel = jnp.min(jnp.where(logits == m, lane, float(LANES)), axis=-1, keepdims=True)
            vals.append(m)
            idxs.append(sel)
            logits = jnp.where(lane == sel, -jnp.inf, logits)
        exps = [jnp.exp(v_ - vals[0]) for v_ in vals]
        denom = exps[0] + exps[1] + exps[2] + exps[3]
        idx_out = jnp.zeros_like(lane)
        prob_out = jnp.zeros_like(lane)
        for k_ in range(TOP_K):
            idx_out = jnp.where(lane == float(k_), idxs[k_], idx_out)
            prob_out = jnp.where(lane == float(k_), exps[k_] / denom, prob_out)
        idx_ref[0, h["rows"], :] = idx_out.astype(jnp.int32)
        prob_ref[0, h["rows"], :] = prob_out


def _merge_call(x, a1, s1, gt1, a2, s2, o_f, o_b, w_gates, norm_w, w_up_a, w_up_b, w_o, g_post, w_r, b_r):
    b_, l_, d = x.shape
    tm = LATENT_TILE
    tile = lambda b, t: (b, t, 0)
    per_b = lambda b, t: (b, 0, 0)
    const2 = lambda b, t: (0, 0)
    return pl.pallas_call(
        _merge_kernel,
        grid=(b_, l_ // tm),
        in_specs=[
            pl.BlockSpec((1, tm, d), tile),
            pl.BlockSpec((1, 1, d), per_b), pl.BlockSpec((1, 1, d), per_b), pl.BlockSpec((1, 1, d), per_b),
            pl.BlockSpec((1, 1, d), per_b), pl.BlockSpec((1, 1, d), per_b),
            pl.BlockSpec((1, tm, 1024), tile), pl.BlockSpec((1, tm, 1024), tile),
            pl.BlockSpec(w_gates.shape, const2),
            pl.BlockSpec(norm_w.shape, const2),
            pl.BlockSpec(w_up_a.shape, const2),
            pl.BlockSpec(w_up_b.shape, const2),
            pl.BlockSpec(w_o.shape, const2),
            pl.BlockSpec(g_post.shape, const2),
            pl.BlockSpec(w_r.shape, lambda b, t: (0, 0, 0)),
            pl.BlockSpec(b_r.shape, const2),
        ],
        out_specs=[
            pl.BlockSpec((1, tm, d), tile),
            pl.BlockSpec((tm * (d // LANES), LANES), lambda b, t: (b * (l_ // tm) + t, 0)),
            pl.BlockSpec((1, tm, LANES), tile), pl.BlockSpec((1, tm, LANES), tile),
        ],
        out_shape=[
            jax.ShapeDtypeStruct((b_, l_, d), F32), jax.ShapeDtypeStruct((b_ * l_ * (d // LANES), LANES), F32),
            jax.ShapeDtypeStruct((b_, l_, LANES), jnp.int32), jax.ShapeDtypeStruct((b_, l_, LANES), F32),
        ],
        compiler_params=pltpu.CompilerParams(
            dimension_semantics=("arbitrary", "arbitrary"), vmem_limit_bytes=VMEM_LIMIT),
    )(x, a1, s1, gt1, a2, s2, o_f, o_b, w_gates, norm_w, w_up_a, w_up_b, w_o, g_post, w_r, b_r)


def _moe_kernel(be_ref, nv_ref, tok_ref, tokn_ref, dst_ref, h2_hbm,
                wg_ref, bg_ref, wu_ref, bu_ref, wd_ref, bd_ref, out_hbm,
                xbuf, ybuf, wgb, wub, wdb, gsem, ssem):
    i = pl.program_id(0)
    n = pl.num_programs(0)
    slot = i % 2
    rows = MOE_BLOCK
    nslab = xbuf.shape[0] // (2 * rows)
    blk = rows * nslab

    def slab(ref, start):
        return ref.at[pl.ds(pl.multiple_of(start, nslab), nslab), :]

    def for_slot(fn):
        for s_static in range(2):
            @pl.when(slot == s_static)
            def _():
                fn(s_static)

    def issue_gather(idx_ref, s):
        for r in range(rows):
            pltpu.make_async_copy(slab(h2_hbm, idx_ref[0, 0, r]), xbuf.at[pl.ds(s * blk + r * nslab, nslab), :],
                                  gsem.at[s]).start(priority=r % 2)

    def wait_gather(s):
        pltpu.make_async_copy(h2_hbm.at[pl.ds(0, blk), :], xbuf.at[pl.ds(pl.multiple_of(s * blk, blk), blk), :],
                              gsem.at[s]).wait()

    def issue_scatter(s):
        for r in range(rows):
            pltpu.make_async_copy(ybuf.at[pl.ds(s * blk + r * nslab, nslab), :], slab(out_hbm, dst_ref[0, 0, r]),
                                  ssem.at[s]).start(priority=r % 2)

    def wait_scatter(s):
        pltpu.make_async_copy(ybuf.at[pl.ds(pl.multiple_of(s * blk, blk), blk), :], out_hbm.at[pl.ds(0, blk), :],
                              ssem.at[s]).wait()

    @pl.when(jnp.logical_and(i == 0, nv_ref[0] > 0))
    def _():
        issue_gather(tok_ref, 0)

    nxt = jnp.minimum(i + 1, n - 1)

    @pl.when(jnp.logical_and(i + 1 < n, nv_ref[nxt] > 0))
    def _():
        for_slot(lambda s: issue_gather(tokn_ref, 1 - s))

    prev = jnp.maximum(i - 1, 0)

    @pl.when(jnp.logical_or(i == 0, be_ref[i] != be_ref[prev]))
    def _():
        wgb[...] = wg_ref[0].astype(BF16)
        wub[...] = wu_ref[0].astype(BF16)
        wdb[...] = wd_ref[0].astype(BF16)

    @pl.when(jnp.logical_and(i >= 2, nv_ref[jnp.maximum(i - 2, 0)] > 0))
    def _():
        wait_scatter(slot)

    @pl.when(nv_ref[i] > 0)
    def _():
        wait_gather(slot)
        base = slot * blk
        xb = jnp.concatenate([xbuf[pl.ds(base + s, rows, stride=nslab), :] for s in range(nslab)],
                             axis=-1).astype(BF16)
        g = jnp.minimum(_dot(xb, wgb[...]) + bg_ref[0], SWIGLU_LIMIT)
        u = jnp.clip(_dot(xb, wub[...]) + bu_ref[0], -SWIGLU_LIMIT, SWIGLU_LIMIT)
        act = g * _sigmoid(SWIGLU_ALPHA * g) * (u + 1.0)
        y = _dot(act.astype(BF16), wdb[...]) + bd_ref[0]
        for s in range(nslab):
            ybuf[pl.ds(base + s, rows, stride=nslab), :] = y[:, s * LANES:(s + 1) * LANES]
        for_slot(issue_scatter)

    @pl.when(nv_ref[i] == 0)
    def _():
        ybuf[pl.ds(pl.multiple_of(slot * blk, blk), blk), :] = jnp.zeros((blk, LANES), F32)
        fill = pltpu.make_async_copy(ybuf.at[pl.ds(pl.multiple_of(slot * blk, blk), blk), :],
                                     out_hbm.at[pl.ds(pl.multiple_of(dst_ref[0, 0, 0], nslab), blk), :], ssem.at[slot])
        fill.start()
        fill.wait()

    @pl.when(i == n - 1)
    def _():
        @pl.when(jnp.logical_and(i >= 1, nv_ref[prev] > 0))
        def _():
            wait_scatter(1 - slot)

        @pl.when(nv_ref[i] > 0)
        def _():
            wait_scatter(slot)


def _moe_call(block_expert, n_valid, tok, dst, h2, n_out_rows, w_gate, b_gate, w_up, b_up, w_down, b_down):
    e_, d, f = w_gate.shape
    nslab = d // LANES
    nblk = tok.shape[0]
    rows = MOE_BLOCK

    def blk(i, be, nv):
        return (i, 0, 0)

    def blk_next(i, be, nv):
        return (jnp.minimum(i + 1, nblk - 1), 0, 0)

    def expert(i, be, nv):
        return (be[i], 0, 0)

    smem_rows = functools.partial(pl.BlockSpec, (1, 1, rows), memory_space=pltpu.SMEM)
    grid_spec = pltpu.PrefetchScalarGridSpec(
        num_scalar_prefetch=2,
        grid=(nblk,),
        in_specs=[
            smem_rows(blk), smem_rows(blk_next), smem_rows(blk),
            pl.BlockSpec(memory_space=pl.ANY),
            pl.BlockSpec((1, d, f), expert), pl.BlockSpec((1, 1, f), expert),
            pl.BlockSpec((1, d, f), expert), pl.BlockSpec((1, 1, f), expert),
            pl.BlockSpec((1, f, d), expert), pl.BlockSpec((1, 1, d), expert),
        ],
        out_specs=pl.BlockSpec(memory_space=pl.ANY),
        scratch_shapes=[
            pltpu.VMEM((2 * rows * nslab, LANES), F32), pltpu.VMEM((2 * rows * nslab, LANES), F32),
            pltpu.VMEM((d, f), BF16), pltpu.VMEM((d, f), BF16), pltpu.VMEM((f, d), BF16),
            pltpu.SemaphoreType.DMA((2,)), pltpu.SemaphoreType.DMA((2,)),
        ],
    )
    return pl.pallas_call(
        _moe_kernel,
        grid_spec=grid_spec,
        out_shape=jax.ShapeDtypeStruct((n_out_rows * nslab, LANES), F32),
        compiler_params=pltpu.CompilerParams(
            dimension_semantics=("arbitrary",), vmem_limit_bytes=VMEM_LIMIT),
    )(block_expert, n_valid, tok, tok, dst, h2,
      w_gate, b_gate.reshape(e_, 1, f), w_up, b_up.reshape(e_, 1, f), w_down, b_down.reshape(e_, 1, d))


def _dispatch_plan(top_idx, n_experts, nslab):
    t_ = top_idx.shape[0]
    n_pairs = t_ * TOP_K
    flat_e = top_idx.reshape(-1)
    pair_bits = max(n_pairs - 1, 1).bit_length()
    keys = jnp.sort(flat_e * (1 << pair_bits) + jnp.arange(n_pairs, dtype=jnp.int32))
    order = keys & ((1 << pair_bits) - 1)
    counts = jnp.sum((flat_e[:, None] == jnp.arange(n_experts, dtype=jnp.int32)[None, :]).astype(jnp.int32), axis=0)
    nblk_e = (counts + MOE_BLOCK - 1) // MOE_BLOCK
    blk_end = jnp.cumsum(nblk_e)
    blk_start = blk_end - nblk_e
    start = jnp.cumsum(counts) - counts
    nblk = n_pairs // MOE_BLOCK + n_experts
    b = jnp.arange(nblk, dtype=jnp.int32)
    be = jnp.minimum(jnp.sum((b[:, None] >= blk_end[None, :]).astype(jnp.int32), axis=1), n_experts - 1)
    first_row = (b - blk_start[be]) * MOE_BLOCK
    n_valid = jnp.clip(counts[be] - first_row, 0, MOE_BLOCK).astype(jnp.int32)
    r = jnp.arange(MOE_BLOCK, dtype=jnp.int32)[None, :]
    valid = r < n_valid[:, None]
    src = jnp.clip(start[be][:, None] + first_row[:, None] + r, 0, n_pairs - 1)
    pair = jnp.where(valid, order[src], 0).astype(jnp.int32)
    tok = pair // TOP_K
    spare = (jnp.cumsum((~valid).reshape(-1).astype(jnp.int32)) - 1).reshape(nblk, MOE_BLOCK)
    dst = jnp.where(valid, (pair % TOP_K) * t_ + tok, n_pairs + spare).astype(jnp.int32)
    n_out_rows = n_pairs + n_experts * MOE_BLOCK
    return (be, n_valid, (tok * nslab).reshape(nblk, 1, MOE_BLOCK), (dst * nslab).reshape(nblk, 1, MOE_BLOCK),
            n_out_rows)


def _final_kernel(x1_ref, y0_ref, y1_ref, y2_ref, y3_ref, p_ref, gt2_ref, gpost_ref, o_ref):
    p = p_ref[0]
    tm, d = x1_ref.shape[1:]
    nslab = d // LANES
    parts = []
    for s in range(nslab):
        acc = p[:, 0:1] * y0_ref[pl.ds(s, tm, stride=nslab), :]
        for k_, y_ref in ((1, y1_ref), (2, y2_ref), (3, y3_ref)):
            acc = acc + p[:, k_:k_ + 1] * y_ref[pl.ds(s, tm, stride=nslab), :]
        parts.append(acc)
    ffn = jnp.concatenate(parts, axis=-1)
    o_ref[0] = x1_ref[0] + gt2_ref[0] * (_rms(ffn) * gpost_ref[...])


def _final_call(x1, y, probs, gt2, g_post):
    b_, l_, d = x1.shape
    tm = LATENT_TILE
    nslab = d // LANES
    tiles_per_k = b_ * l_ // tm
    tile = lambda b, t: (b, t, 0)

    def y_spec(k_):
        return pl.BlockSpec((tm * nslab, LANES), lambda b, t: (k_ * tiles_per_k + b * (l_ // tm) + t, 0))

    return pl.pallas_call(
        _final_kernel,
        grid=(b_, l_ // tm),
        in_specs=[
            pl.BlockSpec((1, tm, d), tile),
            y_spec(0), y_spec(1), y_spec(2), y_spec(3),
            pl.BlockSpec((1, tm, LANES), tile),
            pl.BlockSpec((1, 1, d), lambda b, t: (b, 0, 0)),
            pl.BlockSpec(g_post.shape, lambda b, t: (0, 0)),
        ],
        out_specs=pl.BlockSpec((1, tm, d), tile),
        out_shape=jax.ShapeDtypeStruct((b_, l_, d), F32),
        compiler_params=pltpu.CompilerParams(
            dimension_semantics=("arbitrary", "arbitrary"), vmem_limit_bytes=VMEM_LIMIT),
    )(x1, y, y, y, y, probs, gt2, g_post)


def kernel(x, c, ctx, c_ctx, w_ada, b_ada, g_pre_mix, g_post_mix, g_pre_ffn, g_post_ffn, w_in, hgrn_lb, hgrn_norm_w, gla_gk_w2, gla_gk_b, gla_norm_w, w_up_a, w_up_b, w_o, w_router, b_router, w_gate, b_gate, w_up, b_up, w_down, b_down):
    b_, l_, d = x.shape
    lc = ctx.shape[1]
    n_experts = w_router.shape[-1]
    layer = 0
    hw = HGRN_HEADS * HGRN_DK
    kw = GLA_HEADS * GLA_DK
    vw = GLA_HEADS * GLA_DV
    rk = GLA_GATE_RANK

    rows = -(-(b_ + 1) // 8) * 8
    cc = jnp.concatenate([c, c_ctx[None, :], jnp.zeros((rows - b_ - 1, d), F32)], axis=0)
    mod = _ada_call(cc, w_ada[layer], b_ada[layer])[:b_ + 1]
    sh1, sc1, gt1, sh2, sc2, gt2 = [m.reshape(b_ + 1, 1, d) for m in jnp.split(mod, 6, axis=-1)]
    a1 = g_pre_mix[layer] * (1.0 + sc1)
    a2 = g_pre_ffn[layer] * (1.0 + sc2)

    w = w_in[layer]
    o0 = 0
    cols = {}
    for name, size in (("qa", hw), ("zf", hw), ("zb", hw), ("ia", hw), ("oga", hw), ("qb", kw), ("kb", kw),
                       ("vb", vw), ("rf", rk), ("rb", rk), ("ogb", vw), ("mga", d), ("mgb", d)):
        cols[name] = w[:, o0:o0 + size]
        o0 += size
    w_scan = jnp.concatenate([
        cols["qa"], cols["zf"], cols["zb"], cols["ia"],
        cols["qb"] * (GLA_DK ** -0.5), cols["kb"], cols["vb"]], axis=1).astype(BF16)
    w_rank = jnp.pad(jnp.concatenate([cols["rf"], cols["rb"]], axis=1), ((0, 0), (0, LANES - 2 * rk))).astype(BF16)
    w2 = jnp.zeros((LANES, 2 * kw), F32)
    w2 = w2.at[0:rk, 0:kw].set(gla_gk_w2[layer, 0])
    w2 = w2.at[rk:2 * rk, kw:].set(gla_gk_w2[layer, 1])
    w2 = w2.astype(BF16)
    gkb = jnp.concatenate([gla_gk_b[layer, 0], gla_gk_b[layer, 1]])[None, :]
    lb = jnp.cumsum(jax.nn.softmax(hgrn_lb.astype(F32), axis=0), axis=0)[layer]
    w_gates = jnp.concatenate([cols["oga"], cols["ogb"], cols["mga"], cols["mgb"]], axis=1).astype(BF16)
    norm_w = jnp.concatenate([jnp.tile(hgrn_norm_w[layer], HGRN_HEADS), jnp.tile(gla_norm_w[layer], GLA_HEADS)])[None, :]
    w_r = jnp.stack(_split_bf16(jnp.pad(w_router[layer], ((0, 0), (0, LANES - n_experts)))))
    b_r = jnp.pad(b_router[layer], (0, LANES - n_experts), constant_values=-1e30)[None, :]

    q, v, akf, akb, bk, gf, gb = _inproj_call(x, ctx, a1, sh1, w_scan, w_rank, w2, gkb, lb)
    o_f, o_b = _scan_call(q, v, akf, akb, bk, gf, gb, lc)
    x1, h2, top_idx, probs = _merge_call(
        x, a1, sh1, gt1, a2, sh2, o_f, o_b, w_gates, norm_w,
        w_up_a[layer].astype(BF16), w_up_b[layer].astype(BF16), w_o[layer].astype(BF16),
        g_post_mix[layer][None, :], w_r, b_r)

    t_ = b_ * l_
    be, n_valid, tok, dst, n_out_rows = _dispatch_plan(top_idx.reshape(t_, LANES)[:, :TOP_K], n_experts, d // LANES)
    y = _moe_call(be, n_valid, tok, dst, h2, n_out_rows,
                  w_gate[layer], b_gate[layer], w_up[layer], b_up[layer], w_down[layer], b_down[layer])
    return _final_call(x1, y, probs, gt2, g_post_ffn[layer][None, :])
```

```python
import functools

import jax
import jax.numpy as jnp
from jax import lax
from jax.experimental import pallas as pl
from jax.experimental.pallas import tpu as pltpu

F32 = jnp.float32
BF16 = jnp.bfloat16

NORM_EPS = 1e-6
HGRN_HEADS = 4
HGRN_DK = 128
GLA_HEADS = 4
GLA_DK = 64
GLA_DV = 128
GLA_GATE_RANK = 16
GLA_GATE_NORMALIZER = 16.0
N_HEADS = HGRN_HEADS + GLA_HEADS
HEAD_W = 128
TOP_K = 4
SWIGLU_LIMIT = 7.0
SWIGLU_ALPHA = 1.702
MOE_BLOCK = 256
SCAN_CHUNK = 64
SCAN_BLOCK = 256
LOG2E = 1.4426950408889634
TOKEN_TILE = 256
LATENT_TILE = 512
ADA_TILE_N = 1536
LANES = 128
SUBLANES = 8
VMEM_LIMIT = 56 * 1024 * 1024


def _dot(a, b):
    return jnp.dot(a, b, preferred_element_type=F32)


def _dot_nt(a, b):
    return lax.dot_general(a, b, (((1,), (1,)), ((), ())), preferred_element_type=F32)


def _dot_tn(a, b):
    return lax.dot_general(a, b, (((0,), (0,)), ((), ())), preferred_element_type=F32)


def _split_bf16(a):
    hi = a.astype(BF16)
    lo = (a - hi.astype(F32)).astype(BF16)
    return hi, lo


def _dot3(a, b):
    ah, al = _split_bf16(a)
    bh, bl = _split_bf16(b)
    return _dot(ah, bh) + _dot(al, bh) + _dot(ah, bl)


def _rms(x):
    return x * lax.rsqrt(jnp.mean(x * x, axis=-1, keepdims=True) + NORM_EPS)


def _sigmoid(x):
    return 1.0 / (1.0 + jnp.exp(-x))


def _ada_kernel(c_ref, w_ref, b_ref, o_ref):
    c = c_ref[...]
    o_ref[...] = _dot3(c * _sigmoid(c), w_ref[...]) + b_ref[...]


def _ada_call(cc, w_ada, b_ada):
    rows, d = cc.shape
    n = w_ada.shape[1]
    tn = ADA_TILE_N
    return pl.pallas_call(
        _ada_kernel,
        grid=(n // tn,),
        in_specs=[
            pl.BlockSpec((rows, d), lambda j: (0, 0)),
            pl.BlockSpec((d, tn), lambda j: (0, j)),
            pl.BlockSpec((1, tn), lambda j: (0, j)),
        ],
        out_specs=pl.BlockSpec((rows, tn), lambda j: (0, j)),
        out_shape=jax.ShapeDtypeStruct((rows, n), F32),
        compiler_params=pltpu.CompilerParams(
            dimension_semantics=("arbitrary",), vmem_limit_bytes=VMEM_LIMIT),
    )(cc, w_ada, b_ada.reshape(1, n))


def _inproj_kernel(n_ctx_tiles, x_ref, ctx_ref, a_ref, s_ref, actx_ref, sctx_ref, w_ref, wr_ref, w2_ref, gkb_ref,
                   lb_ref, q_ref, v_ref, akf_ref, akb_ref, bk_ref, gf_ref, gb_ref):
    t = pl.program_id(1)
    is_ctx = t < n_ctx_tiles
    aw = HGRN_HEADS * HGRN_DK
    bw = GLA_HEADS * GLA_DK
    parts = [dict(i=i) for i in range(x_ref.shape[0])]
    for p in parts:
        i = p["i"]
        xt = jnp.where(is_ctx, ctx_ref[i], x_ref[i])
        a = jnp.where(is_ctx, actx_ref[0], a_ref[i])
        s = jnp.where(is_ctx, sctx_ref[0], s_ref[i])
        p["hb"] = (_rms(xt) * a + s).astype(BF16)
    for p in parts:
        hb = p["hb"]

        def seg(lo, hi):
            return _dot(hb, w_ref[:, lo:hi])

        p["z_f"], p["z_b"] = seg(aw, 2 * aw), seg(2 * aw, 3 * aw)
        r = _dot(hb, wr_ref[...]).astype(BF16)
        p["pre"] = _dot(r, w2_ref[...]) + gkb_ref[...]
        p["q_a"], p["v_a"] = seg(0, aw), seg(3 * aw, 4 * aw)
        p["q_b"], p["k_b"] = seg(4 * aw, 4 * aw + bw), seg(4 * aw + bw, 4 * aw + 2 * bw)
        p["v_b"] = seg(4 * aw + 2 * bw, 5 * aw + 2 * bw)
    for p in parts:
        i = p["i"]
        for j, z, k_ref, g_ref in ((1, p["z_f"], akf_ref, gf_ref), (2, p["z_b"], akb_ref, gb_ref)):
            lb = lb_ref[j - 1:j, :]
            sg = _sigmoid(z)
            g_ref[i, :, 0:aw] = jnp.log2(lb + (1.0 - lb) * sg)
            k_ref[i] = ((1.0 - lb) * (1.0 - sg)).astype(BF16)
        pre = p["pre"]
        ls = (jnp.minimum(pre, 0.0) - jnp.log(1.0 + jnp.exp(-jnp.abs(pre)))) * (LOG2E / GLA_GATE_NORMALIZER)
        gf_ref[i, :, aw:aw + bw] = ls[:, 0:bw]
        gb_ref[i, :, aw:aw + bw] = ls[:, bw:2 * bw]
        q_ref[i, :, 0:aw] = p["q_a"].astype(BF16)
        q_ref[i, :, aw:aw + bw] = p["q_b"].astype(BF16)
        v_ref[i, :, 0:aw] = p["v_a"].astype(BF16)
        v_ref[i, :, aw:2 * aw] = p["v_b"].astype(BF16)
        bk_ref[i] = p["k_b"].astype(BF16)


def _inproj_call(x, ctx, mod_a, mod_s, w_scan, w_rank, w2, gkb, lb):
    b_, l_, d = x.shape
    lc = ctx.shape[1]
    tm = TOKEN_TILE
    nb = max(k for k in (4, 2, 1) if b_ % k == 0)
    nct, nlt = lc // tm, l_ // tm
    lt = lc + l_

    def x_map(b, t):
        return (b, jnp.maximum(t - nct, 0), 0)

    def ctx_map(b, t):
        return (b, jnp.minimum(t, nct - 1), 0)

    per_b = lambda b, t: (b, 0, 0)
    ctx_row = lambda b, t: (b_, 0, 0)
    const2 = lambda b, t: (0, 0)
    out_map = lambda b, t: (b, t, 0)
    aw, bw, vw = HGRN_HEADS * HGRN_DK, GLA_HEADS * GLA_DK, GLA_HEADS * GLA_DV
    qk_w, v_w = aw + bw, aw + vw

    def out(width, dtype):
        return jax.ShapeDtypeStruct((b_, lt, width), dtype), pl.BlockSpec((nb, tm, width), out_map)

    outs = [out(qk_w, BF16), out(v_w, BF16), out(aw, BF16), out(aw, BF16), out(bw, BF16), out(qk_w, F32), out(qk_w, F32)]
    return pl.pallas_call(
        functools.partial(_inproj_kernel, nct),
        grid=(b_ // nb, nct + nlt),
        in_specs=[
            pl.BlockSpec((nb, tm, d), x_map),
            pl.BlockSpec((nb, tm, d), ctx_map),
            pl.BlockSpec((nb, 1, d), per_b),
            pl.BlockSpec((nb, 1, d), per_b),
            pl.BlockSpec((1, 1, d), ctx_row),
            pl.BlockSpec((1, 1, d), ctx_row),
            pl.BlockSpec(w_scan.shape, const2),
            pl.BlockSpec(w_rank.shape, const2),
            pl.BlockSpec(w2.shape, const2),
            pl.BlockSpec(gkb.shape, const2),
            pl.BlockSpec(lb.shape, const2),
        ],
        out_specs=[o[1] for o in outs],
        out_shape=[o[0] for o in outs],
        compiler_params=pltpu.CompilerParams(
            dimension_semantics=("arbitrary", "arbitrary"), vmem_limit_bytes=VMEM_LIMIT),
    )(x, ctx, mod_a, mod_s, mod_a, mod_s, w_scan, w_rank, w2, gkb, lb)


def _causal(forward):
    row = lax.broadcasted_iota(jnp.int32, (SCAN_CHUNK, SCAN_CHUNK), 0)
    col = lax.broadcasted_iota(jnp.int32, (SCAN_CHUNK, SCAN_CHUNK), 1)
    return (row >= col) if forward else (col >= row)


def _scan_block(with_out, qf_ref, vf_ref, akf_ref, bkf_ref, gf_ref, qb_ref, vb_ref, akb_ref, bkb_ref, gb_ref,
                of_ref, ob_ref, sf_ref, sb_ref):
    c_ = SCAN_CHUNK
    ncb = qf_ref.shape[1] // c_
    aw = HGRN_HEADS * HGRN_DK
    n_pairs = GLA_HEADS // 2
    dirs = []
    for i in range(qf_ref.shape[0]):
        dirs.append((True,) + tuple(r.at[i] for r in (qf_ref, vf_ref, akf_ref, bkf_ref, gf_ref, of_ref, sf_ref)))
        dirs.append((False,) + tuple(r.at[i] for r in (qb_ref, vb_ref, akb_ref, bkb_ref, gb_ref, ob_ref, sb_ref)))
    lane = lax.broadcasted_iota(jnp.int32, (c_, HEAD_W), 1)
    low = lane < GLA_DK
    pr = lax.broadcasted_iota(jnp.int32, (c_, 2 * c_), 0)
    pc = lax.broadcasted_iota(jnp.int32, (c_, 2 * c_), 1) % c_
    pair_causal = {True: pr >= pc, False: pc >= pr}
    srow = lax.broadcasted_iota(jnp.int32, (2 * HEAD_W, HEAD_W), 0) < HEAD_W
    scol = lax.broadcasted_iota(jnp.int32, (2 * HEAD_W, HEAD_W), 1) < GLA_DK
    pair_state = srow == scol
    zero_v = jnp.zeros((c_, HEAD_W), BF16)
    cums = {}
    for di, (fw, _, _, _, _, g_ref, _, _) in enumerate(dirs):
        tri = jnp.where(_causal(fw), 1.0, 0.0).astype(BF16)
        for c in range(ncb):
            g_hi, g_lo = _split_bf16(g_ref[c * c_:(c + 1) * c_, :])
            cums[di, c] = _dot(tri, g_hi) + _dot(tri, g_lo)
    ops = []
    for di, (fw, q_ref, v_ref, ak_ref, bk_ref, _, _, _) in enumerate(dirs):
        end_row, mid_row = (c_ - 1, c_ // 2 - 1) if fw else (0, c_ // 2)
        for c in range(ncb):
            rows = slice(c * c_, (c + 1) * c_)
            cum = cums[di, c]
            total = cum[end_row:end_row + 1, :]
            mid = cum[mid_row:mid_row + 1, :]
            e_mid = jnp.exp2(mid)
            e_rest = jnp.exp2(total - mid)
            e_tot = jnp.exp2(total)
            for u in range(HGRN_HEADS + n_pairs):
                pair = u >= HGRN_HEADS
                sl = slice(u * HEAD_W, (u + 1) * HEAD_W)
                k_src = bk_ref[rows, (u - HGRN_HEADS) * HEAD_W:(u - HGRN_HEADS + 1) * HEAD_W] if pair \
                    else ak_ref[rows, sl]
                qs = q_ref[rows, sl].astype(F32) * jnp.exp2(cum[:, sl] - mid[:, sl])
                ks = k_src.astype(F32) * jnp.exp2(mid[:, sl] - cum[:, sl])
                ksb = ks.astype(BF16)
                if pair:
                    vl = slice(aw + (u - HGRN_HEADS) * 2 * HEAD_W, aw + (u - HGRN_HEADS + 1) * 2 * HEAD_W)
                    v = v_ref[rows, vl]
                    k_rhs = jnp.concatenate([jnp.where(low, ksb, 0), jnp.where(low, 0, ksb)], axis=0)
                    v_rhs = jnp.concatenate([jnp.concatenate([v[:, :HEAD_W], zero_v], axis=1),
                                             jnp.concatenate([zero_v, v[:, HEAD_W:]], axis=1)], axis=0)
                    st_rows = slice(aw + (u - HGRN_HEADS) * 2 * HEAD_W, aw + (u - HGRN_HEADS + 1) * 2 * HEAD_W)
                else:
                    vl = sl
                    v = v_ref[rows, vl]
                    k_rhs, v_rhs = ksb, v
                    st_rows = sl
                ops.append(dict(di=di, fw=fw, c=c, pair=pair, rows=rows, vl=vl, st_rows=st_rows, v=v,
                                k_rhs=k_rhs, v_rhs=v_rhs, qs=qs.astype(BF16), qd=(qs * e_mid[:, sl]).astype(BF16),
                                kd=(ks * e_rest[:, sl]).astype(BF16), e_tot=e_tot[:, sl]))
    if with_out:
        for d in ops:
            d["sc"] = _dot_nt(d["qs"], d["k_rhs"])
        for d in ops:
            mask = pair_causal[d["fw"]] if d["pair"] else _causal(d["fw"])
            d["sc"] = jnp.where(mask, d["sc"], 0.0).astype(BF16)
        for d in ops:
            d["o"] = _dot(d["sc"], d["v_rhs"])
    for d in ops:
        d["upd"] = _dot_tn(d["v"], d["kd"])
    for d in ops:
        if d["pair"]:
            d["upd"] = jnp.where(pair_state, d["upd"], 0.0)
    for di, (fw, _, _, _, _, _, o_ref, st_ref) in enumerate(dirs):
        sts = {}
        for c in (range(ncb) if fw else range(ncb - 1, -1, -1)):
            for d in ops:
                if d["di"] != di or d["c"] != c:
                    continue
                key = d["st_rows"].start
                if key not in sts:
                    sts[key] = st_ref[d["st_rows"], :]
                if with_out:
                    o = d["o"] + _dot_nt(d["qd"], sts[key].astype(BF16))
                    o_ref[d["rows"], d["vl"]] = o.astype(o_ref.dtype)
                sts[key] = sts[key] * d["e_tot"] + d["upd"]
        for d in ops:
            if d["di"] == di and d["c"] == 0:
                st_ref[d["st_rows"], :] = sts[d["st_rows"].start]


def _scan_kernel(n_ctx_blocks, *refs):
    s = pl.program_id(1)

    @pl.when(s == 0)
    def _():
        refs[-2][...] = jnp.zeros_like(refs[-2])
        refs[-1][...] = jnp.zeros_like(refs[-1])

    @pl.when(s < n_ctx_blocks)
    def _():
        _scan_block(False, *refs)

    @pl.when(s >= n_ctx_blocks)
    def _():
        _scan_block(True, *refs)


def _scan_call(q, v, akf, akb, bk, gf, gb, lc):
    b_, lt, _ = q.shape
    c_ = SCAN_BLOCK
    ncc = lc // c_
    nlc = (lt - lc) // c_
    n = ncc + nlc

    def fwd(b, s):
        return (b, s, 0)

    def bwd(b, s):
        return (b, jnp.where(s < ncc, ncc - 1 - s, n + ncc - 1 - s), 0)

    def out_fwd(b, s):
        return (b, jnp.maximum(s - ncc, 0), 0)

    def out_bwd(b, s):
        return (b, jnp.where(s < ncc, nlc - 1, n - 1 - s), 0)

    nb = 2 if b_ % 2 == 0 else 1

    def specs(index_map):
        return [pl.BlockSpec((nb, c_, a.shape[-1]), index_map) for a in (q, v, akf, bk, gf)]

    v_w = v.shape[-1]
    out = jax.ShapeDtypeStruct((b_, lt - lc, v_w), BF16)
    state = pltpu.VMEM((nb, v_w, HEAD_W), F32)
    return pl.pallas_call(
        functools.partial(_scan_kernel, ncc),
        grid=(b_ // nb, n),
        in_specs=specs(fwd) + specs(bwd),
        out_specs=[pl.BlockSpec((nb, c_, v_w), out_fwd), pl.BlockSpec((nb, c_, v_w), out_bwd)],
        out_shape=[out, out],
        scratch_shapes=[state, state],
        compiler_params=pltpu.CompilerParams(
            dimension_semantics=("arbitrary", "arbitrary"), vmem_limit_bytes=VMEM_LIMIT),
    )(q, v, akf, bk, gf, q, v, akb, bk, gb)


def _merge_kernel(x_ref, a1_ref, s1_ref, gt1_ref, a2_ref, s2_ref, of_ref, ob_ref, wg_ref, nw_ref,
                  wua_ref, wub_ref, wo_ref, gpost_ref, wr_ref, br_ref,
                  x1_ref, h2_ref, idx_ref, prob_ref):
    tm, d = x_ref.shape[1:]
    hr = tm // 2
    nslab = d // LANES
    ow = N_HEADS * HEAD_W
    aw = HGRN_HEADS * HEAD_W
    halves = [dict(i=i, rows=slice(i * hr, (i + 1) * hr)) for i in range(2)]
    for h in halves:
        h["x"] = x_ref[0, h["rows"], :]
        h["hb"] = (_rms(h["x"]) * a1_ref[0] + s1_ref[0]).astype(BF16)
    for h in halves:
        h["og"] = _dot(h["hb"], wg_ref[:, 0:ow])
        h["mg_a"] = _dot(h["hb"], wg_ref[:, ow:ow + d])
        h["mg_b"] = _dot(h["hb"], wg_ref[:, ow + d:ow + 2 * d])
    for h in halves:
        o = of_ref[0, h["rows"], :].astype(F32) + ob_ref[0, h["rows"], :].astype(F32)
        heads = [_rms(o[:, j * HEAD_W:(j + 1) * HEAD_W]) for j in range(N_HEADS)]
        og = h["og"]
        h["r"] = (jnp.concatenate(heads, axis=-1) * nw_ref[...] * (og * _sigmoid(og))).astype(BF16)
    for h in halves:
        h["y_a"] = _dot(h["r"][:, 0:aw], wua_ref[...])
        h["y_b"] = _dot(h["r"][:, aw:ow], wub_ref[...])
    for h in halves:
        h["m"] = (_sigmoid(h["mg_a"]) * h["y_a"] + _sigmoid(h["mg_b"]) * h["y_b"]).astype(BF16)
    for h in halves:
        h["mix"] = _dot(h["m"], wo_ref[...])
    for h in halves:
        x1 = h["x"] + gt1_ref[0] * (_rms(h["mix"]) * gpost_ref[...])
        x1_ref[0, h["rows"], :] = x1
        h2 = _rms(x1) * a2_ref[0] + s2_ref[0]
        for s in range(nslab):
            h2_ref[pl.ds(h["i"] * hr * nslab + s, hr, stride=nslab), :] = h2[:, s * LANES:(s + 1) * LANES]
        h["h_hi"], h["h_lo"] = _split_bf16(h2)
    for h in halves:
        h["logits"] = (_dot(h["h_hi"], wr_ref[0]) + _dot(h["h_lo"], wr_ref[0]) + _dot(h["h_hi"], wr_ref[1])
                       + br_ref[...])
    for h in halves:
        logits = h["logits"]
        lane = lax.broadcasted_iota(jnp.int32, logits.shape, 1).astype(F32)
        vals, idxs = [], []
        for _ in range(TOP_K):
            m = jnp.max(logits, axis=-1, keepdims=True)
            sel = jnp.min(jnp.where(logits == m, lane, float(LANES)), axis=-1, keepdims=True)
            vals.append(m)
            idxs.append(sel)
            logits = jnp.where(lane == sel, -jnp.inf, logits)
        exps = [jnp.exp(v_ - vals[0]) for v_ in vals]
        denom = exps[0] + exps[1] + exps[2] + exps[3]
        idx_out = jnp.zeros_like(lane)
        prob_out = jnp.zeros_like(lane)
        for k_ in range(TOP_K):
            idx_out = jnp.where(lane == float(k_), idxs[k_], idx_out)
            prob_out = jnp.where(lane == float(k_), exps[k_] / denom, prob_out)
        idx_ref[0, h["rows"], :] = idx_out.astype(jnp.int32)
        prob_ref[0, h["rows"], :] = prob_out


def _merge_call(x, a1, s1, gt1, a2, s2, o_f, o_b, w_gates, norm_w, w_up_a, w_up_b, w_o, g_post, w_r, b_r):
    b_, l_, d = x.shape
    tm = LATENT_TILE
    tile = lambda b, t: (b, t, 0)
    per_b = lambda b, t: (b, 0, 0)
    const2 = lambda b, t: (0, 0)
    return pl.pallas_call(
        _merge_kernel,
        grid=(b_, l_ // tm),
        in_specs=[
            pl.BlockSpec((1, tm, d), tile),
            pl.BlockSpec((1, 1, d), per_b), pl.BlockSpec((1, 1, d), per_b), pl.BlockSpec((1, 1, d), per_b),
            pl.BlockSpec((1, 1, d), per_b), pl.BlockSpec((1, 1, d), per_b),
            pl.BlockSpec((1, tm, o_f.shape[-1]), tile), pl.BlockSpec((1, tm, o_b.shape[-1]), tile),
            pl.BlockSpec(w_gates.shape, const2),
            pl.BlockSpec(norm_w.shape, const2),
            pl.BlockSpec(w_up_a.shape, const2),
            pl.BlockSpec(w_up_b.shape, const2),
            pl.BlockSpec(w_o.shape, const2),
            pl.BlockSpec(g_post.shape, const2),
            pl.BlockSpec(w_r.shape, lambda b, t: (0, 0, 0)),
            pl.BlockSpec(b_r.shape, const2),
        ],
        out_specs=[
            pl.BlockSpec((1, tm, d), tile),
            pl.BlockSpec((tm * (d // LANES), LANES), lambda b, t: (b * (l_ // tm) + t, 0)),
            pl.BlockSpec((1, tm, LANES), tile), pl.BlockSpec((1, tm, LANES), tile),
        ],
        out_shape=[
            jax.ShapeDtypeStruct((b_, l_, d), F32), jax.ShapeDtypeStruct((b_ * l_ * (d // LANES), LANES), F32),
            jax.ShapeDtypeStruct((b_, l_, LANES), jnp.int32), jax.ShapeDtypeStruct((b_, l_, LANES), F32),
        ],
        compiler_params=pltpu.CompilerParams(
            dimension_semantics=("arbitrary", "arbitrary"), vmem_limit_bytes=VMEM_LIMIT),
    )(x, a1, s1, gt1, a2, s2, o_f, o_b, w_gates, norm_w, w_up_a, w_up_b, w_o, g_post, w_r, b_r)


def _moe_kernel(be_ref, nv_ref, tok_ref, tokn_ref, dst_ref, h2_hbm,
                wg_ref, bg_ref, wu_ref, bu_ref, wd_ref, bd_ref, out_hbm,
                xbuf, ybuf, wgb, wub, wdb, gsem, ssem):
    i = pl.program_id(0)
    n = pl.num_programs(0)
    slot = i % 2
    rows = MOE_BLOCK
    nslab = xbuf.shape[0] // (2 * rows)
    blk = rows * nslab

    def slab(ref, start):
        return ref.at[pl.ds(pl.multiple_of(start, nslab), nslab), :]

    def for_slot(fn):
        for s_static in range(2):
            @pl.when(slot == s_static)
            def _():
                fn(s_static)

    def issue_gather(idx_ref, s):
        for r in range(rows):
            pltpu.make_async_copy(slab(h2_hbm, idx_ref[0, 0, r]), xbuf.at[pl.ds(s * blk + r * nslab, nslab), :],
                                  gsem.at[s]).start(priority=r % 2)

    def wait_gather(s):
        pltpu.make_async_copy(h2_hbm.at[pl.ds(0, blk), :], xbuf.at[pl.ds(pl.multiple_of(s * blk, blk), blk), :],
                              gsem.at[s]).wait()

    def issue_scatter(s):
        for r in range(rows):
            pltpu.make_async_copy(ybuf.at[pl.ds(s * blk + r * nslab, nslab), :], slab(out_hbm, dst_ref[0, 0, r]),
                                  ssem.at[s]).start(priority=r % 2)

    def wait_scatter(s):
        pltpu.make_async_copy(ybuf.at[pl.ds(pl.multiple_of(s * blk, blk), blk), :], out_hbm.at[pl.ds(0, blk), :],
                              ssem.at[s]).wait()

    @pl.when(jnp.logical_and(i == 0, nv_ref[0] > 0))
    def _():
        issue_gather(tok_ref, 0)

    nxt = jnp.minimum(i + 1, n - 1)

    @pl.when(jnp.logical_and(i + 1 < n, nv_ref[nxt] > 0))
    def _():
        for_slot(lambda s: issue_gather(tokn_ref, 1 - s))

    prev = jnp.maximum(i - 1, 0)

    @pl.when(jnp.logical_or(i == 0, be_ref[i] != be_ref[prev]))
    def _():
        wgb[...] = wg_ref[0].astype(BF16)
        wub[...] = wu_ref[0].astype(BF16)
        wdb[...] = wd_ref[0].astype(BF16)

    @pl.when(jnp.logical_and(i >= 2, nv_ref[jnp.maximum(i - 2, 0)] > 0))
    def _():
        wait_scatter(slot)

    @pl.when(nv_ref[i] > 0)
    def _():
        wait_gather(slot)
        base = slot * blk
        xb = jnp.concatenate([xbuf[pl.ds(base + s, rows, stride=nslab), :] for s in range(nslab)],
                             axis=-1).astype(BF16)
        g = jnp.minimum(_dot(xb, wgb[...]) + bg_ref[0], SWIGLU_LIMIT)
        u = jnp.clip(_dot(xb, wub[...]) + bu_ref[0], -SWIGLU_LIMIT, SWIGLU_LIMIT)
        act = g * _sigmoid(SWIGLU_ALPHA * g) * (u + 1.0)
        y = _dot(act.astype(BF16), wdb[...]) + bd_ref[0]
        for s in range(nslab):
            ybuf[pl.ds(base + s, rows, stride=nslab), :] = y[:, s * LANES:(s + 1) * LANES]
        for_slot(issue_scatter)

    @pl.when(nv_ref[i] == 0)
    def _():
        ybuf[pl.ds(pl.multiple_of(slot * blk, blk), blk), :] = jnp.zeros((blk, LANES), F32)
        fill = pltpu.make_async_copy(ybuf.at[pl.ds(pl.multiple_of(slot * blk, blk), blk), :],
                                     out_hbm.at[pl.ds(pl.multiple_of(dst_ref[0, 0, 0], nslab), blk), :], ssem.at[slot])
        fill.start()
        fill.wait()

    @pl.when(i == n - 1)
    def _():
        @pl.when(jnp.logical_and(i >= 1, nv_ref[prev] > 0))
        def _():
            wait_scatter(1 - slot)

        @pl.when(nv_ref[i] > 0)
        def _():
            wait_scatter(slot)


def _moe_call(block_expert, n_valid, tok, dst, h2, n_out_rows, w_gate, b_gate, w_up, b_up, w_down, b_down):
    e_, d, f = w_gate.shape
    nslab = d // LANES
    nblk = tok.shape[0]
    rows = MOE_BLOCK

    def blk(i, be, nv):
        return (i, 0, 0)

    def blk_next(i, be, nv):
        return (jnp.minimum(i + 1, nblk - 1), 0, 0)

    def expert(i, be, nv):
        return (be[i], 0, 0)

    smem_rows = functools.partial(pl.BlockSpec, (1, 1, rows), memory_space=pltpu.SMEM)
    grid_spec = pltpu.PrefetchScalarGridSpec(
        num_scalar_prefetch=2,
        grid=(nblk,),
        in_specs=[
            smem_rows(blk), smem_rows(blk_next), smem_rows(blk),
            pl.BlockSpec(memory_space=pl.ANY),
            pl.BlockSpec((1, d, f), expert), pl.BlockSpec((1, 1, f), expert),
            pl.BlockSpec((1, d, f), expert), pl.BlockSpec((1, 1, f), expert),
            pl.BlockSpec((1, f, d), expert), pl.BlockSpec((1, 1, d), expert),
        ],
        out_specs=pl.BlockSpec(memory_space=pl.ANY),
        scratch_shapes=[
            pltpu.VMEM((2 * rows * nslab, LANES), F32), pltpu.VMEM((2 * rows * nslab, LANES), F32),
            pltpu.VMEM((d, f), BF16), pltpu.VMEM((d, f), BF16), pltpu.VMEM((f, d), BF16),
            pltpu.SemaphoreType.DMA((2,)), pltpu.SemaphoreType.DMA((2,)),
        ],
    )
    return pl.pallas_call(
        _moe_kernel,
        grid_spec=grid_spec,
        out_shape=jax.ShapeDtypeStruct((n_out_rows * nslab, LANES), F32),
        compiler_params=pltpu.CompilerParams(
            dimension_semantics=("arbitrary",), vmem_limit_bytes=VMEM_LIMIT),
    )(block_expert, n_valid, tok, tok, dst, h2,
      w_gate, b_gate.reshape(e_, 1, f), w_up, b_up.reshape(e_, 1, f), w_down, b_down.reshape(e_, 1, d))


def _dispatch_plan(top_idx, n_experts, nslab):
    t_ = top_idx.shape[0]
    n_pairs = t_ * TOP_K
    flat_e = top_idx.reshape(-1)
    pair_bits = max(n_pairs - 1, 1).bit_length()
    keys = jnp.sort(flat_e * (1 << pair_bits) + jnp.arange(n_pairs, dtype=jnp.int32))
    order = keys & ((1 << pair_bits) - 1)
    counts = jnp.sum((flat_e[:, None] == jnp.arange(n_experts, dtype=jnp.int32)[None, :]).astype(jnp.int32), axis=0)
    nblk_e = (counts + MOE_BLOCK - 1) // MOE_BLOCK
    blk_end = jnp.cumsum(nblk_e)
    blk_start = blk_end - nblk_e
    start = jnp.cumsum(counts) - counts
    nblk = n_pairs // MOE_BLOCK + n_experts
    b = jnp.arange(nblk, dtype=jnp.int32)
    be = jnp.minimum(jnp.sum((b[:, None] >= blk_end[None, :]).astype(jnp.int32), axis=1), n_experts - 1)
    first_row = (b - blk_start[be]) * MOE_BLOCK
    n_valid = jnp.clip(counts[be] - first_row, 0, MOE_BLOCK).astype(jnp.int32)
    r = jnp.arange(MOE_BLOCK, dtype=jnp.int32)[None, :]
    valid = r < n_valid[:, None]
    src = jnp.clip(start[be][:, None] + first_row[:, None] + r, 0, n_pairs - 1)
    pair = jnp.where(valid, order[src], 0).astype(jnp.int32)
    tok = pair // TOP_K
    spare = (jnp.cumsum((~valid).reshape(-1).astype(jnp.int32)) - 1).reshape(nblk, MOE_BLOCK)
    dst = jnp.where(valid, (pair % TOP_K) * t_ + tok, n_pairs + spare).astype(jnp.int32)
    n_out_rows = n_pairs + n_experts * MOE_BLOCK
    return (be, n_valid, (tok * nslab).reshape(nblk, 1, MOE_BLOCK), (dst * nslab).reshape(nblk, 1, MOE_BLOCK),
            n_out_rows)


def _final_kernel(x1_ref, y0_ref, y1_ref, y2_ref, y3_ref, p_ref, gt2_ref, gpost_ref, o_ref):
    p = p_ref[0]
    tm, d = x1_ref.shape[1:]
    nslab = d // LANES
    parts = []
    for s in range(nslab):
        acc = p[:, 0:1] * y0_ref[pl.ds(s, tm, stride=nslab), :]
        for k_, y_ref in ((1, y1_ref), (2, y2_ref), (3, y3_ref)):
            acc = acc + p[:, k_:k_ + 1] * y_ref[pl.ds(s, tm, stride=nslab), :]
        parts.append(acc)
    ffn = jnp.concatenate(parts, axis=-1)
    o_ref[0] = x1_ref[0] + gt2_ref[0] * (_rms(ffn) * gpost_ref[...])


def _final_call(x1, y, probs, gt2, g_post):
    b_, l_, d = x1.shape
    tm = LATENT_TILE
    nslab = d // LANES
    tiles_per_k = b_ * l_ // tm
    tile = lambda b, t: (b, t, 0)

    def y_spec(k_):
        return pl.BlockSpec((tm * nslab, LANES), lambda b, t: (k_ * tiles_per_k + b * (l_ // tm) + t, 0))

    return pl.pallas_call(
        _final_kernel,
        grid=(b_, l_ // tm),
        in_specs=[
            pl.BlockSpec((1, tm, d), tile),
            y_spec(0), y_spec(1), y_spec(2), y_spec(3),
            pl.BlockSpec((1, tm, LANES), tile),
            pl.BlockSpec((1, 1, d), lambda b, t: (b, 0, 0)),
            pl.BlockSpec(g_post.shape, lambda b, t: (0, 0)),
        ],
        out_specs=pl.BlockSpec((1, tm, d), tile),
        out_shape=jax.ShapeDtypeStruct((b_, l_, d), F32),
        compiler_params=pltpu.CompilerParams(
            dimension_semantics=("arbitrary", "arbitrary"), vmem_limit_bytes=VMEM_LIMIT),
    )(x1, y, y, y, y, probs, gt2, g_post)


def kernel(x, c, ctx, c_ctx, w_ada, b_ada, g_pre_mix, g_post_mix, g_pre_ffn, g_post_ffn, w_in, hgrn_lb, hgrn_norm_w, gla_gk_w2, gla_gk_b, gla_norm_w, w_up_a, w_up_b, w_o, w_router, b_router, w_gate, b_gate, w_up, b_up, w_down, b_down):
    b_, l_, d = x.shape
    lc = ctx.shape[1]
    n_experts = w_router.shape[-1]
    assert w_ada.shape[0] == 1, "one layer: the context stream is never updated"
    assert l_ % LATENT_TILE == 0 and lc % SCAN_BLOCK == 0 and (b_ * l_ * TOP_K) % MOE_BLOCK == 0
    assert n_experts <= LANES and d % LANES == 0
    layer = 0
    hw = HGRN_HEADS * HGRN_DK
    kw = GLA_HEADS * GLA_DK
    vw = GLA_HEADS * GLA_DV
    rk = GLA_GATE_RANK

    rows = -(-(b_ + 1) // SUBLANES) * SUBLANES
    cc = jnp.concatenate([c, c_ctx[None, :], jnp.zeros((rows - b_ - 1, d), F32)], axis=0)
    mod = _ada_call(cc, w_ada[layer], b_ada[layer])[:b_ + 1]
    sh1, sc1, gt1, sh2, sc2, gt2 = [m.reshape(b_ + 1, 1, d) for m in jnp.split(mod, 6, axis=-1)]
    a1 = g_pre_mix[layer] * (1.0 + sc1)
    a2 = g_pre_ffn[layer] * (1.0 + sc2)

    w = w_in[layer]
    o0 = 0
    cols = {}
    for name, size in (("qa", hw), ("zf", hw), ("zb", hw), ("ia", hw), ("oga", hw), ("qb", kw), ("kb", kw),
                       ("vb", vw), ("rf", rk), ("rb", rk), ("ogb", vw), ("mga", d), ("mgb", d)):
        cols[name] = w[:, o0:o0 + size]
        o0 += size
    w_scan = jnp.concatenate([
        cols["qa"], cols["zf"], cols["zb"], cols["ia"],
        cols["qb"] * (GLA_DK ** -0.5), cols["kb"], cols["vb"]], axis=1).astype(BF16)
    w_rank = jnp.pad(jnp.concatenate([cols["rf"], cols["rb"]], axis=1), ((0, 0), (0, LANES - 2 * rk))).astype(BF16)
    w2 = jnp.zeros((LANES, 2 * kw), F32)
    w2 = w2.at[0:rk, 0:kw].set(gla_gk_w2[layer, 0])
    w2 = w2.at[rk:2 * rk, kw:].set(gla_gk_w2[layer, 1])
    w2 = w2.astype(BF16)
    gkb = jnp.concatenate([gla_gk_b[layer, 0], gla_gk_b[layer, 1]])[None, :]
    lb = jnp.cumsum(jax.nn.softmax(hgrn_lb.astype(F32), axis=0), axis=0)[layer]
    w_gates = jnp.concatenate([cols["oga"], cols["ogb"], cols["mga"], cols["mgb"]], axis=1).astype(BF16)
    norm_w = jnp.concatenate([jnp.tile(hgrn_norm_w[layer], HGRN_HEADS), jnp.tile(gla_norm_w[layer], GLA_HEADS)])[None, :]
    w_r = jnp.stack(_split_bf16(jnp.pad(w_router[layer], ((0, 0), (0, LANES - n_experts)))))
    b_r = jnp.pad(b_router[layer], (0, LANES - n_experts), constant_values=-1e30)[None, :]

    q, v, akf, akb, bk, gf, gb = _inproj_call(x, ctx, a1, sh1, w_scan, w_rank, w2, gkb, lb)
    o_f, o_b = _scan_call(q, v, akf, akb, bk, gf, gb, lc)
    x1, h2, top_idx, probs = _merge_call(
        x, a1, sh1, gt1, a2, sh2, o_f, o_b, w_gates, norm_w,
        w_up_a[layer].astype(BF16), w_up_b[layer].astype(BF16), w_o[layer].astype(BF16),
        g_post_mix[layer][None, :], w_r, b_r)

    t_ = b_ * l_
    be, n_valid, tok, dst, n_out_rows = _dispatch_plan(top_idx.reshape(t_, LANES)[:, :TOP_K], n_experts, d // LANES)
    y = _moe_call(be, n_valid, tok, dst, h2, n_out_rows,
                  w_gate[layer], b_gate[layer], w_up[layer], b_up[layer], w_down[layer], b_down[layer])
    return _final_call(x1, y, probs, gt2, g_post_ffn[layer][None, :])
```

```python
import functools

import jax
import jax.numpy as jnp
from jax import lax
from jax.experimental import pallas as pl
from jax.experimental.pallas import tpu as pltpu

F32 = jnp.float32
BF16 = jnp.bfloat16

NORM_EPS = 1e-6
HGRN_HEADS = 4
HGRN_DK = 128
GLA_HEADS = 4
GLA_DK = 64
GLA_DV = 128
GLA_GATE_RANK = 16
GLA_GATE_NORMALIZER = 16.0
N_HEADS = HGRN_HEADS + GLA_HEADS
HEAD_W = 128
TOP_K = 4
SWIGLU_LIMIT = 7.0
SWIGLU_ALPHA = 1.702
MOE_BLOCK = 256
SCAN_CHUNK = 64
SCAN_BLOCK = 256
LOG2E = 1.4426950408889634
TOKEN_TILE = 256
LATENT_TILE = 512
ADA_TILE_N = 1536
LANES = 128
SUBLANES = 8
VMEM_LIMIT = 56 * 1024 * 1024


def _dot(a, b):
    return jnp.dot(a, b, preferred_element_type=F32)


def _dot_nt(a, b):
    return lax.dot_general(a, b, (((1,), (1,)), ((), ())), preferred_element_type=F32)


def _dot_tn(a, b):
    return lax.dot_general(a, b, (((0,), (0,)), ((), ())), preferred_element_type=F32)


def _split_bf16(a):
    hi = a.astype(BF16)
    lo = (a - hi.astype(F32)).astype(BF16)
    return hi, lo


def _dot3(a, b):
    ah, al = _split_bf16(a)
    bh, bl = _split_bf16(b)
    return _dot(ah, bh) + _dot(al, bh) + _dot(ah, bl)


def _rms(x):
    return x * lax.rsqrt(jnp.mean(x * x, axis=-1, keepdims=True) + NORM_EPS)


def _sigmoid(x):
    return 1.0 / (1.0 + jnp.exp(-x))


def _ada_kernel(c_ref, w_ref, b_ref, o_ref):
    c = c_ref[...]
    o_ref[...] = _dot3(c * _sigmoid(c), w_ref[...]) + b_ref[...]


def _ada_call(cc, w_ada, b_ada):
    rows, d = cc.shape
    n = w_ada.shape[1]
    tn = ADA_TILE_N
    return pl.pallas_call(
        _ada_kernel,
        grid=(n // tn,),
        in_specs=[
            pl.BlockSpec((rows, d), lambda j: (0, 0)),
            pl.BlockSpec((d, tn), lambda j: (0, j)),
            pl.BlockSpec((1, tn), lambda j: (0, j)),
        ],
        out_specs=pl.BlockSpec((rows, tn), lambda j: (0, j)),
        out_shape=jax.ShapeDtypeStruct((rows, n), F32),
        compiler_params=pltpu.CompilerParams(
            dimension_semantics=("arbitrary",), vmem_limit_bytes=VMEM_LIMIT),
    )(cc, w_ada, b_ada.reshape(1, n))


def _inproj_kernel(n_ctx_tiles, x_ref, ctx_ref, a_ref, s_ref, actx_ref, sctx_ref, w_ref, wr_ref, w2_ref, gkb_ref,
                   lb_ref, q_ref, v_ref, akf_ref, akb_ref, bk_ref, gf_ref, gb_ref):
    t = pl.program_id(1)
    is_ctx = t < n_ctx_tiles
    aw = HGRN_HEADS * HGRN_DK
    bw = GLA_HEADS * GLA_DK
    parts = [dict(i=i) for i in range(x_ref.shape[0])]
    for p in parts:
        i = p["i"]
        xt = jnp.where(is_ctx, ctx_ref[i], x_ref[i])
        a = jnp.where(is_ctx, actx_ref[0], a_ref[i])
        s = jnp.where(is_ctx, sctx_ref[0], s_ref[i])
        p["hb"] = (_rms(xt) * a + s).astype(BF16)
    for p in parts:
        hb = p["hb"]

        def seg(lo, hi):
            return _dot(hb, w_ref[:, lo:hi])

        p["z_f"], p["z_b"] = seg(aw, 2 * aw), seg(2 * aw, 3 * aw)
        r = _dot(hb, wr_ref[...]).astype(BF16)
        p["pre"] = _dot(r, w2_ref[...]) + gkb_ref[...]
        p["q_a"], p["v_a"] = seg(0, aw), seg(3 * aw, 4 * aw)
        p["q_b"], p["k_b"] = seg(4 * aw, 4 * aw + bw), seg(4 * aw + bw, 4 * aw + 2 * bw)
        p["v_b"] = seg(4 * aw + 2 * bw, 5 * aw + 2 * bw)
    for p in parts:
        i = p["i"]
        for j, z, k_ref, g_ref in ((1, p["z_f"], akf_ref, gf_ref), (2, p["z_b"], akb_ref, gb_ref)):
            lb = lb_ref[j - 1:j, :]
            sg = _sigmoid(z)
            g_ref[i, :, 0:aw] = jnp.log2(lb + (1.0 - lb) * sg)
            k_ref[i] = ((1.0 - lb) * (1.0 - sg)).astype(BF16)
        pre = p["pre"]
        ls = (jnp.minimum(pre, 0.0) - jnp.log(1.0 + jnp.exp(-jnp.abs(pre)))) * (LOG2E / GLA_GATE_NORMALIZER)
        gf_ref[i, :, aw:aw + bw] = ls[:, 0:bw]
        gb_ref[i, :, aw:aw + bw] = ls[:, bw:2 * bw]
        q_ref[i, :, 0:aw] = p["q_a"].astype(BF16)
        q_ref[i, :, aw:aw + bw] = p["q_b"].astype(BF16)
        v_ref[i, :, 0:aw] = p["v_a"].astype(BF16)
        v_ref[i, :, aw:2 * aw] = p["v_b"].astype(BF16)
        bk_ref[i] = p["k_b"].astype(BF16)


def _inproj_call(x, ctx, mod_a, mod_s, w_scan, w_rank, w2, gkb, lb):
    b_, l_, d = x.shape
    lc = ctx.shape[1]
    tm = TOKEN_TILE
    nb = max(k for k in (4, 2, 1) if b_ % k == 0)
    nct, nlt = lc // tm, l_ // tm
    lt = lc + l_

    def x_map(b, t):
        return (b, jnp.maximum(t - nct, 0), 0)

    def ctx_map(b, t):
        return (b, jnp.minimum(t, nct - 1), 0)

    per_b = lambda b, t: (b, 0, 0)
    ctx_row = lambda b, t: (b_, 0, 0)
    const2 = lambda b, t: (0, 0)
    out_map = lambda b, t: (b, t, 0)
    aw, bw, vw = HGRN_HEADS * HGRN_DK, GLA_HEADS * GLA_DK, GLA_HEADS * GLA_DV
    qk_w, v_w = aw + bw, aw + vw

    def out(width, dtype):
        return jax.ShapeDtypeStruct((b_, lt, width), dtype), pl.BlockSpec((nb, tm, width), out_map)

    outs = [out(qk_w, BF16), out(v_w, BF16), out(aw, BF16), out(aw, BF16), out(bw, BF16), out(qk_w, F32), out(qk_w, F32)]
    return pl.pallas_call(
        functools.partial(_inproj_kernel, nct),
        grid=(b_ // nb, nct + nlt),
        in_specs=[
            pl.BlockSpec((nb, tm, d), x_map),
            pl.BlockSpec((nb, tm, d), ctx_map),
            pl.BlockSpec((nb, 1, d), per_b),
            pl.BlockSpec((nb, 1, d), per_b),
            pl.BlockSpec((1, 1, d), ctx_row),
            pl.BlockSpec((1, 1, d), ctx_row),
            pl.BlockSpec(w_scan.shape, const2),
            pl.BlockSpec(w_rank.shape, const2),
            pl.BlockSpec(w2.shape, const2),
            pl.BlockSpec(gkb.shape, const2),
            pl.BlockSpec(lb.shape, const2),
        ],
        out_specs=[o[1] for o in outs],
        out_shape=[o[0] for o in outs],
        compiler_params=pltpu.CompilerParams(
            dimension_semantics=("arbitrary", "arbitrary"), vmem_limit_bytes=VMEM_LIMIT),
    )(x, ctx, mod_a, mod_s, mod_a, mod_s, w_scan, w_rank, w2, gkb, lb)


def _causal(forward):
    row = lax.broadcasted_iota(jnp.int32, (SCAN_CHUNK, SCAN_CHUNK), 0)
    col = lax.broadcasted_iota(jnp.int32, (SCAN_CHUNK, SCAN_CHUNK), 1)
    return (row >= col) if forward else (col >= row)


def _scan_block(with_out, qf_ref, vf_ref, akf_ref, bkf_ref, gf_ref, qb_ref, vb_ref, akb_ref, bkb_ref, gb_ref,
                of_ref, ob_ref, sf_ref, sb_ref):
    c_ = SCAN_CHUNK
    ncb = qf_ref.shape[1] // c_
    aw = HGRN_HEADS * HGRN_DK
    n_pairs = GLA_HEADS // 2
    dirs = []
    for i in range(qf_ref.shape[0]):
        dirs.append((True,) + tuple(r.at[i] for r in (qf_ref, vf_ref, akf_ref, bkf_ref, gf_ref, of_ref, sf_ref)))
        dirs.append((False,) + tuple(r.at[i] for r in (qb_ref, vb_ref, akb_ref, bkb_ref, gb_ref, ob_ref, sb_ref)))
    lane = lax.broadcasted_iota(jnp.int32, (c_, HEAD_W), 1)
    low = lane < GLA_DK
    pr = lax.broadcasted_iota(jnp.int32, (c_, 2 * c_), 0)
    pc = lax.broadcasted_iota(jnp.int32, (c_, 2 * c_), 1) % c_
    pair_causal = {True: pr >= pc, False: pc >= pr}
    srow = lax.broadcasted_iota(jnp.int32, (2 * HEAD_W, HEAD_W), 0) < HEAD_W
    scol = lax.broadcasted_iota(jnp.int32, (2 * HEAD_W, HEAD_W), 1) < GLA_DK
    pair_state = srow == scol
    zero_v = jnp.zeros((c_, HEAD_W), BF16)
    cums = {}
    for di, (fw, _, _, _, _, g_ref, _, _) in enumerate(dirs):
        tri = jnp.where(_causal(fw), 1.0, 0.0).astype(BF16)
        for c in range(ncb):
            g_hi, g_lo = _split_bf16(g_ref[c * c_:(c + 1) * c_, :])
            cums[di, c] = _dot(tri, g_hi) + _dot(tri, g_lo)
    ops = []
    for di, (fw, q_ref, v_ref, ak_ref, bk_ref, _, _, _) in enumerate(dirs):
        end_row, mid_row = (c_ - 1, c_ // 2 - 1) if fw else (0, c_ // 2)
        for c in range(ncb):
            rows = slice(c * c_, (c + 1) * c_)
            cum = cums[di, c]
            total = cum[end_row:end_row + 1, :]
            mid = cum[mid_row:mid_row + 1, :]
            e_mid = jnp.exp2(mid)
            e_rest = jnp.exp2(total - mid)
            e_tot = jnp.exp2(total)
            for u in range(HGRN_HEADS + n_pairs):
                pair = u >= HGRN_HEADS
                sl = slice(u * HEAD_W, (u + 1) * HEAD_W)
                k_src = bk_ref[rows, (u - HGRN_HEADS) * HEAD_W:(u - HGRN_HEADS + 1) * HEAD_W] if pair \
                    else ak_ref[rows, sl]
                qs = q_ref[rows, sl].astype(F32) * jnp.exp2(cum[:, sl] - mid[:, sl])
                ks = k_src.astype(F32) * jnp.exp2(mid[:, sl] - cum[:, sl])
                ksb = ks.astype(BF16)
                if pair:
                    vl = slice(aw + (u - HGRN_HEADS) * 2 * HEAD_W, aw + (u - HGRN_HEADS + 1) * 2 * HEAD_W)
                    v = v_ref[rows, vl]
                    k_rhs = jnp.concatenate([jnp.where(low, ksb, 0), jnp.where(low, 0, ksb)], axis=0)
                    v_rhs = jnp.concatenate([jnp.concatenate([v[:, :HEAD_W], zero_v], axis=1),
                                             jnp.concatenate([zero_v, v[:, HEAD_W:]], axis=1)], axis=0)
                    st_rows = slice(aw + (u - HGRN_HEADS) * 2 * HEAD_W, aw + (u - HGRN_HEADS + 1) * 2 * HEAD_W)
                else:
                    vl = sl
                    v = v_ref[rows, vl]
                    k_rhs, v_rhs = ksb, v
                    st_rows = sl
                ops.append(dict(di=di, fw=fw, c=c, pair=pair, rows=rows, vl=vl, st_rows=st_rows, v=v,
                                k_rhs=k_rhs, v_rhs=v_rhs, qs=qs.astype(BF16), qd=(qs * e_mid[:, sl]).astype(BF16),
                                kd=(ks * e_rest[:, sl]).astype(BF16), e_tot=e_tot[:, sl]))
    if with_out:
        for d in ops:
            d["sc"] = _dot_nt(d["qs"], d["k_rhs"])
        for d in ops:
            mask = pair_causal[d["fw"]] if d["pair"] else _causal(d["fw"])
            d["sc"] = jnp.where(mask, d["sc"], 0.0).astype(BF16)
        for d in ops:
            d["o"] = _dot(d["sc"], d["v_rhs"])
    for d in ops:
        d["upd"] = _dot_tn(d["v"], d["kd"])
    for d in ops:
        if d["pair"]:
            d["upd"] = jnp.where(pair_state, d["upd"], 0.0)
    for di, (fw, _, _, _, _, _, o_ref, st_ref) in enumerate(dirs):
        sts = {}
        for c in (range(ncb) if fw else range(ncb - 1, -1, -1)):
            for d in ops:
                if d["di"] != di or d["c"] != c:
                    continue
                key = d["st_rows"].start
                if key not in sts:
                    sts[key] = st_ref[d["st_rows"], :]
                if with_out:
                    o = d["o"] + _dot_nt(d["qd"], sts[key].astype(BF16))
                    o_ref[d["rows"], d["vl"]] = o.astype(o_ref.dtype)
                sts[key] = sts[key] * d["e_tot"] + d["upd"]
        for d in ops:
            if d["di"] == di and d["c"] == 0:
                st_ref[d["st_rows"], :] = sts[d["st_rows"].start]


def _scan_kernel(n_ctx_blocks, *refs):
    s = pl.program_id(1)

    @pl.when(s == 0)
    def _():
        refs[-2][...] = jnp.zeros_like(refs[-2])
        refs[-1][...] = jnp.zeros_like(refs[-1])

    @pl.when(s < n_ctx_blocks)
    def _():
        _scan_block(False, *refs)

    @pl.when(s >= n_ctx_blocks)
    def _():
        _scan_block(True, *refs)


def _scan_call(q, v, akf, akb, bk, gf, gb, lc):
    b_, lt, _ = q.shape
    c_ = SCAN_BLOCK
    ncc = lc // c_
    nlc = (lt - lc) // c_
    n = ncc + nlc

    def fwd(b, s):
        return (b, s, 0)

    def bwd(b, s):
        return (b, jnp.where(s < ncc, ncc - 1 - s, n + ncc - 1 - s), 0)

    def out_fwd(b, s):
        return (b, jnp.maximum(s - ncc, 0), 0)

    def out_bwd(b, s):
        return (b, jnp.where(s < ncc, nlc - 1, n - 1 - s), 0)

    nb = 2 if b_ % 2 == 0 else 1

    def specs(index_map):
        return [pl.BlockSpec((nb, c_, a.shape[-1]), index_map) for a in (q, v, akf, bk, gf)]

    v_w = v.shape[-1]
    out = jax.ShapeDtypeStruct((b_, lt - lc, v_w), BF16)
    state = pltpu.VMEM((nb, v_w, HEAD_W), F32)
    return pl.pallas_call(
        functools.partial(_scan_kernel, ncc),
        grid=(b_ // nb, n),
        in_specs=specs(fwd) + specs(bwd),
        out_specs=[pl.BlockSpec((nb, c_, v_w), out_fwd), pl.BlockSpec((nb, c_, v_w), out_bwd)],
        out_shape=[out, out],
        scratch_shapes=[state, state],
        compiler_params=pltpu.CompilerParams(
            dimension_semantics=("arbitrary", "arbitrary"), vmem_limit_bytes=VMEM_LIMIT),
    )(q, v, akf, bk, gf, q, v, akb, bk, gb)


def _merge_kernel(x_ref, a1_ref, s1_ref, gt1_ref, a2_ref, s2_ref, of_ref, ob_ref, wg_ref, nw_ref,
                  wua_ref, wub_ref, wo_ref, gpost_ref, wr_ref, br_ref,
                  x1_ref, h2_ref, idx_ref, prob_ref):
    tm, d = x_ref.shape[1:]
    hr = tm // 2
    nslab = d // LANES
    ow = N_HEADS * HEAD_W
    aw = HGRN_HEADS * HEAD_W
    halves = [dict(i=i, rows=slice(i * hr, (i + 1) * hr)) for i in range(2)]
    for h in halves:
        h["x"] = x_ref[0, h["rows"], :]
        h["hb"] = (_rms(h["x"]) * a1_ref[0] + s1_ref[0]).astype(BF16)
    for h in halves:
        h["og"] = _dot(h["hb"], wg_ref[:, 0:ow])
        h["mg_a"] = _dot(h["hb"], wg_ref[:, ow:ow + d])
        h["mg_b"] = _dot(h["hb"], wg_ref[:, ow + d:ow + 2 * d])
    for h in halves:
        o = of_ref[0, h["rows"], :].astype(F32) + ob_ref[0, h["rows"], :].astype(F32)
        heads = [_rms(o[:, j * HEAD_W:(j + 1) * HEAD_W]) for j in range(N_HEADS)]
        og = h["og"]
        h["r"] = (jnp.concatenate(heads, axis=-1) * nw_ref[...] * (og * _sigmoid(og))).astype(BF16)
    for h in halves:
        h["y_a"] = _dot(h["r"][:, 0:aw], wua_ref[...])
        h["y_b"] = _dot(h["r"][:, aw:ow], wub_ref[...])
    for h in halves:
        h["m"] = (_sigmoid(h["mg_a"]) * h["y_a"] + _sigmoid(h["mg_b"]) * h["y_b"]).astype(BF16)
    for h in halves:
        h["mix"] = _dot(h["m"], wo_ref[...])
    for h in halves:
        x1 = h["x"] + gt1_ref[0] * (_rms(h["mix"]) * gpost_ref[...])
        x1_ref[0, h["rows"], :] = x1
        h2 = _rms(x1) * a2_ref[0] + s2_ref[0]
        for s in range(nslab):
            h2_ref[pl.ds(h["i"] * hr * nslab + s, hr, stride=nslab), :] = h2[:, s * LANES:(s + 1) * LANES]
        h["h_hi"], h["h_lo"] = _split_bf16(h2)
    for h in halves:
        h["logits"] = (_dot(h["h_hi"], wr_ref[0]) + _dot(h["h_lo"], wr_ref[0]) + _dot(h["h_hi"], wr_ref[1])
                       + br_ref[...])
    for h in halves:
        logits = h["logits"]
        lane = lax.broadcasted_iota(jnp.int32, logits.shape, 1).astype(F32)
        vals, idxs = [], []
        for _ in range(TOP_K):
            m = jnp.max(logits, axis=-1, keepdims=True)
            sel = jnp.min(jnp.where(logits == m, lane, float(LANES)), axis=-1, keepdims=True)
            vals.append(m)
            idxs.append(sel)
            logits = jnp.where(lane == sel, -jnp.inf, logits)
        exps = [jnp.exp(v_ - vals[0]) for v_ in vals]
        denom = exps[0] + exps[1] + exps[2] + exps[3]
        idx_out = jnp.zeros_like(lane)
        prob_out = jnp.zeros_like(lane)
        for k_ in range(TOP_K):
            idx_out = jnp.where(lane == float(k_), idxs[k_], idx_out)
            prob_out = jnp.where(lane == float(k_), exps[k_] / denom, prob_out)
        idx_ref[:, h["rows"]] = jnp.transpose(idx_out)[0:SUBLANES, :].astype(jnp.int32)
        prob_ref[0, h["rows"], :] = prob_out


def _merge_call(x, a1, s1, gt1, a2, s2, o_f, o_b, w_gates, norm_w, w_up_a, w_up_b, w_o, g_post, w_r, b_r):
    b_, l_, d = x.shape
    tm = LATENT_TILE
    tile = lambda b, t: (b, t, 0)
    per_b = lambda b, t: (b, 0, 0)
    const2 = lambda b, t: (0, 0)
    return pl.pallas_call(
        _merge_kernel,
        grid=(b_, l_ // tm),
        in_specs=[
            pl.BlockSpec((1, tm, d), tile),
            pl.BlockSpec((1, 1, d), per_b), pl.BlockSpec((1, 1, d), per_b), pl.BlockSpec((1, 1, d), per_b),
            pl.BlockSpec((1, 1, d), per_b), pl.BlockSpec((1, 1, d), per_b),
            pl.BlockSpec((1, tm, o_f.shape[-1]), tile), pl.BlockSpec((1, tm, o_b.shape[-1]), tile),
            pl.BlockSpec(w_gates.shape, const2),
            pl.BlockSpec(norm_w.shape, const2),
            pl.BlockSpec(w_up_a.shape, const2),
            pl.BlockSpec(w_up_b.shape, const2),
            pl.BlockSpec(w_o.shape, const2),
            pl.BlockSpec(g_post.shape, const2),
            pl.BlockSpec(w_r.shape, lambda b, t: (0, 0, 0)),
            pl.BlockSpec(b_r.shape, const2),
        ],
        out_specs=[
            pl.BlockSpec((1, tm, d), tile),
            pl.BlockSpec((tm * (d // LANES), LANES), lambda b, t: (b * (l_ // tm) + t, 0)),
            pl.BlockSpec((SUBLANES, tm), lambda b, t: (0, b * (l_ // tm) + t)), pl.BlockSpec((1, tm, LANES), tile),
        ],
        out_shape=[
            jax.ShapeDtypeStruct((b_, l_, d), F32), jax.ShapeDtypeStruct((b_ * l_ * (d // LANES), LANES), F32),
            jax.ShapeDtypeStruct((SUBLANES, b_ * l_), jnp.int32), jax.ShapeDtypeStruct((b_, l_, LANES), F32),
        ],
        compiler_params=pltpu.CompilerParams(
            dimension_semantics=("arbitrary", "arbitrary"), vmem_limit_bytes=VMEM_LIMIT),
    )(x, a1, s1, gt1, a2, s2, o_f, o_b, w_gates, norm_w, w_up_a, w_up_b, w_o, g_post, w_r, b_r)


def _moe_kernel(be_ref, nv_ref, tok_ref, tokn_ref, dst_ref, h2_hbm,
                wg_ref, bg_ref, wu_ref, bu_ref, wd_ref, bd_ref, out_hbm,
                xbuf, ybuf, wgb, wub, wdb, gsem, ssem):
    i = pl.program_id(0)
    n = pl.num_programs(0)
    slot = i % 2
    rows = MOE_BLOCK
    nslab = xbuf.shape[0] // (2 * rows)
    blk = rows * nslab

    def slab(ref, start):
        return ref.at[pl.ds(pl.multiple_of(start, nslab), nslab), :]

    def for_slot(fn):
        for s_static in range(2):
            @pl.when(slot == s_static)
            def _():
                fn(s_static)

    def issue_gather(idx_ref, s):
        for r in range(rows):
            pltpu.make_async_copy(slab(h2_hbm, idx_ref[0, 0, r]), xbuf.at[pl.ds(s * blk + r * nslab, nslab), :],
                                  gsem.at[s]).start(priority=r % 2)

    def wait_gather(s):
        pltpu.make_async_copy(h2_hbm.at[pl.ds(0, blk), :], xbuf.at[pl.ds(pl.multiple_of(s * blk, blk), blk), :],
                              gsem.at[s]).wait()

    def issue_scatter(s):
        for r in range(rows):
            pltpu.make_async_copy(ybuf.at[pl.ds(s * blk + r * nslab, nslab), :], slab(out_hbm, dst_ref[0, 0, r]),
                                  ssem.at[s]).start(priority=r % 2)

    def wait_scatter(s):
        pltpu.make_async_copy(ybuf.at[pl.ds(pl.multiple_of(s * blk, blk), blk), :], out_hbm.at[pl.ds(0, blk), :],
                              ssem.at[s]).wait()

    @pl.when(jnp.logical_and(i == 0, nv_ref[0] > 0))
    def _():
        issue_gather(tok_ref, 0)

    nxt = jnp.minimum(i + 1, n - 1)

    @pl.when(jnp.logical_and(i + 1 < n, nv_ref[nxt] > 0))
    def _():
        for_slot(lambda s: issue_gather(tokn_ref, 1 - s))

    prev = jnp.maximum(i - 1, 0)

    @pl.when(jnp.logical_or(i == 0, be_ref[i] != be_ref[prev]))
    def _():
        wgb[...] = wg_ref[0].astype(BF16)
        wub[...] = wu_ref[0].astype(BF16)
        wdb[...] = wd_ref[0].astype(BF16)

    @pl.when(jnp.logical_and(i >= 2, nv_ref[jnp.maximum(i - 2, 0)] > 0))
    def _():
        wait_scatter(slot)

    @pl.when(nv_ref[i] > 0)
    def _():
        wait_gather(slot)
        base = slot * blk
        xb = jnp.concatenate([xbuf[pl.ds(base + s, rows, stride=nslab), :] for s in range(nslab)],
                             axis=-1).astype(BF16)
        g = jnp.minimum(_dot(xb, wgb[...]) + bg_ref[0], SWIGLU_LIMIT)
        u = jnp.clip(_dot(xb, wub[...]) + bu_ref[0], -SWIGLU_LIMIT, SWIGLU_LIMIT)
        act = g * _sigmoid(SWIGLU_ALPHA * g) * (u + 1.0)
        y = _dot(act.astype(BF16), wdb[...]) + bd_ref[0]
        for s in range(nslab):
            ybuf[pl.ds(base + s, rows, stride=nslab), :] = y[:, s * LANES:(s + 1) * LANES]
        for_slot(issue_scatter)

    @pl.when(nv_ref[i] == 0)
    def _():
        ybuf[pl.ds(pl.multiple_of(slot * blk, blk), blk), :] = jnp.zeros((blk, LANES), F32)
        fill = pltpu.make_async_copy(ybuf.at[pl.ds(pl.multiple_of(slot * blk, blk), blk), :],
                                     out_hbm.at[pl.ds(pl.multiple_of(dst_ref[0, 0, 0], nslab), blk), :], ssem.at[slot])
        fill.start()
        fill.wait()

    @pl.when(i == n - 1)
    def _():
        @pl.when(jnp.logical_and(i >= 1, nv_ref[prev] > 0))
        def _():
            wait_scatter(1 - slot)

        @pl.when(nv_ref[i] > 0)
        def _():
            wait_scatter(slot)


def _moe_call(block_expert, n_valid, tok, dst, h2, n_out_rows, w_gate, b_gate, w_up, b_up, w_down, b_down):
    e_, d, f = w_gate.shape
    nslab = d // LANES
    nblk = tok.shape[0]
    rows = MOE_BLOCK

    def blk(i, be, nv):
        return (i, 0, 0)

    def blk_next(i, be, nv):
        return (jnp.minimum(i + 1, nblk - 1), 0, 0)

    def expert(i, be, nv):
        return (be[i], 0, 0)

    smem_rows = functools.partial(pl.BlockSpec, (1, 1, rows), memory_space=pltpu.SMEM)
    grid_spec = pltpu.PrefetchScalarGridSpec(
        num_scalar_prefetch=2,
        grid=(nblk,),
        in_specs=[
            smem_rows(blk), smem_rows(blk_next), smem_rows(blk),
            pl.BlockSpec(memory_space=pl.ANY),
            pl.BlockSpec((1, d, f), expert), pl.BlockSpec((1, 1, f), expert),
            pl.BlockSpec((1, d, f), expert), pl.BlockSpec((1, 1, f), expert),
            pl.BlockSpec((1, f, d), expert), pl.BlockSpec((1, 1, d), expert),
        ],
        out_specs=pl.BlockSpec(memory_space=pl.ANY),
        scratch_shapes=[
            pltpu.VMEM((2 * rows * nslab, LANES), F32), pltpu.VMEM((2 * rows * nslab, LANES), F32),
            pltpu.VMEM((d, f), BF16), pltpu.VMEM((d, f), BF16), pltpu.VMEM((f, d), BF16),
            pltpu.SemaphoreType.DMA((2,)), pltpu.SemaphoreType.DMA((2,)),
        ],
    )
    return pl.pallas_call(
        _moe_kernel,
        grid_spec=grid_spec,
        out_shape=jax.ShapeDtypeStruct((n_out_rows * nslab, LANES), F32),
        compiler_params=pltpu.CompilerParams(
            dimension_semantics=("arbitrary",), vmem_limit_bytes=VMEM_LIMIT),
    )(block_expert, n_valid, tok, tok, dst, h2,
      w_gate, b_gate.reshape(e_, 1, f), w_up, b_up.reshape(e_, 1, f), w_down, b_down.reshape(e_, 1, d))


def _dispatch_plan(top_idx, n_experts, nslab):
    t_ = top_idx.shape[1]
    n_pairs = t_ * TOP_K
    flat_e = top_idx.reshape(-1)
    pair_bits = max(n_pairs - 1, 1).bit_length()
    keys = jnp.sort(flat_e * (1 << pair_bits) + jnp.arange(n_pairs, dtype=jnp.int32))
    order = keys & ((1 << pair_bits) - 1)
    counts = jnp.sum((flat_e[:, None] == jnp.arange(n_experts, dtype=jnp.int32)[None, :]).astype(jnp.int32), axis=0)
    nblk_e = (counts + MOE_BLOCK - 1) // MOE_BLOCK
    blk_end = jnp.cumsum(nblk_e)
    blk_start = blk_end - nblk_e
    start = jnp.cumsum(counts) - counts
    nblk = n_pairs // MOE_BLOCK + n_experts
    b = jnp.arange(nblk, dtype=jnp.int32)
    be = jnp.minimum(jnp.sum((b[:, None] >= blk_end[None, :]).astype(jnp.int32), axis=1), n_experts - 1)
    first_row = (b - blk_start[be]) * MOE_BLOCK
    n_valid = jnp.clip(counts[be] - first_row, 0, MOE_BLOCK).astype(jnp.int32)
    r = jnp.arange(MOE_BLOCK, dtype=jnp.int32)[None, :]
    valid = r < n_valid[:, None]
    src = jnp.clip(start[be][:, None] + first_row[:, None] + r, 0, n_pairs - 1)
    pair = jnp.where(valid, order[src], 0).astype(jnp.int32)
    tok = pair % t_
    spare = (jnp.cumsum((~valid).reshape(-1).astype(jnp.int32)) - 1).reshape(nblk, MOE_BLOCK)
    dst = jnp.where(valid, pair, n_pairs + spare).astype(jnp.int32)
    n_out_rows = n_pairs + n_experts * MOE_BLOCK
    return (be, n_valid, (tok * nslab).reshape(nblk, 1, MOE_BLOCK), (dst * nslab).reshape(nblk, 1, MOE_BLOCK),
            n_out_rows)


def _final_kernel(x1_ref, y0_ref, y1_ref, y2_ref, y3_ref, p_ref, gt2_ref, gpost_ref, o_ref):
    p = p_ref[0]
    tm, d = x1_ref.shape[1:]
    nslab = d // LANES
    parts = []
    for s in range(nslab):
        acc = p[:, 0:1] * y0_ref[pl.ds(s, tm, stride=nslab), :]
        for k_, y_ref in ((1, y1_ref), (2, y2_ref), (3, y3_ref)):
            acc = acc + p[:, k_:k_ + 1] * y_ref[pl.ds(s, tm, stride=nslab), :]
        parts.append(acc)
    ffn = jnp.concatenate(parts, axis=-1)
    o_ref[0] = x1_ref[0] + gt2_ref[0] * (_rms(ffn) * gpost_ref[...])


def _final_call(x1, y, probs, gt2, g_post):
    b_, l_, d = x1.shape
    tm = LATENT_TILE
    nslab = d // LANES
    tiles_per_k = b_ * l_ // tm
    tile = lambda b, t: (b, t, 0)

    def y_spec(k_):
        return pl.BlockSpec((tm * nslab, LANES), lambda b, t: (k_ * tiles_per_k + b * (l_ // tm) + t, 0))

    return pl.pallas_call(
        _final_kernel,
        grid=(b_, l_ // tm),
        in_specs=[
            pl.BlockSpec((1, tm, d), tile),
            y_spec(0), y_spec(1), y_spec(2), y_spec(3),
            pl.BlockSpec((1, tm, LANES), tile),
            pl.BlockSpec((1, 1, d), lambda b, t: (b, 0, 0)),
            pl.BlockSpec(g_post.shape, lambda b, t: (0, 0)),
        ],
        out_specs=pl.BlockSpec((1, tm, d), tile),
        out_shape=jax.ShapeDtypeStruct((b_, l_, d), F32),
        compiler_params=pltpu.CompilerParams(
            dimension_semantics=("arbitrary", "arbitrary"), vmem_limit_bytes=VMEM_LIMIT),
    )(x1, y, y, y, y, probs, gt2, g_post)


def kernel(x, c, ctx, c_ctx, w_ada, b_ada, g_pre_mix, g_post_mix, g_pre_ffn, g_post_ffn, w_in, hgrn_lb, hgrn_norm_w, gla_gk_w2, gla_gk_b, gla_norm_w, w_up_a, w_up_b, w_o, w_router, b_router, w_gate, b_gate, w_up, b_up, w_down, b_down):
    b_, l_, d = x.shape
    lc = ctx.shape[1]
    n_experts = w_router.shape[-1]
    assert w_ada.shape[0] == 1, "one layer: the context stream is never updated"
    assert l_ % LATENT_TILE == 0 and lc % SCAN_BLOCK == 0 and (b_ * l_ * TOP_K) % MOE_BLOCK == 0
    assert n_experts <= LANES and d % LANES == 0
    layer = 0
    hw = HGRN_HEADS * HGRN_DK
    kw = GLA_HEADS * GLA_DK
    vw = GLA_HEADS * GLA_DV
    rk = GLA_GATE_RANK

    rows = -(-(b_ + 1) // SUBLANES) * SUBLANES
    cc = jnp.concatenate([c, c_ctx[None, :], jnp.zeros((rows - b_ - 1, d), F32)], axis=0)
    mod = _ada_call(cc, w_ada[layer], b_ada[layer])[:b_ + 1]
    sh1, sc1, gt1, sh2, sc2, gt2 = [m.reshape(b_ + 1, 1, d) for m in jnp.split(mod, 6, axis=-1)]
    a1 = g_pre_mix[layer] * (1.0 + sc1)
    a2 = g_pre_ffn[layer] * (1.0 + sc2)

    w = w_in[layer]
    o0 = 0
    cols = {}
    for name, size in (("qa", hw), ("zf", hw), ("zb", hw), ("ia", hw), ("oga", hw), ("qb", kw), ("kb", kw),
                       ("vb", vw), ("rf", rk), ("rb", rk), ("ogb", vw), ("mga", d), ("mgb", d)):
        cols[name] = w[:, o0:o0 + size]
        o0 += size
    w_scan = jnp.concatenate([
        cols["qa"], cols["zf"], cols["zb"], cols["ia"],
        cols["qb"] * (GLA_DK ** -0.5), cols["kb"], cols["vb"]], axis=1).astype(BF16)
    w_rank = jnp.pad(jnp.concatenate([cols["rf"], cols["rb"]], axis=1), ((0, 0), (0, LANES - 2 * rk))).astype(BF16)
    w2 = jnp.zeros((LANES, 2 * kw), F32)
    w2 = w2.at[0:rk, 0:kw].set(gla_gk_w2[layer, 0])
    w2 = w2.at[rk:2 * rk, kw:].set(gla_gk_w2[layer, 1])
    w2 = w2.astype(BF16)
    gkb = jnp.concatenate([gla_gk_b[layer, 0], gla_gk_b[layer, 1]])[None, :]
    lb = jnp.cumsum(jax.nn.softmax(hgrn_lb.astype(F32), axis=0), axis=0)[layer]
    w_gates = jnp.concatenate([cols["oga"], cols["ogb"], cols["mga"], cols["mgb"]], axis=1).astype(BF16)
    norm_w = jnp.concatenate([jnp.tile(hgrn_norm_w[layer], HGRN_HEADS), jnp.tile(gla_norm_w[layer], GLA_HEADS)])[None, :]
    w_r = jnp.stack(_split_bf16(jnp.pad(w_router[layer], ((0, 0), (0, LANES - n_experts)))))
    b_r = jnp.pad(b_router[layer], (0, LANES - n_experts), constant_values=-1e30)[None, :]

    q, v, akf, akb, bk, gf, gb = _inproj_call(x, ctx, a1, sh1, w_scan, w_rank, w2, gkb, lb)
    o_f, o_b = _scan_call(q, v, akf, akb, bk, gf, gb, lc)
    x1, h2, top_idx, probs = _merge_call(
        x, a1, sh1, gt1, a2, sh2, o_f, o_b, w_gates, norm_w,
        w_up_a[layer].astype(BF16), w_up_b[layer].astype(BF16), w_o[layer].astype(BF16),
        g_post_mix[layer][None, :], w_r, b_r)

    t_ = b_ * l_
    be, n_valid, tok, dst, n_out_rows = _dispatch_plan(top_idx[:TOP_K], n_experts, d // LANES)
    y = _moe_call(be, n_valid, tok, dst, h2, n_out_rows,
                  w_gate[layer], b_gate[layer], w_up[layer], b_up[layer], w_down[layer], b_down[layer])
    return _final_call(x1, y, probs, gt2, g_post_ffn[layer][None, :])
```

```python
import functools

import jax
import jax.numpy as jnp
from jax import lax
from jax.experimental import pallas as pl
from jax.experimental.pallas import tpu as pltpu

F32 = jnp.float32
BF16 = jnp.bfloat16

NORM_EPS = 1e-6
HGRN_HEADS = 4
HGRN_DK = 128
GLA_HEADS = 4
GLA_DK = 64
GLA_DV = 128
GLA_GATE_RANK = 16
GLA_GATE_NORMALIZER = 16.0
N_HEADS = HGRN_HEADS + GLA_HEADS
HEAD_W = 128
TOP_K = 4
SWIGLU_LIMIT = 7.0
SWIGLU_ALPHA = 1.702
MOE_BLOCK = 256
SCAN_CHUNK = 64
SCAN_BLOCK = 256
LOG2E = 1.4426950408889634
TOKEN_TILE = 256
LATENT_TILE = 512
ADA_TILE_N = 1536
LANES = 128
SUBLANES = 8
VMEM_LIMIT = 56 * 1024 * 1024


def _dot(a, b):
    return jnp.dot(a, b, preferred_element_type=F32)


def _dot_nt(a, b):
    return lax.dot_general(a, b, (((1,), (1,)), ((), ())), preferred_element_type=F32)


def _dot_tn(a, b):
    return lax.dot_general(a, b, (((0,), (0,)), ((), ())), preferred_element_type=F32)


def _split_bf16(a):
    hi = a.astype(BF16)
    lo = (a - hi.astype(F32)).astype(BF16)
    return hi, lo


def _dot3(a, b):
    ah, al = _split_bf16(a)
    bh, bl = _split_bf16(b)
    return _dot(ah, bh) + _dot(al, bh) + _dot(ah, bl)


def _rms(x):
    return x * lax.rsqrt(jnp.mean(x * x, axis=-1, keepdims=True) + NORM_EPS)


def _sigmoid(x):
    return 1.0 / (1.0 + jnp.exp(-x))


def _ada_kernel(c_ref, w_ref, b_ref, o_ref):
    c = c_ref[...]
    o_ref[...] = _dot3(c * _sigmoid(c), w_ref[...]) + b_ref[...]


def _ada_call(cc, w_ada, b_ada):
    rows, d = cc.shape
    n = w_ada.shape[1]
    tn = ADA_TILE_N
    return pl.pallas_call(
        _ada_kernel,
        grid=(n // tn,),
        in_specs=[
            pl.BlockSpec((rows, d), lambda j: (0, 0)),
            pl.BlockSpec((d, tn), lambda j: (0, j)),
            pl.BlockSpec((1, tn), lambda j: (0, j)),
        ],
        out_specs=pl.BlockSpec((rows, tn), lambda j: (0, j)),
        out_shape=jax.ShapeDtypeStruct((rows, n), F32),
        compiler_params=pltpu.CompilerParams(
            dimension_semantics=("arbitrary",), vmem_limit_bytes=VMEM_LIMIT),
    )(cc, w_ada, b_ada.reshape(1, n))


def _inproj_kernel(n_ctx_tiles, x_ref, ctx_ref, a_ref, s_ref, actx_ref, sctx_ref, w_ref, wr_ref, w2_ref, gkb_ref,
                   lb_ref, q_ref, v_ref, akf_ref, akb_ref, bk_ref, gf_ref, gb_ref):
    t = pl.program_id(1)
    is_ctx = t < n_ctx_tiles
    aw = HGRN_HEADS * HGRN_DK
    bw = GLA_HEADS * GLA_DK
    parts = [dict(i=i) for i in range(x_ref.shape[0])]
    for p in parts:
        i = p["i"]
        xt = jnp.where(is_ctx, ctx_ref[i], x_ref[i])
        a = jnp.where(is_ctx, actx_ref[0], a_ref[i])
        s = jnp.where(is_ctx, sctx_ref[0], s_ref[i])
        p["hb"] = (_rms(xt) * a + s).astype(BF16)
    for p in parts:
        hb = p["hb"]

        def seg(lo, hi):
            return _dot(hb, w_ref[:, lo:hi])

        p["z_f"], p["z_b"] = seg(aw, 2 * aw), seg(2 * aw, 3 * aw)
        r = _dot(hb, wr_ref[...]).astype(BF16)
        p["pre"] = _dot(r, w2_ref[...]) + gkb_ref[...]
        p["q_a"], p["v_a"] = seg(0, aw), seg(3 * aw, 4 * aw)
        p["q_b"], p["k_b"] = seg(4 * aw, 4 * aw + bw), seg(4 * aw + bw, 4 * aw + 2 * bw)
        p["v_b"] = seg(4 * aw + 2 * bw, 5 * aw + 2 * bw)
    for p in parts:
        i = p["i"]
        for j, z, k_ref, g_ref in ((1, p["z_f"], akf_ref, gf_ref), (2, p["z_b"], akb_ref, gb_ref)):
            lb = lb_ref[j - 1:j, :]
            sg = _sigmoid(z)
            g_ref[i, :, 0:aw] = jnp.log2(lb + (1.0 - lb) * sg)
            k_ref[i] = ((1.0 - lb) * (1.0 - sg)).astype(BF16)
        pre = p["pre"]
        ls = (jnp.minimum(pre, 0.0) - jnp.log(1.0 + jnp.exp(-jnp.abs(pre)))) * (LOG2E / GLA_GATE_NORMALIZER)
        gf_ref[i, :, aw:aw + bw] = ls[:, 0:bw]
        gb_ref[i, :, aw:aw + bw] = ls[:, bw:2 * bw]
        q_ref[i, :, 0:aw] = p["q_a"].astype(BF16)
        q_ref[i, :, aw:aw + bw] = p["q_b"].astype(BF16)
        v_ref[i, :, 0:aw] = p["v_a"].astype(BF16)
        v_ref[i, :, aw:2 * aw] = p["v_b"].astype(BF16)
        bk_ref[i] = p["k_b"].astype(BF16)


def _inproj_call(x, ctx, mod_a, mod_s, w_scan, w_rank, w2, gkb, lb):
    b_, l_, d = x.shape
    lc = ctx.shape[1]
    tm = TOKEN_TILE
    nb = max(k for k in (4, 2, 1) if b_ % k == 0)
    nct, nlt = lc // tm, l_ // tm
    lt = lc + l_

    def x_map(b, t):
        return (b, jnp.maximum(t - nct, 0), 0)

    def ctx_map(b, t):
        return (b, jnp.minimum(t, nct - 1), 0)

    per_b = lambda b, t: (b, 0, 0)
    ctx_row = lambda b, t: (b_, 0, 0)
    const2 = lambda b, t: (0, 0)
    out_map = lambda b, t: (b, t, 0)
    aw, bw, vw = HGRN_HEADS * HGRN_DK, GLA_HEADS * GLA_DK, GLA_HEADS * GLA_DV
    qk_w, v_w = aw + bw, aw + vw

    def out(width, dtype):
        return jax.ShapeDtypeStruct((b_, lt, width), dtype), pl.BlockSpec((nb, tm, width), out_map)

    outs = [out(qk_w, BF16), out(v_w, BF16), out(aw, BF16), out(aw, BF16), out(bw, BF16), out(qk_w, F32), out(qk_w, F32)]
    return pl.pallas_call(
        functools.partial(_inproj_kernel, nct),
        grid=(b_ // nb, nct + nlt),
        in_specs=[
            pl.BlockSpec((nb, tm, d), x_map),
            pl.BlockSpec((nb, tm, d), ctx_map),
            pl.BlockSpec((nb, 1, d), per_b),
            pl.BlockSpec((nb, 1, d), per_b),
            pl.BlockSpec((1, 1, d), ctx_row),
            pl.BlockSpec((1, 1, d), ctx_row),
            pl.BlockSpec(w_scan.shape, const2),
            pl.BlockSpec(w_rank.shape, const2),
            pl.BlockSpec(w2.shape, const2),
            pl.BlockSpec(gkb.shape, const2),
            pl.BlockSpec(lb.shape, const2),
        ],
        out_specs=[o[1] for o in outs],
        out_shape=[o[0] for o in outs],
        compiler_params=pltpu.CompilerParams(
            dimension_semantics=("arbitrary", "arbitrary"), vmem_limit_bytes=VMEM_LIMIT),
    )(x, ctx, mod_a, mod_s, mod_a, mod_s, w_scan, w_rank, w2, gkb, lb)


def _causal(forward):
    row = lax.broadcasted_iota(jnp.int32, (SCAN_CHUNK, SCAN_CHUNK), 0)
    col = lax.broadcasted_iota(jnp.int32, (SCAN_CHUNK, SCAN_CHUNK), 1)
    return (row >= col) if forward else (col >= row)


def _scan_block(with_out, qf_ref, vf_ref, akf_ref, bkf_ref, gf_ref, qb_ref, vb_ref, akb_ref, bkb_ref, gb_ref,
                of_ref, ob_ref, sf_ref, sb_ref):
    c_ = SCAN_CHUNK
    ncb = qf_ref.shape[1] // c_
    aw = HGRN_HEADS * HGRN_DK
    n_pairs = GLA_HEADS // 2
    dirs = []
    for i in range(qf_ref.shape[0]):
        dirs.append((True,) + tuple(r.at[i] for r in (qf_ref, vf_ref, akf_ref, bkf_ref, gf_ref, of_ref, sf_ref)))
        dirs.append((False,) + tuple(r.at[i] for r in (qb_ref, vb_ref, akb_ref, bkb_ref, gb_ref, ob_ref, sb_ref)))
    lane = lax.broadcasted_iota(jnp.int32, (c_, HEAD_W), 1)
    low = lane < GLA_DK
    pr = lax.broadcasted_iota(jnp.int32, (c_, 2 * c_), 0)
    pc = lax.broadcasted_iota(jnp.int32, (c_, 2 * c_), 1) % c_
    pair_causal = {True: pr >= pc, False: pc >= pr}
    srow = lax.broadcasted_iota(jnp.int32, (2 * HEAD_W, HEAD_W), 0) < HEAD_W
    scol = lax.broadcasted_iota(jnp.int32, (2 * HEAD_W, HEAD_W), 1) < GLA_DK
    pair_state = srow == scol
    zero_v = jnp.zeros((c_, HEAD_W), BF16)
    cums = {}
    for di, (fw, _, _, _, _, g_ref, _, _) in enumerate(dirs):
        tri = jnp.where(_causal(fw), 1.0, 0.0).astype(BF16)
        for c in range(ncb):
            g_hi, g_lo = _split_bf16(g_ref[c * c_:(c + 1) * c_, :])
            cums[di, c] = _dot(tri, g_hi) + _dot(tri, g_lo)
    ops = []
    for di, (fw, q_ref, v_ref, ak_ref, bk_ref, _, _, _) in enumerate(dirs):
        end_row, mid_row = (c_ - 1, c_ // 2 - 1) if fw else (0, c_ // 2)
        for c in range(ncb):
            rows = slice(c * c_, (c + 1) * c_)
            cum = cums[di, c]
            total = cum[end_row:end_row + 1, :]
            mid = cum[mid_row:mid_row + 1, :]
            e_mid = jnp.exp2(mid)
            e_rest = jnp.exp2(total - mid)
            e_tot = jnp.exp2(total)
            for u in range(HGRN_HEADS + n_pairs):
                pair = u >= HGRN_HEADS
                sl = slice(u * HEAD_W, (u + 1) * HEAD_W)
                k_src = bk_ref[rows, (u - HGRN_HEADS) * HEAD_W:(u - HGRN_HEADS + 1) * HEAD_W] if pair \
                    else ak_ref[rows, sl]
                qs = q_ref[rows, sl].astype(F32) * jnp.exp2(cum[:, sl] - mid[:, sl])
                ks = k_src.astype(F32) * jnp.exp2(mid[:, sl] - cum[:, sl])
                ksb = ks.astype(BF16)
                if pair:
                    vl = slice(aw + (u - HGRN_HEADS) * 2 * HEAD_W, aw + (u - HGRN_HEADS + 1) * 2 * HEAD_W)
                    v = v_ref[rows, vl]
                    k_rhs = jnp.concatenate([jnp.where(low, ksb, 0), jnp.where(low, 0, ksb)], axis=0)
                    v_rhs = jnp.concatenate([jnp.concatenate([v[:, :HEAD_W], zero_v], axis=1),
                                             jnp.concatenate([zero_v, v[:, HEAD_W:]], axis=1)], axis=0)
                    st_rows = slice(aw + (u - HGRN_HEADS) * 2 * HEAD_W, aw + (u - HGRN_HEADS + 1) * 2 * HEAD_W)
                else:
                    vl = sl
                    v = v_ref[rows, vl]
                    k_rhs, v_rhs = ksb, v
                    st_rows = sl
                ops.append(dict(di=di, fw=fw, c=c, pair=pair, rows=rows, vl=vl, st_rows=st_rows, v=v,
                                k_rhs=k_rhs, v_rhs=v_rhs, qs=qs.astype(BF16), qd=(qs * e_mid[:, sl]).astype(BF16),
                                kd=(ks * e_rest[:, sl]).astype(BF16), e_tot=e_tot[:, sl]))
    if with_out:
        for d in ops:
            d["sc"] = _dot_nt(d["qs"], d["k_rhs"])
        for d in ops:
            mask = pair_causal[d["fw"]] if d["pair"] else _causal(d["fw"])
            d["sc"] = jnp.where(mask, d["sc"], 0.0).astype(BF16)
        for d in ops:
            d["o"] = _dot(d["sc"], d["v_rhs"])
    for d in ops:
        d["upd"] = _dot_tn(d["v"], d["kd"])
    for d in ops:
        if d["pair"]:
            d["upd"] = jnp.where(pair_state, d["upd"], 0.0)
    for di, (fw, _, _, _, _, _, o_ref, st_ref) in enumerate(dirs):
        sts = {}
        for c in (range(ncb) if fw else range(ncb - 1, -1, -1)):
            for d in ops:
                if d["di"] != di or d["c"] != c:
                    continue
                key = d["st_rows"].start
                if key not in sts:
                    sts[key] = st_ref[d["st_rows"], :]
                if with_out:
                    o = d["o"] + _dot_nt(d["qd"], sts[key].astype(BF16))
                    o_ref[d["rows"], d["vl"]] = o.astype(o_ref.dtype)
                sts[key] = sts[key] * d["e_tot"] + d["upd"]
        for d in ops:
            if d["di"] == di and d["c"] == 0:
                st_ref[d["st_rows"], :] = sts[d["st_rows"].start]


def _scan_kernel(n_ctx_blocks, *refs):
    s = pl.program_id(1)

    @pl.when(s == 0)
    def _():
        refs[-2][...] = jnp.zeros_like(refs[-2])
        refs[-1][...] = jnp.zeros_like(refs[-1])

    @pl.when(s < n_ctx_blocks)
    def _():
        _scan_block(False, *refs)

    @pl.when(s >= n_ctx_blocks)
    def _():
        _scan_block(True, *refs)


def _scan_call(q, v, akf, akb, bk, gf, gb, lc):
    b_, lt, _ = q.shape
    c_ = SCAN_BLOCK
    ncc = lc // c_
    nlc = (lt - lc) // c_
    n = ncc + nlc

    def fwd(b, s):
        return (b, s, 0)

    def bwd(b, s):
        return (b, jnp.where(s < ncc, ncc - 1 - s, n + ncc - 1 - s), 0)

    def out_fwd(b, s):
        return (b, jnp.maximum(s - ncc, 0), 0)

    def out_bwd(b, s):
        return (b, jnp.where(s < ncc, nlc - 1, n - 1 - s), 0)

    nb = 2 if b_ % 2 == 0 else 1

    def specs(index_map):
        return [pl.BlockSpec((nb, c_, a.shape[-1]), index_map) for a in (q, v, akf, bk, gf)]

    v_w = v.shape[-1]
    out = jax.ShapeDtypeStruct((b_, lt - lc, v_w), BF16)
    state = pltpu.VMEM((nb, v_w, HEAD_W), F32)
    return pl.pallas_call(
        functools.partial(_scan_kernel, ncc),
        grid=(b_ // nb, n),
        in_specs=specs(fwd) + specs(bwd),
        out_specs=[pl.BlockSpec((nb, c_, v_w), out_fwd), pl.BlockSpec((nb, c_, v_w), out_bwd)],
        out_shape=[out, out],
        scratch_shapes=[state, state],
        compiler_params=pltpu.CompilerParams(
            dimension_semantics=("arbitrary", "arbitrary"), vmem_limit_bytes=VMEM_LIMIT),
    )(q, v, akf, bk, gf, q, v, akb, bk, gb)


def _merge_kernel(x_ref, a1_ref, s1_ref, gt1_ref, a2_ref, s2_ref, of_ref, ob_ref, wg_ref, nw_ref,
                  wua_ref, wub_ref, wo_ref, gpost_ref, wr_ref, br_ref,
                  x1_ref, h2_ref, idx_ref, prob_ref):
    tm, d = x_ref.shape[1:]
    hr = tm // 2
    nslab = d // LANES
    ow = N_HEADS * HEAD_W
    aw = HGRN_HEADS * HEAD_W
    halves = [dict(i=i, rows=slice(i * hr, (i + 1) * hr)) for i in range(2)]
    for h in halves:
        h["x"] = x_ref[0, h["rows"], :]
        h["hb"] = (_rms(h["x"]) * a1_ref[0] + s1_ref[0]).astype(BF16)
    for h in halves:
        h["og"] = _dot(h["hb"], wg_ref[:, 0:ow])
        h["mg_a"] = _dot(h["hb"], wg_ref[:, ow:ow + d])
        h["mg_b"] = _dot(h["hb"], wg_ref[:, ow + d:ow + 2 * d])
    for h in halves:
        o = of_ref[0, h["rows"], :].astype(F32) + ob_ref[0, h["rows"], :].astype(F32)
        heads = [_rms(o[:, j * HEAD_W:(j + 1) * HEAD_W]) for j in range(N_HEADS)]
        og = h["og"]
        h["r"] = (jnp.concatenate(heads, axis=-1) * nw_ref[...] * (og * _sigmoid(og))).astype(BF16)
    for h in halves:
        h["y_a"] = _dot(h["r"][:, 0:aw], wua_ref[...])
        h["y_b"] = _dot(h["r"][:, aw:ow], wub_ref[...])
    for h in halves:
        h["m"] = (_sigmoid(h["mg_a"]) * h["y_a"] + _sigmoid(h["mg_b"]) * h["y_b"]).astype(BF16)
    for h in halves:
        h["mix"] = _dot(h["m"], wo_ref[...])
    for h in halves:
        x1 = h["x"] + gt1_ref[0] * (_rms(h["mix"]) * gpost_ref[...])
        x1_ref[0, h["rows"], :] = x1
        h2 = _rms(x1) * a2_ref[0] + s2_ref[0]
        for s in range(nslab):
            h2_ref[pl.ds(h["i"] * hr * nslab + s, hr, stride=nslab), :] = h2[:, s * LANES:(s + 1) * LANES]
        h["h_hi"], h["h_lo"] = _split_bf16(h2)
    for h in halves:
        h["logits"] = (_dot(h["h_hi"], wr_ref[0]) + _dot(h["h_lo"], wr_ref[0]) + _dot(h["h_hi"], wr_ref[1])
                       + br_ref[...])
    for h in halves:
        logits = h["logits"]
        lane = lax.broadcasted_iota(jnp.int32, logits.shape, 1).astype(F32)
        vals, idxs = [], []
        for _ in range(TOP_K):
            m = jnp.max(logits, axis=-1, keepdims=True)
            sel = jnp.min(jnp.where(logits == m, lane, float(LANES)), axis=-1, keepdims=True)
            vals.append(m)
            idxs.append(sel)
            logits = jnp.where(lane == sel, -jnp.inf, logits)
        exps = [jnp.exp(v_ - vals[0]) for v_ in vals]
        denom = exps[0] + exps[1] + exps[2] + exps[3]
        idx_out = jnp.zeros_like(lane)
        prob_out = jnp.zeros_like(lane)
        for k_ in range(TOP_K):
            idx_out = jnp.where(lane == float(k_), idxs[k_], idx_out)
            prob_out = jnp.where(lane == float(k_), exps[k_] / denom, prob_out)
        idx_ref[:, h["rows"]] = jnp.transpose(idx_out)[0:SUBLANES, :].astype(jnp.int32)
        prob_ref[0, h["rows"], :] = prob_out


def _merge_call(x, a1, s1, gt1, a2, s2, o_f, o_b, w_gates, norm_w, w_up_a, w_up_b, w_o, g_post, w_r, b_r):
    b_, l_, d = x.shape
    tm = LATENT_TILE
    tile = lambda b, t: (b, t, 0)
    per_b = lambda b, t: (b, 0, 0)
    const2 = lambda b, t: (0, 0)
    return pl.pallas_call(
        _merge_kernel,
        grid=(b_, l_ // tm),
        in_specs=[
            pl.BlockSpec((1, tm, d), tile),
            pl.BlockSpec((1, 1, d), per_b), pl.BlockSpec((1, 1, d), per_b), pl.BlockSpec((1, 1, d), per_b),
            pl.BlockSpec((1, 1, d), per_b), pl.BlockSpec((1, 1, d), per_b),
            pl.BlockSpec((1, tm, o_f.shape[-1]), tile), pl.BlockSpec((1, tm, o_b.shape[-1]), tile),
            pl.BlockSpec(w_gates.shape, const2),
            pl.BlockSpec(norm_w.shape, const2),
            pl.BlockSpec(w_up_a.shape, const2),
            pl.BlockSpec(w_up_b.shape, const2),
            pl.BlockSpec(w_o.shape, const2),
            pl.BlockSpec(g_post.shape, const2),
            pl.BlockSpec(w_r.shape, lambda b, t: (0, 0, 0)),
            pl.BlockSpec(b_r.shape, const2),
        ],
        out_specs=[
            pl.BlockSpec((1, tm, d), tile),
            pl.BlockSpec((tm * (d // LANES), LANES), lambda b, t: (b * (l_ // tm) + t, 0)),
            pl.BlockSpec((SUBLANES, tm), lambda b, t: (0, b * (l_ // tm) + t)), pl.BlockSpec((1, tm, LANES), tile),
        ],
        out_shape=[
            jax.ShapeDtypeStruct((b_, l_, d), F32), jax.ShapeDtypeStruct((b_ * l_ * (d // LANES), LANES), F32),
            jax.ShapeDtypeStruct((SUBLANES, b_ * l_), jnp.int32), jax.ShapeDtypeStruct((b_, l_, LANES), F32),
        ],
        compiler_params=pltpu.CompilerParams(
            dimension_semantics=("arbitrary", "arbitrary"), vmem_limit_bytes=VMEM_LIMIT),
    )(x, a1, s1, gt1, a2, s2, o_f, o_b, w_gates, norm_w, w_up_a, w_up_b, w_o, g_post, w_r, b_r)


def _moe_kernel(be_ref, nv_ref, tok_ref, tokn_ref, dst_ref, h2_hbm,
                wg_ref, bg_ref, wu_ref, bu_ref, wd_ref, bd_ref, out_hbm,
                xbuf, ybuf, wgb, wub, wdb, gsem, ssem):
    i = pl.program_id(0)
    n = pl.num_programs(0)
    slot = i % 2
    rows = MOE_BLOCK
    nslab = xbuf.shape[0] // (2 * rows)
    blk = rows * nslab

    def slab(ref, start):
        return ref.at[pl.ds(pl.multiple_of(start, nslab), nslab), :]

    def for_slot(fn):
        for s_static in range(2):
            @pl.when(slot == s_static)
            def _():
                fn(s_static)

    def issue_gather(idx_ref, s):
        for r in range(rows):
            pltpu.make_async_copy(slab(h2_hbm, idx_ref[0, 0, r]), xbuf.at[pl.ds(s * blk + r * nslab, nslab), :],
                                  gsem.at[s]).start(priority=r % 2)

    def wait_gather(s):
        pltpu.make_async_copy(h2_hbm.at[pl.ds(0, blk), :], xbuf.at[pl.ds(pl.multiple_of(s * blk, blk), blk), :],
                              gsem.at[s]).wait()

    def issue_scatter(s):
        for r in range(rows):
            pltpu.make_async_copy(ybuf.at[pl.ds(s * blk + r * nslab, nslab), :], slab(out_hbm, dst_ref[0, 0, r]),
                                  ssem.at[s]).start(priority=r % 2)

    def wait_scatter(s):
        pltpu.make_async_copy(ybuf.at[pl.ds(pl.multiple_of(s * blk, blk), blk), :], out_hbm.at[pl.ds(0, blk), :],
                              ssem.at[s]).wait()

    @pl.when(jnp.logical_and(i == 0, nv_ref[0] > 0))
    def _():
        issue_gather(tok_ref, 0)

    nxt = jnp.minimum(i + 1, n - 1)

    @pl.when(jnp.logical_and(i + 1 < n, nv_ref[nxt] > 0))
    def _():
        for_slot(lambda s: issue_gather(tokn_ref, 1 - s))

    prev = jnp.maximum(i - 1, 0)

    @pl.when(jnp.logical_or(i == 0, be_ref[i] != be_ref[prev]))
    def _():
        wgb[...] = wg_ref[0].astype(BF16)
        wub[...] = wu_ref[0].astype(BF16)
        wdb[...] = wd_ref[0].astype(BF16)

    @pl.when(jnp.logical_and(i >= 2, nv_ref[jnp.maximum(i - 2, 0)] > 0))
    def _():
        wait_scatter(slot)

    @pl.when(nv_ref[i] > 0)
    def _():
        wait_gather(slot)
        base = slot * blk
        xb = jnp.concatenate([xbuf[pl.ds(base + s, rows, stride=nslab), :] for s in range(nslab)],
                             axis=-1).astype(BF16)
        g = jnp.minimum(_dot(xb, wgb[...]) + bg_ref[0], SWIGLU_LIMIT)
        u = jnp.clip(_dot(xb, wub[...]) + bu_ref[0], -SWIGLU_LIMIT, SWIGLU_LIMIT)
        act = g * _sigmoid(SWIGLU_ALPHA * g) * (u + 1.0)
        y = _dot(act.astype(BF16), wdb[...]) + bd_ref[0]
        for s in range(nslab):
            ybuf[pl.ds(base + s, rows, stride=nslab), :] = y[:, s * LANES:(s + 1) * LANES]
        for_slot(issue_scatter)

    @pl.when(nv_ref[i] == 0)
    def _():
        ybuf[pl.ds(pl.multiple_of(slot * blk, blk), blk), :] = jnp.zeros((blk, LANES), F32)
        fill = pltpu.make_async_copy(ybuf.at[pl.ds(pl.multiple_of(slot * blk, blk), blk), :],
                                     out_hbm.at[pl.ds(pl.multiple_of(dst_ref[0, 0, 0], nslab), blk), :], ssem.at[slot])
        fill.start()
        fill.wait()

    @pl.when(i == n - 1)
    def _():
        @pl.when(jnp.logical_and(i >= 1, nv_ref[prev] > 0))
        def _():
            wait_scatter(1 - slot)

        @pl.when(nv_ref[i] > 0)
        def _():
            wait_scatter(slot)


def _moe_call(block_expert, n_valid, tok, dst, h2, n_out_rows, w_gate, b_gate, w_up, b_up, w_down, b_down):
    e_, d, f = w_gate.shape
    nslab = d // LANES
    nblk = tok.shape[0]
    rows = MOE_BLOCK

    def blk(i, be, nv):
        return (i, 0, 0)

    def blk_next(i, be, nv):
        return (jnp.minimum(i + 1, nblk - 1), 0, 0)

    def expert(i, be, nv):
        return (be[i], 0, 0)

    smem_rows = functools.partial(pl.BlockSpec, (1, 1, rows), memory_space=pltpu.SMEM)
    grid_spec = pltpu.PrefetchScalarGridSpec(
        num_scalar_prefetch=2,
        grid=(nblk,),
        in_specs=[
            smem_rows(blk), smem_rows(blk_next), smem_rows(blk),
            pl.BlockSpec(memory_space=pl.ANY),
            pl.BlockSpec((1, d, f), expert), pl.BlockSpec((1, 1, f), expert),
            pl.BlockSpec((1, d, f), expert), pl.BlockSpec((1, 1, f), expert),
            pl.BlockSpec((1, f, d), expert), pl.BlockSpec((1, 1, d), expert),
        ],
        out_specs=pl.BlockSpec(memory_space=pl.ANY),
        scratch_shapes=[
            pltpu.VMEM((2 * rows * nslab, LANES), F32), pltpu.VMEM((2 * rows * nslab, LANES), F32),
            pltpu.VMEM((d, f), BF16), pltpu.VMEM((d, f), BF16), pltpu.VMEM((f, d), BF16),
            pltpu.SemaphoreType.DMA((2,)), pltpu.SemaphoreType.DMA((2,)),
        ],
    )
    return pl.pallas_call(
        _moe_kernel,
        grid_spec=grid_spec,
        out_shape=jax.ShapeDtypeStruct((n_out_rows * nslab, LANES), F32),
        compiler_params=pltpu.CompilerParams(
            dimension_semantics=("arbitrary",), vmem_limit_bytes=VMEM_LIMIT),
    )(block_expert, n_valid, tok, tok, dst, h2,
      w_gate, b_gate.reshape(e_, 1, f), w_up, b_up.reshape(e_, 1, f), w_down, b_down.reshape(e_, 1, d))


def _dispatch_plan(top_idx, n_experts, nslab):
    t_ = top_idx.shape[1]
    n_pairs = t_ * TOP_K
    flat_e = top_idx.reshape(-1)
    pair_bits = max(n_pairs - 1, 1).bit_length()
    keys = lax.sort(flat_e * (1 << pair_bits) + jnp.arange(n_pairs, dtype=jnp.int32), is_stable=False)
    order = keys & ((1 << pair_bits) - 1)
    counts = jnp.sum((flat_e[:, None] == jnp.arange(n_experts, dtype=jnp.int32)[None, :]).astype(jnp.int32), axis=0)
    nblk_e = (counts + MOE_BLOCK - 1) // MOE_BLOCK
    blk_end = jnp.cumsum(nblk_e)
    blk_start = blk_end - nblk_e
    start = jnp.cumsum(counts) - counts
    nblk = n_pairs // MOE_BLOCK + n_experts
    b = jnp.arange(nblk, dtype=jnp.int32)
    be = jnp.minimum(jnp.sum((b[:, None] >= blk_end[None, :]).astype(jnp.int32), axis=1), n_experts - 1)
    first_row = (b - blk_start[be]) * MOE_BLOCK
    n_valid = jnp.clip(counts[be] - first_row, 0, MOE_BLOCK).astype(jnp.int32)
    r = jnp.arange(MOE_BLOCK, dtype=jnp.int32)[None, :]
    valid = r < n_valid[:, None]
    src = jnp.clip(start[be][:, None] + first_row[:, None] + r, 0, n_pairs - 1)
    pair = jnp.where(valid, order[src], 0).astype(jnp.int32)
    tok = pair % t_
    spare = (jnp.cumsum((~valid).reshape(-1).astype(jnp.int32)) - 1).reshape(nblk, MOE_BLOCK)
    dst = jnp.where(valid, pair, n_pairs + spare).astype(jnp.int32)
    n_out_rows = n_pairs + n_experts * MOE_BLOCK
    return (be, n_valid, (tok * nslab).reshape(nblk, 1, MOE_BLOCK), (dst * nslab).reshape(nblk, 1, MOE_BLOCK),
            n_out_rows)


def _final_kernel(x1_ref, y0_ref, y1_ref, y2_ref, y3_ref, p_ref, gt2_ref, gpost_ref, o_ref):
    p = p_ref[0]
    tm, d = x1_ref.shape[1:]
    nslab = d // LANES
    parts = []
    for s in range(nslab):
        acc = p[:, 0:1] * y0_ref[pl.ds(s, tm, stride=nslab), :]
        for k_, y_ref in ((1, y1_ref), (2, y2_ref), (3, y3_ref)):
            acc = acc + p[:, k_:k_ + 1] * y_ref[pl.ds(s, tm, stride=nslab), :]
        parts.append(acc)
    ffn = jnp.concatenate(parts, axis=-1)
    o_ref[0] = x1_ref[0] + gt2_ref[0] * (_rms(ffn) * gpost_ref[...])


def _final_call(x1, y, probs, gt2, g_post):
    b_, l_, d = x1.shape
    tm = LATENT_TILE
    nslab = d // LANES
    tiles_per_k = b_ * l_ // tm
    tile = lambda b, t: (b, t, 0)

    def y_spec(k_):
        return pl.BlockSpec((tm * nslab, LANES), lambda b, t: (k_ * tiles_per_k + b * (l_ // tm) + t, 0))

    return pl.pallas_call(
        _final_kernel,
        grid=(b_, l_ // tm),
        in_specs=[
            pl.BlockSpec((1, tm, d), tile),
            y_spec(0), y_spec(1), y_spec(2), y_spec(3),
            pl.BlockSpec((1, tm, LANES), tile),
            pl.BlockSpec((1, 1, d), lambda b, t: (b, 0, 0)),
            pl.BlockSpec(g_post.shape, lambda b, t: (0, 0)),
        ],
        out_specs=pl.BlockSpec((1, tm, d), tile),
        out_shape=jax.ShapeDtypeStruct((b_, l_, d), F32),
        compiler_params=pltpu.CompilerParams(
            dimension_semantics=("arbitrary", "arbitrary"), vmem_limit_bytes=VMEM_LIMIT),
    )(x1, y, y, y, y, probs, gt2, g_post)


def kernel(x, c, ctx, c_ctx, w_ada, b_ada, g_pre_mix, g_post_mix, g_pre_ffn, g_post_ffn, w_in, hgrn_lb, hgrn_norm_w, gla_gk_w2, gla_gk_b, gla_norm_w, w_up_a, w_up_b, w_o, w_router, b_router, w_gate, b_gate, w_up, b_up, w_down, b_down):
    b_, l_, d = x.shape
    lc = ctx.shape[1]
    n_experts = w_router.shape[-1]
    assert w_ada.shape[0] == 1, "one layer: the context stream is never updated"
    assert l_ % LATENT_TILE == 0 and lc % SCAN_BLOCK == 0 and (b_ * l_ * TOP_K) % MOE_BLOCK == 0
    assert n_experts <= LANES and d % LANES == 0
    layer = 0
    hw = HGRN_HEADS * HGRN_DK
    kw = GLA_HEADS * GLA_DK
    vw = GLA_HEADS * GLA_DV
    rk = GLA_GATE_RANK

    rows = -(-(b_ + 1) // SUBLANES) * SUBLANES
    cc = jnp.concatenate([c, c_ctx[None, :], jnp.zeros((rows - b_ - 1, d), F32)], axis=0)
    mod = _ada_call(cc, w_ada[layer], b_ada[layer])[:b_ + 1]
    sh1, sc1, gt1, sh2, sc2, gt2 = [m.reshape(b_ + 1, 1, d) for m in jnp.split(mod, 6, axis=-1)]
    a1 = g_pre_mix[layer] * (1.0 + sc1)
    a2 = g_pre_ffn[layer] * (1.0 + sc2)

    w = w_in[layer]
    o0 = 0
    cols = {}
    for name, size in (("qa", hw), ("zf", hw), ("zb", hw), ("ia", hw), ("oga", hw), ("qb", kw), ("kb", kw),
                       ("vb", vw), ("rf", rk), ("rb", rk), ("ogb", vw), ("mga", d), ("mgb", d)):
        cols[name] = w[:, o0:o0 + size]
        o0 += size
    w_scan = jnp.concatenate([
        cols["qa"], cols["zf"], cols["zb"], cols["ia"],
        cols["qb"] * (GLA_DK ** -0.5), cols["kb"], cols["vb"]], axis=1).astype(BF16)
    w_rank = jnp.pad(jnp.concatenate([cols["rf"], cols["rb"]], axis=1), ((0, 0), (0, LANES - 2 * rk))).astype(BF16)
    w2 = jnp.zeros((LANES, 2 * kw), F32)
    w2 = w2.at[0:rk, 0:kw].set(gla_gk_w2[layer, 0])
    w2 = w2.at[rk:2 * rk, kw:].set(gla_gk_w2[layer, 1])
    w2 = w2.astype(BF16)
    gkb = jnp.concatenate([gla_gk_b[layer, 0], gla_gk_b[layer, 1]])[None, :]
    lb = jnp.cumsum(jax.nn.softmax(hgrn_lb.astype(F32), axis=0), axis=0)[layer]
    w_gates = jnp.concatenate([cols["oga"], cols["ogb"], cols["mga"], cols["mgb"]], axis=1).astype(BF16)
    norm_w = jnp.concatenate([jnp.tile(hgrn_norm_w[layer], HGRN_HEADS), jnp.tile(gla_norm_w[layer], GLA_HEADS)])[None, :]
    w_r = jnp.stack(_split_bf16(jnp.pad(w_router[layer], ((0, 0), (0, LANES - n_experts)))))
    b_r = jnp.pad(b_router[layer], (0, LANES - n_experts), constant_values=-1e30)[None, :]

    q, v, akf, akb, bk, gf, gb = _inproj_call(x, ctx, a1, sh1, w_scan, w_rank, w2, gkb, lb)
    o_f, o_b = _scan_call(q, v, akf, akb, bk, gf, gb, lc)
    x1, h2, top_idx, probs = _merge_call(
        x, a1, sh1, gt1, a2, sh2, o_f, o_b, w_gates, norm_w,
        w_up_a[layer].astype(BF16), w_up_b[layer].astype(BF16), w_o[layer].astype(BF16),
        g_post_mix[layer][None, :], w_r, b_r)

    t_ = b_ * l_
    be, n_valid, tok, dst, n_out_rows = _dispatch_plan(top_idx[:TOP_K], n_experts, d // LANES)
    y = _moe_call(be, n_valid, tok, dst, h2, n_out_rows,
                  w_gate[layer], b_gate[layer], w_up[layer], b_up[layer], w_down[layer], b_down[layer])
    return _final_call(x1, y, probs, gt2, g_post_ffn[layer][None, :])
```
